```python
import jax, jax.numpy as jnp
from jax import lax
import numpy as np

D_MODEL = 1024
BATCH = 4
SEQ = 8192
DEPTH = 1

HEAD_DIM = 64
ATTN_PATTERNS = ((128, 1), (512, 4), (2048, 16))
N_PATTERNS = 3
HEADS_PER_PATTERN = 4
N_ATTN_HEADS = N_PATTERNS * HEADS_PER_PATTERN
ATTN_WIDTH = N_ATTN_HEADS * HEAD_DIM
ATTN_OUT_WIDTH = HEADS_PER_PATTERN * HEAD_DIM
N_KEYS = ATTN_PATTERNS[0][0] // ATTN_PATTERNS[0][1] + 1
Q_BLOCK = 128
REL_BUCKETS = 32
REL_MAX_DIST = 2048
SSM_GROUP = 16
SSM_STATE = 64
SSM_WIDTH = 512
SSM_GROUPS = SSM_WIDTH // SSM_GROUP
DT_MIN = 1e-3
DT_MAX = 1e-1
MEM_LEN = 256
XATTN_HEADS = 4
XATTN_HEAD_DIM = 128
XATTN_WIDTH = XATTN_HEADS * XATTN_HEAD_DIM
N_BRANCHES = 3
IN_WIDTH = SSM_WIDTH + 3 * ATTN_WIDTH + XATTN_WIDTH + N_BRANCHES * D_MODEL
D_FF = 2816
EPS = 1e-6

kernel_name = "hybrid_s5_dilated_attn_gated_block"


def rmsnorm(x, g):
    xf = x.astype(jnp.float32)
    y = xf * lax.rsqrt(jnp.mean(xf * xf, axis=-1, keepdims=True) + EPS)
    return (y * g.astype(jnp.float32)).astype(x.dtype)


def swiglu_ffn(x, w_in, w_down):
    a, b = jnp.split(x @ w_in, 2, axis=-1)
    return (jax.nn.silu(a) * b) @ w_down


def t5_causal_buckets(dist):
    dist = np.asarray(dist, np.int32)
    max_exact = REL_BUCKETS // 2
    safe = np.maximum(dist, 1).astype(np.float32)
    large = max_exact + (np.log(safe / max_exact) / np.log(REL_MAX_DIST / max_exact)
                         * (REL_BUCKETS - max_exact)).astype(np.int32)
    large = np.minimum(large, REL_BUCKETS - 1)
    return np.where(dist < max_exact, dist, large).astype(np.int32)


def pattern_offsets():
    return np.stack([np.arange(N_KEYS, dtype=np.int32) * d for (_, d) in ATTN_PATTERNS])


def s5_scan(u, a_re, a_im, log_dt, b_re, b_im, c_re, c_im, d_skip):
    bsz, seqlen, _ = u.shape
    f32 = jnp.float32
    uf = u.astype(f32).reshape(bsz, seqlen, SSM_GROUPS, SSM_GROUP)
    a_re = a_re.astype(f32); a_im = a_im.astype(f32)
    dt = jnp.exp(log_dt.astype(f32))[:, None]
    mag = jnp.exp(a_re * dt)
    ang = a_im * dt
    abar_re = mag * jnp.cos(ang)
    abar_im = mag * jnp.sin(ang)
    nr = abar_re - 1.0
    ni = abar_im
    den = a_re * a_re + a_im * a_im
    coef_re = (nr * a_re + ni * a_im) / den
    coef_im = (ni * a_re - nr * a_im) / den
    b_re = b_re.astype(f32); b_im = b_im.astype(f32)
    bbar_re = coef_re[..., None] * b_re - coef_im[..., None] * b_im
    bbar_im = coef_re[..., None] * b_im + coef_im[..., None] * b_re
    bu_re = jnp.einsum('blgh,gph->blgp', uf, bbar_re)
    bu_im = jnp.einsum('blgh,gph->blgp', uf, bbar_im)
    a_seq_re = jnp.broadcast_to(abar_re[None, None], (1, seqlen, SSM_GROUPS, SSM_STATE))
    a_seq_im = jnp.broadcast_to(abar_im[None, None], (1, seqlen, SSM_GROUPS, SSM_STATE))

    def combine(e1, e2):
        a1r, a1i, b1r, b1i = e1
        a2r, a2i, b2r, b2i = e2
        return (a2r * a1r - a2i * a1i,
                a2r * a1i + a2i * a1r,
                a2r * b1r - a2i * b1i + b2r,
                a2r * b1i + a2i * b1r + b2i)

    _, _, x_re, x_im = lax.associative_scan(combine, (a_seq_re, a_seq_im, bu_re, bu_im), axis=1)
    y = (jnp.einsum('blgp,ghp->blgh', x_re, c_re.astype(f32))
         - jnp.einsum('blgp,ghp->blgh', x_im, c_im.astype(f32))
         + uf * d_skip.astype(f32))
    return y.reshape(bsz, seqlen, SSM_WIDTH).astype(u.dtype)


def dilated_mixture_attention(q, k, v, rel_table):
    bsz, seqlen = q.shape[0], q.shape[1]
    f32 = jnp.float32
    offsets = pattern_offsets()
    buckets = t5_causal_buckets(offsets)
    table = rel_table.astype(f32).reshape(REL_BUCKETS, N_PATTERNS, HEADS_PER_PATTERN)
    grp = np.arange(N_PATTERNS)[:, None]
    bias = jnp.transpose(table[buckets, grp], (0, 2, 1))
    grp3 = np.arange(N_PATTERNS)[:, None, None]
    scale = HEAD_DIM ** -0.5

    def block(start):
        qb = lax.dynamic_slice_in_dim(q, start, Q_BLOCK, axis=1).astype(f32)
        pos = start + jnp.arange(Q_BLOCK, dtype=jnp.int32)
        idx = pos[None, :, None] - offsets[:, None, :]
        valid = idx >= 0
        idx = jnp.maximum(idx, 0)
        kb = k[:, idx, grp3].astype(f32)
        vb = v[:, idx, grp3].astype(f32)
        logits = jnp.einsum('bqghd,bgqkhd->bghqk', qb, kb) * scale + bias[None, :, :, None, :]
        logits = jnp.where(valid[None, :, None], logits, -jnp.inf)
        lse = jax.nn.logsumexp(logits, axis=-1)
        probs = jnp.exp(logits - lse[..., None])
        out = jnp.einsum('bghqk,bgqkhd->bqghd', probs, vb)
        mix = jax.nn.softmax(lse, axis=1)
        return jnp.einsum('bghq,bqghd->bqhd', mix, out).astype(q.dtype)

    starts = jnp.arange(seqlen // Q_BLOCK, dtype=jnp.int32) * Q_BLOCK
    out = lax.map(block, starts)
    return out.transpose(1, 0, 2, 3, 4).reshape(bsz, seqlen, ATTN_OUT_WIDTH)


def memory_cross_attention(xq, mem_n, w_kv):
    bsz, seqlen, _ = xq.shape
    f32 = jnp.float32
    mk, mv = jnp.split(mem_n @ w_kv, 2, axis=-1)
    mk = mk.reshape(bsz, -1, XATTN_HEADS, XATTN_HEAD_DIM).astype(f32)
    mv = mv.reshape(bsz, -1, XATTN_HEADS, XATTN_HEAD_DIM).astype(f32)
    qh = xq.reshape(bsz, seqlen, XATTN_HEADS, XATTN_HEAD_DIM).astype(f32)
    logits = jnp.einsum('bqhd,bkhd->bhqk', qh, mk) * (XATTN_HEAD_DIM ** -0.5)
    probs = jax.nn.softmax(logits, axis=-1)
    out = jnp.einsum('bhqk,bkhd->bqhd', probs, mv)
    return out.reshape(bsz, seqlen, XATTN_WIDTH).astype(xq.dtype)


def hybrid_layer(x, mem, rel_table, ffn1_norm, ffn1_w_in, ffn1_w_down, mix_norm, w_in,
                 ssm_a_re, ssm_a_im, ssm_log_dt, ssm_b_re, ssm_b_im, ssm_c_re, ssm_c_im, ssm_d, ssm_w_glu,
                 attn_w_up, mem_norm, xattn_w_kv, xattn_w_up, w_out, ffn2_norm, ffn2_w_in, ffn2_w_down):
    bsz, seqlen, _ = x.shape
    x = x + 0.5 * swiglu_ffn(rmsnorm(x, ffn1_norm), ffn1_w_in, ffn1_w_down)
    u = rmsnorm(x, mix_norm)
    proj = u @ w_in
    cuts = np.cumsum([SSM_WIDTH, ATTN_WIDTH, ATTN_WIDTH, ATTN_WIDTH, XATTN_WIDTH, D_MODEL, D_MODEL]).tolist()
    u_ssm, q, k, v, xq, g_ssm, g_attn, g_mem = jnp.split(proj, cuts, axis=-1)
    y_ssm = jax.nn.gelu(s5_scan(u_ssm, ssm_a_re, ssm_a_im, ssm_log_dt, ssm_b_re, ssm_b_im,
                                ssm_c_re, ssm_c_im, ssm_d))
    glu_a, glu_b = jnp.split(y_ssm @ ssm_w_glu, 2, axis=-1)
    br_ssm = glu_a * jax.nn.sigmoid(glu_b)
    hshape = (bsz, seqlen, N_PATTERNS, HEADS_PER_PATTERN, HEAD_DIM)
    br_attn = dilated_mixture_attention(q.reshape(hshape), k.reshape(hshape), v.reshape(hshape),
                                        rel_table) @ attn_w_up
    br_mem = memory_cross_attention(xq, rmsnorm(mem, mem_norm), xattn_w_kv) @ xattn_w_up
    merged = (jax.nn.sigmoid(g_ssm) * br_ssm + jax.nn.sigmoid(g_attn) * br_attn
              + jax.nn.sigmoid(g_mem) * br_mem)
    x = x + merged @ w_out
    x = x + 0.5 * swiglu_ffn(rmsnorm(x, ffn2_norm), ffn2_w_in, ffn2_w_down)
    return x


def setup_inputs(seed: int = 0) -> dict:
    key = jax.random.key(seed)
    ks = jax.random.split(key, 32)
    f32 = jnp.float32

    def nrm(k, shape, fan_in):
        return jax.random.normal(k, shape, f32) * (fan_in ** -0.5)

    def gain(k, shape):
        return 1.0 + 0.02 * jax.random.normal(k, shape, f32)

    L = DEPTH
    a_im_init = jnp.pi * jnp.arange(SSM_STATE, dtype=f32)
    return {
        "x": jax.random.normal(ks[0], (BATCH, SEQ, D_MODEL), f32),
        "mem": jax.random.normal(ks[1], (BATCH, MEM_LEN, D_MODEL), f32),
        "ffn1_norm": gain(ks[2], (L, D_MODEL)),
        "ffn1_w_in": nrm(ks[3], (L, D_MODEL, 2 * D_FF), D_MODEL),
        "ffn1_w_down": nrm(ks[4], (L, D_FF, D_MODEL), D_FF),
        "mix_norm": gain(ks[5], (L, D_MODEL)),
        "w_in": nrm(ks[6], (L, D_MODEL, IN_WIDTH), D_MODEL),
        "ssm_a_re": -0.5 + 0.01 * jax.random.normal(ks[7], (L, SSM_GROUPS, SSM_STATE), f32),
        "ssm_a_im": a_im_init + 0.01 * jax.random.normal(ks[8], (L, SSM_GROUPS, SSM_STATE), f32),
        "ssm_log_dt": jax.random.uniform(ks[9], (L, SSM_GROUPS), f32,
                                         float(np.log(DT_MIN)), float(np.log(DT_MAX))),
        "ssm_b_re": nrm(ks[10], (L, SSM_GROUPS, SSM_STATE, SSM_GROUP), 2 * SSM_GROUP),
        "ssm_b_im": nrm(ks[11], (L, SSM_GROUPS, SSM_STATE, SSM_GROUP), 2 * SSM_GROUP),
        "ssm_c_re": nrm(ks[12], (L, SSM_GROUPS, SSM_GROUP, SSM_STATE), SSM_STATE),
        "ssm_c_im": nrm(ks[13], (L, SSM_GROUPS, SSM_GROUP, SSM_STATE), SSM_STATE),
        "ssm_d": jax.random.normal(ks[14], (L, SSM_GROUPS, SSM_GROUP), f32),
        "ssm_w_glu": nrm(ks[15], (L, SSM_WIDTH, 2 * D_MODEL), SSM_WIDTH),
        "rel_table": 0.5 * jax.random.normal(ks[16], (REL_BUCKETS, N_ATTN_HEADS), f32),
        "attn_w_up": nrm(ks[17], (L, ATTN_OUT_WIDTH, D_MODEL), ATTN_OUT_WIDTH),
        "mem_norm": gain(ks[18], (L, D_MODEL)),
        "xattn_w_kv": nrm(ks[19], (L, D_MODEL, 2 * XATTN_WIDTH), D_MODEL),
        "xattn_w_up": nrm(ks[20], (L, XATTN_WIDTH, D_MODEL), XATTN_WIDTH),
        "w_out": nrm(ks[21], (L, D_MODEL, D_MODEL), D_MODEL),
        "ffn2_norm": gain(ks[22], (L, D_MODEL)),
        "ffn2_w_in": nrm(ks[23], (L, D_MODEL, 2 * D_FF), D_MODEL),
        "ffn2_w_down": nrm(ks[24], (L, D_FF, D_MODEL), D_FF),
        "final_norm": gain(ks[25], (D_MODEL,)),
    }


def reference(x, mem, ffn1_norm, ffn1_w_in, ffn1_w_down, mix_norm, w_in,
              ssm_a_re, ssm_a_im, ssm_log_dt, ssm_b_re, ssm_b_im, ssm_c_re, ssm_c_im, ssm_d, ssm_w_glu,
              rel_table, attn_w_up, mem_norm, xattn_w_kv, xattn_w_up, w_out,
              ffn2_norm, ffn2_w_in, ffn2_w_down, final_norm):
    h = x
    for l in range(DEPTH):
        h = hybrid_layer(h, mem, rel_table, ffn1_norm[l], ffn1_w_in[l], ffn1_w_down[l], mix_norm[l], w_in[l],
                         ssm_a_re[l], ssm_a_im[l], ssm_log_dt[l], ssm_b_re[l], ssm_b_im[l],
                         ssm_c_re[l], ssm_c_im[l], ssm_d[l], ssm_w_glu[l],
                         attn_w_up[l], mem_norm[l], xattn_w_kv[l], xattn_w_up[l], w_out[l],
                         ffn2_norm[l], ffn2_w_in[l], ffn2_w_down[l])
    return rmsnorm(h, final_norm)
```

```python
import functools

import numpy as np
import jax
import jax.numpy as jnp
from jax import lax
from jax.experimental import pallas as pl
from jax.experimental.pallas import tpu as pltpu

F32 = jnp.float32
BF16 = jnp.bfloat16

D_MODEL = 1024
D_FF = 2816
EPS = 1e-6

HEAD_DIM = 64
ATTN_PATTERNS = ((128, 1), (512, 4), (2048, 16))
N_PATTERNS = 3
HEADS_PER_PATTERN = 4
N_ATTN_HEADS = N_PATTERNS * HEADS_PER_PATTERN
ATTN_WIDTH = N_ATTN_HEADS * HEAD_DIM
PATTERN_WIDTH = HEADS_PER_PATTERN * HEAD_DIM
WINDOW_KEYS = 128
REL_BUCKETS = 32
REL_MAX_DIST = 2048

SSM_GROUP = 16
SSM_STATE = 64
SSM_WIDTH = 512
SSM_GROUPS = SSM_WIDTH // SSM_GROUP
SSM_CHUNK = 16
SSM_ROW = SSM_CHUNK * SSM_GROUP

MEM_LEN = 256
XATTN_HEADS = 4
XATTN_HEAD_DIM = 128
XATTN_WIDTH = XATTN_HEADS * XATTN_HEAD_DIM

PRE_WIDTH = SSM_WIDTH + 3 * ATTN_WIDTH
POST_WIDTH = XATTN_WIDTH + 3 * D_MODEL

NEG_BIG = -1e30

VMEM_LIMIT_BYTES = 56 * 1024 * 1024

TM_FFN = 512
FFN_CHUNK = 1408
TM_PROJ = 512
TM_MERGE = 256
TQ_ATTN = 512
SUB_Q = 128


def _rmsnorm(x, g):
    return x * lax.rsqrt(jnp.mean(x * x, axis=-1, keepdims=True) + EPS) * g


def _dot(a, b):
    return jnp.dot(a, b, preferred_element_type=F32)


def _dot_nt(a, b):
    return lax.dot_general(a, b, (((1,), (1,)), ((), ())), preferred_element_type=F32)


def _resident(shape):
    nd = len(shape)
    return pl.BlockSpec(shape, lambda *_: (0,) * nd, pipeline_mode=pl.Buffered(1))


def _params(n_axes):
    return pltpu.CompilerParams(dimension_semantics=("arbitrary",) * n_axes,
                                vmem_limit_bytes=VMEM_LIMIT_BYTES)


def _ffn_kernel(x_ref, g_ref, win_ref, wdn_ref, *rest, final_norm):
    if final_norm:
        fg_ref, o_ref = rest
    else:
        (o_ref,) = rest
    x = x_ref[...]
    un = _rmsnorm(x, g_ref[...]).astype(BF16)
    acc = None
    for c in range(D_FF // FFN_CHUNK):
        lo = c * FFN_CHUNK
        a = _dot(un, win_ref[:, lo:lo + FFN_CHUNK])
        b = _dot(un, win_ref[:, D_FF + lo:D_FF + lo + FFN_CHUNK])
        h = (a * jax.nn.sigmoid(a) * b).astype(BF16)
        part = _dot(h, wdn_ref[lo:lo + FFN_CHUNK, :])
        acc = part if acc is None else acc + part
    y = x + 0.5 * acc
    if final_norm:
        y = _rmsnorm(y, fg_ref[...])
    o_ref[...] = y


def _ffn(x, norm_g, w_in, w_down, final_g=None):
    n = x.shape[0]
    row = pl.BlockSpec((TM_FFN, D_MODEL), lambda i: (i, 0))
    in_specs = [row, _resident((1, D_MODEL)), _resident((D_MODEL, 2 * D_FF)), _resident((D_FF, D_MODEL))]
    args = [x, norm_g.reshape(1, D_MODEL), w_in.astype(BF16), w_down.astype(BF16)]
    if final_g is not None:
        in_specs.append(_resident((1, D_MODEL)))
        args.append(final_g.reshape(1, D_MODEL))
    return pl.pallas_call(
        functools.partial(_ffn_kernel, final_norm=final_g is not None),
        grid=(n // TM_FFN,),
        in_specs=in_specs,
        out_specs=row,
        out_shape=jax.ShapeDtypeStruct((n, D_MODEL), F32),
        compiler_params=_params(1),
        name="ffn_final" if final_g is not None else "ffn",
    )(*args)


def _proj_kernel(x_ref, g_ref, w_ref, u_ref, q_ref, k_ref, v_ref):
    un = _rmsnorm(x_ref[...], g_ref[...]).astype(BF16)
    u_ref[...] = _dot(un, w_ref[:, :SSM_WIDTH])
    lo = SSM_WIDTH
    q_ref[...] = (_dot(un, w_ref[:, lo:lo + ATTN_WIDTH]) * (HEAD_DIM ** -0.5)).astype(BF16)
    lo += ATTN_WIDTH
    k_ref[...] = _dot(un, w_ref[:, lo:lo + ATTN_WIDTH]).astype(BF16)
    lo += ATTN_WIDTH
    v_ref[...] = _dot(un, w_ref[:, lo:lo + ATTN_WIDTH]).astype(BF16)


def _proj(x, norm_g, w_pre):
    n = x.shape[0]
    rows = lambda w: pl.BlockSpec((TM_PROJ, w), lambda i: (i, 0))
    return pl.pallas_call(
        _proj_kernel,
        grid=(n // TM_PROJ,),
        in_specs=[rows(D_MODEL), _resident((1, D_MODEL)), _resident((D_MODEL, PRE_WIDTH))],
        out_specs=[rows(SSM_WIDTH), rows(ATTN_WIDTH), rows(ATTN_WIDTH), rows(ATTN_WIDTH)],
        out_shape=[jax.ShapeDtypeStruct((n, SSM_WIDTH), F32)] + [jax.ShapeDtypeStruct((n, ATTN_WIDTH), BF16)] * 3,
        compiler_params=_params(1),
        name="proj",
    )(x, norm_g.reshape(1, D_MODEL), w_pre.astype(BF16))


def _ssm_matrices(a_re, a_im, log_dt, b_re, b_im, c_re, c_im, d_skip):
    t_len = SSM_CHUNK
    dt = jnp.exp(log_dt)[:, None]
    mag = jnp.exp(a_re * dt)
    ang = a_im * dt
    abar_re = mag * jnp.cos(ang)
    abar_im = mag * jnp.sin(ang)
    nr = abar_re - 1.0
    ni = abar_im
    den = a_re * a_re + a_im * a_im
    coef_re = (nr * a_re + ni * a_im) / den
    coef_im = (ni * a_re - nr * a_im) / den
    bbar_re = coef_re[..., None] * b_re - coef_im[..., None] * b_im
    bbar_im = coef_re[..., None] * b_im + coef_im[..., None] * b_re
    pw_re = [jnp.ones_like(abar_re)]
    pw_im = [jnp.zeros_like(abar_re)]
    for _ in range(t_len):
        pr, pi = pw_re[-1], pw_im[-1]
        pw_re.append(pr * abar_re - pi * abar_im)
        pw_im.append(pr * abar_im + pi * abar_re)
    pw_re = jnp.stack(pw_re)
    pw_im = jnp.stack(pw_im)
    hp = lax.Precision.HIGHEST
    cp_re = c_re[None] * pw_re[:, :, None, :] - c_im[None] * pw_im[:, :, None, :]
    cp_im = c_re[None] * pw_im[:, :, None, :] + c_im[None] * pw_re[:, :, None, :]
    kern = (jnp.einsum("jghp,gpk->jghk", cp_re[:t_len], bbar_re, precision=hp)
            - jnp.einsum("jghp,gpk->jghk", cp_im[:t_len], bbar_im, precision=hp))
    lag = np.arange(t_len)[None, :] - np.arange(t_len)[:, None]
    toep = kern[np.maximum(lag, 0)]
    toep = jnp.where((lag >= 0)[:, :, None, None, None], toep, 0.0)
    toep = toep.transpose(2, 0, 4, 1, 3).reshape(SSM_GROUPS, SSM_ROW, SSM_ROW)
    rev_re = pw_re[t_len - 1::-1][:t_len]
    rev_im = pw_im[t_len - 1::-1][:t_len]
    in_re = rev_re[..., None] * bbar_re[None] - rev_im[..., None] * bbar_im[None]
    in_im = rev_re[..., None] * bbar_im[None] + rev_im[..., None] * bbar_re[None]
    in_re = in_re.transpose(1, 0, 3, 2).reshape(SSM_GROUPS, SSM_ROW, SSM_STATE)
    in_im = in_im.transpose(1, 0, 3, 2).reshape(SSM_GROUPS, SSM_ROW, SSM_STATE)
    into = jnp.concatenate([in_re, in_im, in_im, in_re], axis=-1)
    out_re = cp_re[1:].transpose(1, 3, 0, 2).reshape(SSM_GROUPS, SSM_STATE, SSM_ROW)
    out_im = cp_im[1:].transpose(1, 3, 0, 2).reshape(SSM_GROUPS, SSM_STATE, SSM_ROW)
    outof = jnp.concatenate([out_re, -out_im], axis=1)
    ar, ai = pw_re[t_len], pw_im[t_len]
    zero = jnp.zeros_like(ar)
    decay = jnp.stack([jnp.concatenate([ar, ar], -1), jnp.concatenate([-ai, ai], -1),
                       jnp.concatenate([ai, -ai], -1)] + [jnp.concatenate([zero, zero], -1)] * 5, axis=1)
    skip = jnp.tile(d_skip, (1, t_len)).reshape(SSM_GROUPS, 1, SSM_ROW)
    return toep.astype(BF16), into.astype(BF16), outof.astype(BF16), decay, skip


def _ssm_kernel(u_ref, toep_ref, into_ref, outof_ref, decay_ref, skip_ref, y_ref, loc_ref, locsw_ref, prev_ref,
                *, n_seq, n_chunks):
    u = u_ref[0]
    ub = u.astype(BF16)
    loc_both = _dot(ub, into_ref[0])
    loc_ref[...] = loc_both[:, :2 * SSM_STATE]
    locsw_ref[...] = loc_both[:, 2 * SSM_STATE:]
    p1 = jnp.broadcast_to(decay_ref[0, 0:1, :], (n_seq, 2 * SSM_STATE))
    p2 = jnp.broadcast_to(decay_ref[0, 1:2, :], (n_seq, 2 * SSM_STATE))
    p2s = jnp.broadcast_to(decay_ref[0, 2:3, :], (n_seq, 2 * SSM_STATE))

    def step(c, carry):
        s, ssw = carry
        prev_ref[pl.ds(c, n_seq, stride=n_chunks), :] = s
        s_new = p1 * s + p2 * ssw + loc_ref[pl.ds(c, n_seq, stride=n_chunks), :]
        ssw_new = p1 * ssw + p2s * s + locsw_ref[pl.ds(c, n_seq, stride=n_chunks), :]
        return s_new, ssw_new

    zero = jnp.zeros((n_seq, 2 * SSM_STATE), F32)
    lax.fori_loop(0, n_chunks, step, (zero, zero), unroll=8)
    y_ref[0] = (_dot(ub, toep_ref[0]) + _dot(prev_ref[...].astype(BF16), outof_ref[0])
                + u * skip_ref[0])


def _ssm(u_groups, mats, n_seq):
    toep, into, outof, decay, skip = mats
    g, rows, _ = u_groups.shape
    n_chunks = rows // n_seq
    per_group = lambda *shape: pl.BlockSpec((1,) + shape, lambda i: (i, 0, 0))
    return pl.pallas_call(
        functools.partial(_ssm_kernel, n_seq=n_seq, n_chunks=n_chunks),
        grid=(g,),
        in_specs=[per_group(rows, SSM_ROW), per_group(SSM_ROW, SSM_ROW), per_group(SSM_ROW, 4 * SSM_STATE),
                  per_group(2 * SSM_STATE, SSM_ROW), per_group(8, 2 * SSM_STATE), per_group(1, SSM_ROW)],
        out_specs=per_group(rows, SSM_ROW),
        out_shape=jax.ShapeDtypeStruct((g, rows, SSM_ROW), F32),
        scratch_shapes=[pltpu.VMEM((rows, 2 * SSM_STATE), F32)] * 3,
        compiler_params=_params(1),
        name="ssm",
    )(u_groups, toep, into, outof, decay, skip)


def _t5_buckets(dist):
    dist = np.asarray(dist, np.int32)
    max_exact = REL_BUCKETS // 2
    safe = np.maximum(dist, 1).astype(np.float32)
    large = max_exact + (np.log(safe / max_exact) / np.log(REL_MAX_DIST / max_exact)
                         * (REL_BUCKETS - max_exact)).astype(np.int32)
    large = np.minimum(large, REL_BUCKETS - 1)
    return np.where(dist < max_exact, dist, large).astype(np.int32)


def _band_bias(rel_table, pattern):
    dilation = ATTN_PATTERNS[pattern][1]
    steps = (np.arange(SUB_Q)[:, None] + SUB_Q) - np.arange(2 * SUB_Q)[None, :]
    valid = (steps >= 0) & (steps <= WINDOW_KEYS)
    buckets = _t5_buckets(np.clip(steps, 0, WINDOW_KEYS) * dilation)
    table = rel_table[:, pattern * HEADS_PER_PATTERN:(pattern + 1) * HEADS_PER_PATTERN]
    bias = jnp.transpose(table[buckets], (2, 0, 1))
    bias = jnp.where(valid[None], bias, NEG_BIG)
    first = jnp.where((np.arange(2 * SUB_Q) >= SUB_Q)[None, None, :], bias, NEG_BIG)
    return jnp.stack([bias, first])


def _attn_kernel(q_ref, kc_ref, kh_ref, vc_ref, vh_ref, bias_ref, o_ref):
    first = jnp.where(pl.program_id(2) == 0, 1, 0)
    lane = lax.broadcasted_iota(jnp.int32, (SUB_Q, 2 * HEAD_DIM), 1)
    low_half = lane < HEAD_DIM
    for j in range(TQ_ATTN // SUB_Q):
        rows = slice(j * SUB_Q, (j + 1) * SUB_Q)
        q = q_ref[0, rows, :]
        if j == 0:
            k = jnp.concatenate([kh_ref[0], kc_ref[0, rows, :]], axis=0)
            v = jnp.concatenate([vh_ref[0], vc_ref[0, rows, :]], axis=0)
        else:
            k = kc_ref[0, (j - 1) * SUB_Q:(j + 1) * SUB_Q, :]
            v = vc_ref[0, (j - 1) * SUB_Q:(j + 1) * SUB_Q, :]
        for pair in range(HEADS_PER_PATTERN // 2):
            cols = slice(pair * 2 * HEAD_DIM, (pair + 1) * 2 * HEAD_DIM)
            q2, k2, v2 = q[:, cols], k[:, cols], v[:, cols]
            outs, lses = [], []
            for sub in range(2):
                head = 2 * pair + sub
                keep = low_half if sub == 0 else jnp.logical_not(low_half)
                qh = jnp.where(keep, q2, jnp.zeros_like(q2))
                s = _dot_nt(qh, k2)
                s = s + (bias_ref[first, head] if j == 0 else bias_ref[0, head])
                m = jnp.max(s, axis=-1, keepdims=True)
                p = jnp.exp(s - m)
                l = jnp.sum(p, axis=-1, keepdims=True)
                outs.append(_dot(p.astype(BF16), v2) / l)
                lses.append(jnp.broadcast_to(m + jnp.log(l), (SUB_Q, 2 * HEAD_DIM)))
            o_ref[0, rows, cols] = jnp.where(low_half, outs[0], outs[1])
            o_ref[0, rows, PATTERN_WIDTH + pair * 2 * HEAD_DIM:PATTERN_WIDTH + (pair + 1) * 2 * HEAD_DIM] = (
                jnp.where(low_half, lses[0], lses[1]))


def _dilated_attention(q, k, v, rel_table, pattern, bsz, seqlen):
    d = ATTN_PATTERNS[pattern][1]
    res_len = seqlen // d
    shape = (bsz, res_len, d * ATTN_WIDTH)
    q3, k3, v3 = q.reshape(shape), k.reshape(shape), v.reshape(shape)
    col = lambda r: r * N_PATTERNS + pattern
    cur = pl.BlockSpec((1, TQ_ATTN, PATTERN_WIDTH), lambda b, r, i: (b, i, col(r)))
    halo = pl.BlockSpec((1, SUB_Q, PATTERN_WIDTH),
                        lambda b, r, i: (b, jnp.maximum(i * (TQ_ATTN // SUB_Q) - 1, 0), col(r)))
    out = pl.pallas_call(
        _attn_kernel,
        grid=(bsz, d, res_len // TQ_ATTN),
        in_specs=[cur, cur, halo, cur, halo, _resident((2, HEADS_PER_PATTERN, SUB_Q, 2 * SUB_Q))],
        out_specs=pl.BlockSpec((1, TQ_ATTN, 2 * PATTERN_WIDTH), lambda b, r, i: (b, i, r)),
        out_shape=jax.ShapeDtypeStruct((bsz, res_len, d * 2 * PATTERN_WIDTH), F32),
        compiler_params=_params(3),
        name=f"attn_d{d}",
    )(q3, k3, k3, v3, v3, _band_bias(rel_table, pattern))
    return out.reshape(bsz * seqlen, 2 * PATTERN_WIDTH)


def _memkv_kernel(mem_ref, g_ref, w_ref, k_ref, v_ref):
    mn = _rmsnorm(mem_ref[...], g_ref[...]).astype(BF16)
    k_ref[...] = _dot(mn, w_ref[:, :XATTN_WIDTH]).astype(BF16)
    v_ref[...] = _dot(mn, w_ref[:, XATTN_WIDTH:]).astype(BF16)


def _memkv(mem2d, norm_g, w_kv):
    rows = mem2d.shape[0]
    full = lambda *shape: pl.BlockSpec(shape, lambda i: (0,) * len(shape))
    return pl.pallas_call(
        _memkv_kernel,
        grid=(1,),
        in_specs=[full(rows, D_MODEL), full(1, D_MODEL), full(D_MODEL, 2 * XATTN_WIDTH)],
        out_specs=[full(rows, XATTN_WIDTH), full(rows, XATTN_WIDTH)],
        out_shape=[jax.ShapeDtypeStruct((rows, XATTN_WIDTH), BF16)] * 2,
        compiler_params=_params(1),
        name="memkv",
    )(mem2d, norm_g.reshape(1, D_MODEL), w_kv.astype(BF16))


def _merge_kernel(x_ref, y_ref, o0_ref, o1_ref, o2_ref, mk_ref, mv_ref, g_ref, wpost_ref, wglu_ref,
                  wau_ref, wxu_ref, wout_ref, out_ref):
    x = x_ref[...]
    un = _rmsnorm(x, g_ref[...]).astype(BF16)

    def gate(index):
        lo = XATTN_WIDTH + index * D_MODEL
        return jax.nn.sigmoid(_dot(un, wpost_ref[:, lo:lo + D_MODEL]))

    ys = jax.nn.gelu(y_ref[...]).astype(BF16)
    glu = _dot(ys, wglu_ref[:, :D_MODEL]) * jax.nn.sigmoid(_dot(ys, wglu_ref[:, D_MODEL:]))
    merged = gate(0) * glu

    outs = [r[:, :PATTERN_WIDTH] for r in (o0_ref, o1_ref, o2_ref)]
    lses = [r[:, PATTERN_WIDTH:] for r in (o0_ref, o1_ref, o2_ref)]
    top = jnp.maximum(jnp.maximum(lses[0], lses[1]), lses[2])
    ws = [jnp.exp(l - top) for l in lses]
    att = (ws[0] * outs[0] + ws[1] * outs[1] + ws[2] * outs[2]) / (ws[0] + ws[1] + ws[2])
    merged = merged + gate(1) * _dot(att.astype(BF16), wau_ref[...])

    xq = _dot(un, wpost_ref[:, :XATTN_WIDTH]).astype(BF16)
    heads = []
    for h in range(XATTN_HEADS):
        cols = slice(h * XATTN_HEAD_DIM, (h + 1) * XATTN_HEAD_DIM)
        s = _dot_nt(xq[:, cols], mk_ref[0, :, cols]) * (XATTN_HEAD_DIM ** -0.5)
        p = jnp.exp(s - jnp.max(s, axis=-1, keepdims=True))
        heads.append(_dot(p.astype(BF16), mv_ref[0, :, cols]) / jnp.sum(p, axis=-1, keepdims=True))
    xo = jnp.concatenate(heads, axis=-1).astype(BF16)
    merged = merged + gate(2) * _dot(xo, wxu_ref[...])

    out_ref[...] = x + _dot(merged.astype(BF16), wout_ref[...])


def _merge(x, y_ssm, attn_outs, mk, mv, norm_g, w_post, w_glu, w_au, w_xu, w_out, seqlen):
    n = x.shape[0]
    tiles_per_seq = seqlen // TM_MERGE
    rows = lambda w: pl.BlockSpec((TM_MERGE, w), lambda i: (i, 0))
    mem = pl.BlockSpec((1, MEM_LEN, XATTN_WIDTH), lambda i: (i // tiles_per_seq, 0, 0))
    return pl.pallas_call(
        _merge_kernel,
        grid=(n // TM_MERGE,),
        in_specs=[rows(D_MODEL), rows(SSM_WIDTH)] + [rows(2 * PATTERN_WIDTH)] * 3 + [mem, mem,
                  _resident((1, D_MODEL)), _resident((D_MODEL, POST_WIDTH)), _resident((SSM_WIDTH, 2 * D_MODEL)),
                  _resident((PATTERN_WIDTH, D_MODEL)), _resident((XATTN_WIDTH, D_MODEL)),
                  _resident((D_MODEL, D_MODEL))],
        out_specs=rows(D_MODEL),
        out_shape=jax.ShapeDtypeStruct((n, D_MODEL), F32),
        compiler_params=_params(1),
        name="merge",
    )(x, y_ssm, *attn_outs, mk, mv, norm_g.reshape(1, D_MODEL), w_post.astype(BF16), w_glu.astype(BF16),
      w_au.astype(BF16), w_xu.astype(BF16), w_out.astype(BF16))


def _layer(x, mem, rel_table, ffn1_norm, ffn1_w_in, ffn1_w_down, mix_norm, w_in,
           ssm_a_re, ssm_a_im, ssm_log_dt, ssm_b_re, ssm_b_im, ssm_c_re, ssm_c_im, ssm_d, ssm_w_glu,
           attn_w_up, mem_norm, xattn_w_kv, xattn_w_up, w_out, ffn2_norm, ffn2_w_in, ffn2_w_down,
           final_norm, bsz, seqlen):
    n = bsz * seqlen
    x = _ffn(x, ffn1_norm, ffn1_w_in, ffn1_w_down)
    u, q, k, v = _proj(x, mix_norm, w_in[:, :PRE_WIDTH])

    n_rows = n // SSM_CHUNK
    u_groups = u.reshape(n_rows, SSM_CHUNK, SSM_GROUPS, SSM_GROUP).transpose(2, 0, 1, 3)
    u_groups = u_groups.reshape(SSM_GROUPS, n_rows, SSM_ROW)
    mats = _ssm_matrices(ssm_a_re, ssm_a_im, ssm_log_dt, ssm_b_re, ssm_b_im, ssm_c_re, ssm_c_im, ssm_d)
    y_groups = _ssm(u_groups, mats, bsz)
    y_ssm = y_groups.reshape(SSM_GROUPS, n_rows, SSM_CHUNK, SSM_GROUP).transpose(1, 2, 0, 3)
    y_ssm = y_ssm.reshape(n, SSM_WIDTH)

    attn_outs = [_dilated_attention(q, k, v, rel_table, g, bsz, seqlen) for g in range(N_PATTERNS)]
    mk, mv = _memkv(mem.reshape(bsz * MEM_LEN, D_MODEL), mem_norm, xattn_w_kv)
    mk = mk.reshape(bsz, MEM_LEN, XATTN_WIDTH)
    mv = mv.reshape(bsz, MEM_LEN, XATTN_WIDTH)
    x = _merge(x, y_ssm, attn_outs, mk, mv, mix_norm, w_in[:, PRE_WIDTH:], ssm_w_glu, attn_w_up,
               xattn_w_up, w_out, seqlen)
    return _ffn(x, ffn2_norm, ffn2_w_in, ffn2_w_down, final_g=final_norm)


def kernel(x, mem, ffn1_norm, ffn1_w_in, ffn1_w_down, mix_norm, w_in, ssm_a_re, ssm_a_im, ssm_log_dt,
           ssm_b_re, ssm_b_im, ssm_c_re, ssm_c_im, ssm_d, ssm_w_glu, rel_table, attn_w_up, mem_norm,
           xattn_w_kv, xattn_w_up, w_out, ffn2_norm, ffn2_w_in, ffn2_w_down, final_norm):
    bsz, seqlen, _ = x.shape
    assert ffn1_norm.shape[0] == 1, "single-layer trunk"
    assert seqlen % (TQ_ATTN * ATTN_PATTERNS[-1][1]) == 0 and seqlen % TM_MERGE == 0
    h = _layer(x.reshape(bsz * seqlen, D_MODEL), mem, rel_table, ffn1_norm[0], ffn1_w_in[0], ffn1_w_down[0],
               mix_norm[0], w_in[0], ssm_a_re[0], ssm_a_im[0], ssm_log_dt[0], ssm_b_re[0], ssm_b_im[0],
               ssm_c_re[0], ssm_c_im[0], ssm_d[0], ssm_w_glu[0], attn_w_up[0], mem_norm[0], xattn_w_kv[0],
               xattn_w_up[0], w_out[0], ffn2_norm[0], ffn2_w_in[0], ffn2_w_down[0], final_norm,
               bsz, seqlen)
    return h.reshape(bsz, seqlen, D_MODEL)
```

```python
import functools

import numpy as np
import jax
import jax.numpy as jnp
from jax import lax
from jax.experimental import pallas as pl
from jax.experimental.pallas import tpu as pltpu

F32 = jnp.float32
BF16 = jnp.bfloat16

LANES = 128

D_MODEL = 1024
D_FF = 2816
EPS = 1e-6

HEAD_DIM = 64
ATTN_PATTERNS = ((128, 1), (512, 4), (2048, 16))
N_PATTERNS = 3
HEADS_PER_PATTERN = 4
N_ATTN_HEADS = N_PATTERNS * HEADS_PER_PATTERN
ATTN_WIDTH = N_ATTN_HEADS * HEAD_DIM
PATTERN_WIDTH = HEADS_PER_PATTERN * HEAD_DIM
WINDOW_KEYS = 128
REL_BUCKETS = 32
REL_MAX_DIST = 2048

SSM_GROUP = 16
SSM_STATE = 64
SSM_WIDTH = 512
SSM_GROUPS = SSM_WIDTH // SSM_GROUP
SSM_CHUNK = 16
SSM_OCT = LANES // SSM_GROUP
SSM_N_OCT = SSM_WIDTH // LANES
SSM_OCT_IN = SSM_CHUNK * LANES
SSM_OCT_STATE = SSM_OCT * SSM_STATE
SSM_ROW = SSM_CHUNK * SSM_WIDTH

MEM_LEN = 256
XATTN_HEADS = 4
XATTN_HEAD_DIM = 128
XATTN_WIDTH = XATTN_HEADS * XATTN_HEAD_DIM

PRE_WIDTH = SSM_WIDTH + 3 * ATTN_WIDTH
POST_WIDTH = XATTN_WIDTH + 3 * D_MODEL

NEG_BIG = -1e30

VMEM_LIMIT_BYTES = 56 * 1024 * 1024

TM_FFN = 512
FFN_CHUNK = 1408
TM_PROJ = 512
TM_MERGE = 256
TQ_ATTN = 512
SUB_Q = 128


def _rmsnorm(x, g):
    return x * lax.rsqrt(jnp.mean(x * x, axis=-1, keepdims=True) + EPS) * g


def _dot(a, b):
    return jnp.dot(a, b, preferred_element_type=F32)


def _dot_nt(a, b):
    return lax.dot_general(a, b, (((1,), (1,)), ((), ())), preferred_element_type=F32)


def _resident(shape):
    nd = len(shape)
    return pl.BlockSpec(shape, lambda *_: (0,) * nd, pipeline_mode=pl.Buffered(1))


def _params(n_axes):
    return pltpu.CompilerParams(dimension_semantics=("arbitrary",) * n_axes,
                                vmem_limit_bytes=VMEM_LIMIT_BYTES)


def _ffn_kernel(x_ref, g_ref, win_ref, wdn_ref, *rest, final_norm):
    if final_norm:
        fg_ref, o_ref = rest
    else:
        (o_ref,) = rest
    x = x_ref[...]
    un = _rmsnorm(x, g_ref[...]).astype(BF16)
    acc = None
    for c in range(D_FF // FFN_CHUNK):
        lo = c * FFN_CHUNK
        a = _dot(un, win_ref[:, lo:lo + FFN_CHUNK])
        b = _dot(un, win_ref[:, D_FF + lo:D_FF + lo + FFN_CHUNK])
        h = (a * jax.nn.sigmoid(a) * b).astype(BF16)
        part = _dot(h, wdn_ref[lo:lo + FFN_CHUNK, :])
        acc = part if acc is None else acc + part
    y = x + 0.5 * acc
    if final_norm:
        y = _rmsnorm(y, fg_ref[...])
    o_ref[...] = y


def _ffn(x, norm_g, w_in, w_down, final_g=None):
    n = x.shape[0]
    row = pl.BlockSpec((TM_FFN, D_MODEL), lambda i: (i, 0))
    in_specs = [row, _resident((1, D_MODEL)), _resident((D_MODEL, 2 * D_FF)), _resident((D_FF, D_MODEL))]
    args = [x, norm_g.reshape(1, D_MODEL), w_in.astype(BF16), w_down.astype(BF16)]
    if final_g is not None:
        in_specs.append(_resident((1, D_MODEL)))
        args.append(final_g.reshape(1, D_MODEL))
    return pl.pallas_call(
        functools.partial(_ffn_kernel, final_norm=final_g is not None),
        grid=(n // TM_FFN,),
        in_specs=in_specs,
        out_specs=row,
        out_shape=jax.ShapeDtypeStruct((n, D_MODEL), F32),
        compiler_params=_params(1),
        name="ffn_final" if final_g is not None else "ffn",
    )(*args)


def _proj_kernel(x_ref, g_ref, w_ref, u_ref, *rest):
    qkv_refs, (su_ref, sq_ref, sk_ref, sv_ref) = rest[:3 * N_PATTERNS], rest[3 * N_PATTERNS:]
    un = _rmsnorm(x_ref[...], g_ref[...]).astype(BF16)

    u = _dot(un, w_ref[:, :SSM_WIDTH])
    for o in range(SSM_N_OCT):
        su_ref[o] = u[:, o * LANES:(o + 1) * LANES]
    chunk_rows = TM_PROJ // SSM_CHUNK
    for o in range(SSM_N_OCT):
        for t in range(SSM_CHUNK):
            lo = o * SSM_OCT_IN + t * LANES
            u_ref[:, lo:lo + LANES] = su_ref[o, pl.ds(t, chunk_rows, stride=SSM_CHUNK), :].astype(BF16)

    for which, scr in enumerate((sq_ref, sk_ref, sv_ref)):
        lo = SSM_WIDTH + which * ATTN_WIDTH
        res = _dot(un, w_ref[:, lo:lo + ATTN_WIDTH])
        if which == 0:
            res = res * (HEAD_DIM ** -0.5)
        for j in range(ATTN_WIDTH // LANES):
            scr[j] = res[:, j * LANES:(j + 1) * LANES]
        for g, (_, d) in enumerate(ATTN_PATTERNS):
            out = qkv_refs[which * N_PATTERNS + g]
            for jj in range(PATTERN_WIDTH // LANES):
                j = g * (PATTERN_WIDTH // LANES) + jj
                for r in range(d):
                    out[0, r, :, jj * LANES:(jj + 1) * LANES] = (
                        scr[j, pl.ds(r, TM_PROJ // d, stride=d), :].astype(BF16))


def _proj(x, norm_g, w_pre, bsz, seqlen):
    n = x.shape[0]
    tiles_per_seq = seqlen // TM_PROJ
    rows = lambda w: pl.BlockSpec((TM_PROJ, w), lambda i: (i, 0))
    qkv_specs, qkv_shapes = [], []
    for _ in range(3):
        for _, d in ATTN_PATTERNS:
            qkv_specs.append(pl.BlockSpec((1, d, TM_PROJ // d, PATTERN_WIDTH),
                                          lambda i: (i // tiles_per_seq, 0, i % tiles_per_seq, 0)))
            qkv_shapes.append(jax.ShapeDtypeStruct((bsz, d, seqlen // d, PATTERN_WIDTH), BF16))
    slabs = lambda k: pltpu.VMEM((k, TM_PROJ, LANES), F32)
    outs = pl.pallas_call(
        _proj_kernel,
        grid=(n // TM_PROJ,),
        in_specs=[rows(D_MODEL), _resident((1, D_MODEL)), _resident((D_MODEL, PRE_WIDTH))],
        out_specs=[pl.BlockSpec((TM_PROJ // SSM_CHUNK, SSM_ROW), lambda i: (i, 0))] + qkv_specs,
        out_shape=[jax.ShapeDtypeStruct((n // SSM_CHUNK, SSM_ROW), BF16)] + qkv_shapes,
        scratch_shapes=[slabs(SSM_N_OCT)] + [slabs(ATTN_WIDTH // LANES)] * 3,
        compiler_params=_params(1),
        name="proj",
    )(x, norm_g.reshape(1, D_MODEL), w_pre.astype(BF16))
    u = outs[0]
    q, k, v = (outs[1 + i * N_PATTERNS:1 + (i + 1) * N_PATTERNS] for i in range(3))
    return u, q, k, v


def _ssm_matrices(a_re, a_im, log_dt, b_re, b_im, c_re, c_im, d_skip):
    t_len = SSM_CHUNK
    dt = jnp.exp(log_dt)[:, None]
    mag = jnp.exp(a_re * dt)
    ang = a_im * dt
    abar_re = mag * jnp.cos(ang)
    abar_im = mag * jnp.sin(ang)
    nr = abar_re - 1.0
    ni = abar_im
    den = a_re * a_re + a_im * a_im
    coef_re = (nr * a_re + ni * a_im) / den
    coef_im = (ni * a_re - nr * a_im) / den
    bbar_re = coef_re[..., None] * b_re - coef_im[..., None] * b_im
    bbar_im = coef_re[..., None] * b_im + coef_im[..., None] * b_re
    pw_re = [jnp.ones_like(abar_re)]
    pw_im = [jnp.zeros_like(abar_re)]
    for _ in range(t_len):
        pr, pi = pw_re[-1], pw_im[-1]
        pw_re.append(pr * abar_re - pi * abar_im)
        pw_im.append(pr * abar_im + pi * abar_re)
    rev_re = jnp.stack(pw_re[t_len - 1::-1])
    rev_im = jnp.stack(pw_im[t_len - 1::-1])
    pw_re = jnp.stack(pw_re)
    pw_im = jnp.stack(pw_im)
    hp = lax.Precision.HIGHEST
    n_oct, oct_, grp, st = SSM_N_OCT, SSM_OCT, SSM_GROUP, SSM_STATE
    eye = jnp.eye(oct_, dtype=F32)
    cp_re = c_re[None] * pw_re[:, :, None, :] - c_im[None] * pw_im[:, :, None, :]
    cp_im = c_re[None] * pw_im[:, :, None, :] + c_im[None] * pw_re[:, :, None, :]
    kern = (jnp.einsum("jghp,gpk->jghk", cp_re[:t_len], bbar_re, precision=hp)
            - jnp.einsum("jghp,gpk->jghk", cp_im[:t_len], bbar_im, precision=hp))
    kern = kern.at[0].add(d_skip[:, :, None] * jnp.eye(grp, dtype=F32)[None])
    lag = np.arange(t_len)[None, :] - np.arange(t_len)[:, None]
    toep = jnp.where((lag >= 0)[:, :, None, None, None], kern[np.maximum(lag, 0)], 0.0)
    toep = toep.reshape(t_len, t_len, n_oct, oct_, grp, grp).transpose(2, 0, 3, 5, 1, 4)
    toep = toep[:, :, :, :, :, None, :] * eye[None, None, :, None, None, :, None]
    toep = toep.reshape(n_oct, SSM_OCT_IN, SSM_OCT_IN)
    in_re = rev_re[..., None] * bbar_re[None] - rev_im[..., None] * bbar_im[None]
    in_im = rev_re[..., None] * bbar_im[None] + rev_im[..., None] * bbar_re[None]

    def into_layout(m):
        m = m.reshape(t_len, n_oct, oct_, st, grp).transpose(1, 0, 2, 4, 3)
        m = m[:, :, :, :, None, :] * eye[None, None, :, None, :, None]
        return m.reshape(n_oct, SSM_OCT_IN, SSM_OCT_STATE)

    into = jnp.concatenate([into_layout(in_re), into_layout(in_im)], axis=-1)

    def outof_layout(m):
        m = m.reshape(t_len, n_oct, oct_, grp, st).transpose(1, 2, 4, 0, 3)
        m = m[:, :, :, :, None, :] * eye[None, :, None, None, :, None]
        return m.reshape(n_oct, SSM_OCT_STATE, SSM_OCT_IN)

    outof = jnp.concatenate([outof_layout(cp_re[1:]), -outof_layout(cp_im[1:])], axis=1)
    decay = jnp.stack([pw_re[t_len].reshape(n_oct, SSM_OCT_STATE), pw_im[t_len].reshape(n_oct, SSM_OCT_STATE)]
                      + [jnp.zeros((n_oct, SSM_OCT_STATE), F32)] * 6, axis=1)
    return toep.astype(BF16), into.astype(BF16), outof.astype(BF16), decay


def _ssm_kernel(x_ref, toep_ref, into_ref, outof_ref, decay_ref, y_ref, loc_ref, prev_ref, *, n_chunks):
    x = x_ref[...]
    loc_ref[...] = _dot(x, into_ref[0])
    ar = decay_ref[0, 0:1, :]
    ai = decay_ref[0, 1:2, :]

    def step(c, carry):
        s_re, s_im = carry
        row = pl.ds(c, 1)
        prev_ref[row, :SSM_OCT_STATE] = s_re
        prev_ref[row, SSM_OCT_STATE:] = s_im
        n_re = ar * s_re - ai * s_im + loc_ref[row, :SSM_OCT_STATE]
        n_im = ar * s_im + ai * s_re + loc_ref[row, SSM_OCT_STATE:]
        return n_re, n_im

    zero = jnp.zeros((1, SSM_OCT_STATE), F32)
    lax.fori_loop(0, n_chunks, step, (zero, zero), unroll=8)
    y = _dot(x, toep_ref[0]) + _dot(prev_ref[...].astype(BF16), outof_ref[0])
    y_ref[...] = jax.nn.gelu(y).astype(BF16)


def _ssm(u_rows, mats, bsz):
    toep, into, outof, decay = mats
    n_chunks = u_rows.shape[0] // bsz
    per_oct = lambda *shape: pl.BlockSpec((1,) + shape, lambda o, b: (o, 0, 0), pipeline_mode=pl.Buffered(1))
    rows = pl.BlockSpec((n_chunks, SSM_OCT_IN), lambda o, b: (b, o))
    return pl.pallas_call(
        functools.partial(_ssm_kernel, n_chunks=n_chunks),
        grid=(SSM_N_OCT, bsz),
        in_specs=[rows, per_oct(SSM_OCT_IN, SSM_OCT_IN), per_oct(SSM_OCT_IN, 2 * SSM_OCT_STATE),
                  per_oct(2 * SSM_OCT_STATE, SSM_OCT_IN), per_oct(8, SSM_OCT_STATE)],
        out_specs=rows,
        out_shape=jax.ShapeDtypeStruct(u_rows.shape, BF16),
        scratch_shapes=[pltpu.VMEM((n_chunks, 2 * SSM_OCT_STATE), F32)] * 2,
        compiler_params=_params(2),
        name="ssm",
    )(u_rows, toep, into, outof, decay)


def _t5_buckets(dist):
    dist = np.asarray(dist, np.int32)
    max_exact = REL_BUCKETS // 2
    safe = np.maximum(dist, 1).astype(np.float32)
    large = max_exact + (np.log(safe / max_exact) / np.log(REL_MAX_DIST / max_exact)
                         * (REL_BUCKETS - max_exact)).astype(np.int32)
    large = np.minimum(large, REL_BUCKETS - 1)
    return np.where(dist < max_exact, dist, large).astype(np.int32)


def _band_bias(rel_table, pattern):
    dilation = ATTN_PATTERNS[pattern][1]
    buckets = _t5_buckets(np.arange(WINDOW_KEYS + 1) * dilation)
    table = rel_table[:, pattern * HEADS_PER_PATTERN:(pattern + 1) * HEADS_PER_PATTERN].T
    runs = []
    for b in np.unique(buckets):
        runs.append(jnp.broadcast_to(table[:, b:b + 1], (HEADS_PER_PATTERN, int(np.sum(buckets == b)))))
    per_step = jnp.concatenate(runs, axis=1)
    period = 3 * SUB_Q
    masked = lambda n: jnp.full((HEADS_PER_PATTERN, n), NEG_BIG, F32)
    w = jnp.concatenate([masked(SUB_Q - 1), per_step[:, ::-1], masked(period - SUB_Q - WINDOW_KEYS)], axis=1)
    skew = jnp.tile(w, (1, SUB_Q))[:, :SUB_Q * (period - 1)].reshape(HEADS_PER_PATTERN, SUB_Q, period - 1)
    bias = skew[:, :, SUB_Q - 1:3 * SUB_Q - 1]
    first = jnp.where((np.arange(2 * SUB_Q) >= SUB_Q)[None, None, :], bias, NEG_BIG)
    return jnp.stack([bias, first])


def _attn_kernel(q_ref, kc_ref, kh_ref, vc_ref, vh_ref, bias_ref, o_ref):
    first = jnp.where(pl.program_id(2) == 0, 1, 0)
    lane = lax.broadcasted_iota(jnp.int32, (SUB_Q, 2 * HEAD_DIM), 1)
    low_half = lane < HEAD_DIM
    for j in range(TQ_ATTN // SUB_Q):
        rows = slice(j * SUB_Q, (j + 1) * SUB_Q)
        q = q_ref[0, 0, rows, :]
        if j == 0:
            k = jnp.concatenate([kh_ref[0, 0], kc_ref[0, 0, rows, :]], axis=0)
            v = jnp.concatenate([vh_ref[0, 0], vc_ref[0, 0, rows, :]], axis=0)
        else:
            k = kc_ref[0, 0, (j - 1) * SUB_Q:(j + 1) * SUB_Q, :]
            v = vc_ref[0, 0, (j - 1) * SUB_Q:(j + 1) * SUB_Q, :]
        for pair in range(HEADS_PER_PATTERN // 2):
            cols = slice(pair * 2 * HEAD_DIM, (pair + 1) * 2 * HEAD_DIM)
            q2, k2, v2 = q[:, cols], k[:, cols], v[:, cols]
            outs, lses = [], []
            for sub in range(2):
                head = 2 * pair + sub
                keep = low_half if sub == 0 else jnp.logical_not(low_half)
                qh = jnp.where(keep, q2, jnp.zeros_like(q2))
                s = _dot_nt(qh, k2)
                s = s + (bias_ref[first, head] if j == 0 else bias_ref[0, head])
                m = jnp.max(s, axis=-1, keepdims=True)
                p = jnp.exp(s - m)
                l = jnp.sum(p, axis=-1, keepdims=True)
                outs.append(_dot(p.astype(BF16), v2) / l)
                lses.append(jnp.broadcast_to(m + jnp.log(l), (SUB_Q, 2 * HEAD_DIM)))
            o_ref[0, 0, rows, cols] = jnp.where(low_half, outs[0], outs[1])
            o_ref[0, 0, rows, PATTERN_WIDTH + pair * 2 * HEAD_DIM:PATTERN_WIDTH + (pair + 1) * 2 * HEAD_DIM] = (
                jnp.where(low_half, lses[0], lses[1]))


def _dilated_attention(q, k, v, rel_table, pattern):
    bsz, d, res_len, _ = q.shape
    cur = pl.BlockSpec((1, 1, TQ_ATTN, PATTERN_WIDTH), lambda b, r, i: (b, r, i, 0))
    halo = pl.BlockSpec((1, 1, SUB_Q, PATTERN_WIDTH),
                        lambda b, r, i: (b, r, jnp.maximum(i * (TQ_ATTN // SUB_Q) - 1, 0), 0))
    return pl.pallas_call(
        _attn_kernel,
        grid=(bsz, d, res_len // TQ_ATTN),
        in_specs=[cur, cur, halo, cur, halo, _resident((2, HEADS_PER_PATTERN, SUB_Q, 2 * SUB_Q))],
        out_specs=pl.BlockSpec((1, 1, TQ_ATTN, 2 * PATTERN_WIDTH), lambda b, r, i: (b, r, i, 0)),
        out_shape=jax.ShapeDtypeStruct((bsz, d, res_len, 2 * PATTERN_WIDTH), F32),
        compiler_params=_params(3),
        name=f"attn_d{d}",
    )(q, k, k, v, v, _band_bias(rel_table, pattern))


def _memkv_kernel(mem_ref, g_ref, w_ref, k_ref, v_ref):
    mn = _rmsnorm(mem_ref[...], g_ref[...]).astype(BF16)
    k_ref[...] = _dot(mn, w_ref[:, :XATTN_WIDTH]).astype(BF16)
    v_ref[...] = _dot(mn, w_ref[:, XATTN_WIDTH:]).astype(BF16)


def _memkv(mem2d, norm_g, w_kv):
    rows = mem2d.shape[0]
    full = lambda *shape: pl.BlockSpec(shape, lambda i: (0,) * len(shape))
    return pl.pallas_call(
        _memkv_kernel,
        grid=(1,),
        in_specs=[full(rows, D_MODEL), full(1, D_MODEL), full(D_MODEL, 2 * XATTN_WIDTH)],
        out_specs=[full(rows, XATTN_WIDTH), full(rows, XATTN_WIDTH)],
        out_shape=[jax.ShapeDtypeStruct((rows, XATTN_WIDTH), BF16)] * 2,
        compiler_params=_params(1),
        name="memkv",
    )(mem2d, norm_g.reshape(1, D_MODEL), w_kv.astype(BF16))


def _merge_kernel(x_ref, y_ref, o0_ref, o1_ref, o2_ref, mk_ref, mv_ref, g_ref, wpost_ref, wglu_ref,
                  wau_ref, wxu_ref, wout_ref, out_ref, ys_ref, s1_ref, s2_ref):
    x = x_ref[...]
    un = _rmsnorm(x, g_ref[...]).astype(BF16)

    def gate(index):
        lo = XATTN_WIDTH + index * D_MODEL
        return jax.nn.sigmoid(_dot(un, wpost_ref[:, lo:lo + D_MODEL]))

    chunk_rows = TM_MERGE // SSM_CHUNK
    for o in range(SSM_N_OCT):
        for t in range(SSM_CHUNK):
            lo = o * SSM_OCT_IN + t * LANES
            ys_ref[o, pl.ds(t, chunk_rows, stride=SSM_CHUNK), :] = y_ref[:, lo:lo + LANES].astype(F32)
    ys = jnp.concatenate([ys_ref[o] for o in range(SSM_N_OCT)], axis=-1).astype(BF16)
    glu = _dot(ys, wglu_ref[:, :D_MODEL]) * jax.nn.sigmoid(_dot(ys, wglu_ref[:, D_MODEL:]))
    merged = gate(0) * glu

    slabs = 2 * PATTERN_WIDTH // LANES
    for o_ref, s_ref in ((o1_ref, s1_ref), (o2_ref, s2_ref)):
        d = o_ref.shape[1]
        for r in range(d):
            for j in range(slabs):
                s_ref[j, pl.ds(r, TM_MERGE // d, stride=d), :] = o_ref[0, r, :, j * LANES:(j + 1) * LANES]
    half = slabs // 2
    nat = lambda s_ref, lo: jnp.concatenate([s_ref[j] for j in range(lo, lo + half)], axis=-1)
    outs = [o0_ref[0, 0, :, :PATTERN_WIDTH], nat(s1_ref, 0), nat(s2_ref, 0)]
    lses = [o0_ref[0, 0, :, PATTERN_WIDTH:], nat(s1_ref, half), nat(s2_ref, half)]
    top = jnp.maximum(jnp.maximum(lses[0], lses[1]), lses[2])
    ws = [jnp.exp(l - top) for l in lses]
    att = (ws[0] * outs[0] + ws[1] * outs[1] + ws[2] * outs[2]) / (ws[0] + ws[1] + ws[2])
    merged = merged + gate(1) * _dot(att.astype(BF16), wau_ref[...])

    xq = _dot(un, wpost_ref[:, :XATTN_WIDTH]).astype(BF16)
    heads = []
    for h in range(XATTN_HEADS):
        cols = slice(h * XATTN_HEAD_DIM, (h + 1) * XATTN_HEAD_DIM)
        s = _dot_nt(xq[:, cols], mk_ref[0, :, cols]) * (XATTN_HEAD_DIM ** -0.5)
        p = jnp.exp(s - jnp.max(s, axis=-1, keepdims=True))
        heads.append(_dot(p.astype(BF16), mv_ref[0, :, cols]) / jnp.sum(p, axis=-1, keepdims=True))
    xo = jnp.concatenate(heads, axis=-1).astype(BF16)
    merged = merged + gate(2) * _dot(xo, wxu_ref[...])

    out_ref[...] = x + _dot(merged.astype(BF16), wout_ref[...])


def _merge(x, y_rows, attn_outs, mk, mv, norm_g, w_post, w_glu, w_au, w_xu, w_out, seqlen):
    n = x.shape[0]
    tiles_per_seq = seqlen // TM_MERGE
    rows = lambda w: pl.BlockSpec((TM_MERGE, w), lambda i: (i, 0))
    mem = pl.BlockSpec((1, MEM_LEN, XATTN_WIDTH), lambda i: (i // tiles_per_seq, 0, 0))
    attn_specs = [pl.BlockSpec((1, d, TM_MERGE // d, 2 * PATTERN_WIDTH),
                               lambda i: (i // tiles_per_seq, 0, i % tiles_per_seq, 0))
                  for _, d in ATTN_PATTERNS]
    slabs = lambda k: pltpu.VMEM((k, TM_MERGE, LANES), F32)
    return pl.pallas_call(
        _merge_kernel,
        grid=(n // TM_MERGE,),
        in_specs=[rows(D_MODEL), pl.BlockSpec((TM_MERGE // SSM_CHUNK, SSM_ROW), lambda i: (i, 0))] + attn_specs
                 + [mem, mem, _resident((1, D_MODEL)), _resident((D_MODEL, POST_WIDTH)),
                    _resident((SSM_WIDTH, 2 * D_MODEL)), _resident((PATTERN_WIDTH, D_MODEL)),
                    _resident((XATTN_WIDTH, D_MODEL)), _resident((D_MODEL, D_MODEL))],
        out_specs=rows(D_MODEL),
        out_shape=jax.ShapeDtypeStruct((n, D_MODEL), F32),
        scratch_shapes=[slabs(SSM_N_OCT), slabs(2 * PATTERN_WIDTH // LANES), slabs(2 * PATTERN_WIDTH // LANES)],
        compiler_params=_params(1),
        name="merge",
    )(x, y_rows, *attn_outs, mk, mv, norm_g.reshape(1, D_MODEL), w_post.astype(BF16), w_glu.astype(BF16),
      w_au.astype(BF16), w_xu.astype(BF16), w_out.astype(BF16))


def _layer(x, mem, rel_table, ffn1_norm, ffn1_w_in, ffn1_w_down, mix_norm, w_in,
           ssm_a_re, ssm_a_im, ssm_log_dt, ssm_b_re, ssm_b_im, ssm_c_re, ssm_c_im, ssm_d, ssm_w_glu,
           attn_w_up, mem_norm, xattn_w_kv, xattn_w_up, w_out, ffn2_norm, ffn2_w_in, ffn2_w_down,
           final_norm, bsz, seqlen):
    x = _ffn(x, ffn1_norm, ffn1_w_in, ffn1_w_down)
    u_rows, q, k, v = _proj(x, mix_norm, w_in[:, :PRE_WIDTH], bsz, seqlen)
    mats = _ssm_matrices(ssm_a_re, ssm_a_im, ssm_log_dt, ssm_b_re, ssm_b_im, ssm_c_re, ssm_c_im, ssm_d)
    y_rows = _ssm(u_rows, mats, bsz)
    attn_outs = [_dilated_attention(q[g], k[g], v[g], rel_table, g) for g in range(N_PATTERNS)]
    mk, mv = _memkv(mem.reshape(bsz * MEM_LEN, D_MODEL), mem_norm, xattn_w_kv)
    mk = mk.reshape(bsz, MEM_LEN, XATTN_WIDTH)
    mv = mv.reshape(bsz, MEM_LEN, XATTN_WIDTH)
    x = _merge(x, y_rows, attn_outs, mk, mv, mix_norm, w_in[:, PRE_WIDTH:], ssm_w_glu, attn_w_up,
               xattn_w_up, w_out, seqlen)
    return _ffn(x, ffn2_norm, ffn2_w_in, ffn2_w_down, final_g=final_norm)


def kernel(x, mem, ffn1_norm, ffn1_w_in, ffn1_w_down, mix_norm, w_in, ssm_a_re, ssm_a_im, ssm_log_dt,
           ssm_b_re, ssm_b_im, ssm_c_re, ssm_c_im, ssm_d, ssm_w_glu, rel_table, attn_w_up, mem_norm,
           xattn_w_kv, xattn_w_up, w_out, ffn2_norm, ffn2_w_in, ffn2_w_down, final_norm):
    bsz, seqlen, _ = x.shape
    assert ffn1_norm.shape[0] == 1, "single-layer trunk"
    assert seqlen % (TQ_ATTN * ATTN_PATTERNS[-1][1]) == 0 and seqlen % TM_PROJ == 0 and seqlen % TM_MERGE == 0
    h = _layer(x.reshape(bsz * seqlen, D_MODEL), mem, rel_table, ffn1_norm[0], ffn1_w_in[0], ffn1_w_down[0],
               mix_norm[0], w_in[0], ssm_a_re[0], ssm_a_im[0], ssm_log_dt[0], ssm_b_re[0], ssm_b_im[0],
               ssm_c_re[0], ssm_c_im[0], ssm_d[0], ssm_w_glu[0], attn_w_up[0], mem_norm[0], xattn_w_kv[0],
               xattn_w_up[0], w_out[0], ffn2_norm[0], ffn2_w_in[0], ffn2_w_down[0], final_norm,
               bsz, seqlen)
    return h.reshape(bsz, seqlen, D_MODEL)
```

```python
import functools

import numpy as np
import jax
import jax.numpy as jnp
from jax import lax
from jax.experimental import pallas as pl
from jax.experimental.pallas import tpu as pltpu

F32 = jnp.float32
BF16 = jnp.bfloat16

LANES = 128
MXU_TILE = 256

D_MODEL = 1024
D_FF = 2816
EPS = 1e-6

HEAD_DIM = 64
ATTN_PATTERNS = ((128, 1), (512, 4), (2048, 16))
N_PATTERNS = 3
HEADS_PER_PATTERN = 4
N_ATTN_HEADS = N_PATTERNS * HEADS_PER_PATTERN
ATTN_WIDTH = N_ATTN_HEADS * HEAD_DIM
PATTERN_WIDTH = HEADS_PER_PATTERN * HEAD_DIM
WINDOW_KEYS = 128
REL_BUCKETS = 32
REL_MAX_DIST = 2048

SSM_GROUP = 16
SSM_STATE = 64
SSM_WIDTH = 512
SSM_GROUPS = SSM_WIDTH // SSM_GROUP
SSM_CHUNK = 16
SSM_OCT = LANES // SSM_GROUP
SSM_N_OCT = SSM_WIDTH // LANES
SSM_OCT_IN = SSM_CHUNK * LANES
SSM_OCT_STATE = SSM_OCT * SSM_STATE
SSM_ROW = SSM_CHUNK * SSM_WIDTH

MEM_LEN = 256
XATTN_HEADS = 4
XATTN_HEAD_DIM = 128
XATTN_WIDTH = XATTN_HEADS * XATTN_HEAD_DIM

PRE_WIDTH = SSM_WIDTH + 3 * ATTN_WIDTH
POST_WIDTH = XATTN_WIDTH + 3 * D_MODEL

NEG_BIG = -1e30

VMEM_LIMIT_BYTES = 56 * 1024 * 1024

TM_FFN = 512
FFN_CHUNK = 1408
TM_PROJ = 512
TM_MERGE = 256
TQ_ATTN = 512
SUB_Q = 128


def _rmsnorm(x, g):
    return x * lax.rsqrt(jnp.mean(x * x, axis=-1, keepdims=True) + EPS) * g


def _dot(a, b):
    return jnp.dot(a, b, preferred_element_type=F32)


def _dot_nt(a, b):
    return lax.dot_general(a, b, (((1,), (1,)), ((), ())), preferred_element_type=F32)


def _resident(shape):
    nd = len(shape)
    return pl.BlockSpec(shape, lambda *_: (0,) * nd, pipeline_mode=pl.Buffered(1))


def _params(n_axes):
    return pltpu.CompilerParams(dimension_semantics=("arbitrary",) * n_axes,
                                vmem_limit_bytes=VMEM_LIMIT_BYTES)


def _ffn_kernel(x_ref, g_ref, win_ref, wdn_ref, *rest, final_norm):
    if final_norm:
        fg_ref, o_ref = rest
    else:
        (o_ref,) = rest
    x = x_ref[...]
    un = _rmsnorm(x, g_ref[...]).astype(BF16)
    acc = None
    for c in range(D_FF // FFN_CHUNK):
        lo = c * FFN_CHUNK
        a = _dot(un, win_ref[:, lo:lo + FFN_CHUNK])
        b = _dot(un, win_ref[:, D_FF + lo:D_FF + lo + FFN_CHUNK])
        h = (a * jax.nn.sigmoid(a) * b).astype(BF16)
        part = _dot(h, wdn_ref[lo:lo + FFN_CHUNK, :])
        acc = part if acc is None else acc + part
    y = x + 0.5 * acc
    if final_norm:
        y = _rmsnorm(y, fg_ref[...])
    o_ref[...] = y


def _ffn(x, norm_g, w_in, w_down, final_g=None):
    n = x.shape[0]
    row = pl.BlockSpec((TM_FFN, D_MODEL), lambda i: (i, 0))
    in_specs = [row, _resident((1, D_MODEL)), _resident((D_MODEL, 2 * D_FF)), _resident((D_FF, D_MODEL))]
    args = [x, norm_g.reshape(1, D_MODEL), w_in.astype(BF16), w_down.astype(BF16)]
    if final_g is not None:
        in_specs.append(_resident((1, D_MODEL)))
        args.append(final_g.reshape(1, D_MODEL))
    return pl.pallas_call(
        functools.partial(_ffn_kernel, final_norm=final_g is not None),
        grid=(n // TM_FFN,),
        in_specs=in_specs,
        out_specs=row,
        out_shape=jax.ShapeDtypeStruct((n, D_MODEL), F32),
        compiler_params=_params(1),
        name="ffn_final" if final_g is not None else "ffn",
    )(*args)


def _proj_kernel(x_ref, g_ref, w_ref, u_ref, *rest):
    qkv_refs, (su_ref, sq_ref, sk_ref, sv_ref) = rest[:3 * N_PATTERNS], rest[3 * N_PATTERNS:]
    un = _rmsnorm(x_ref[...], g_ref[...]).astype(BF16)

    u = _dot(un, w_ref[:, :SSM_WIDTH])
    for o in range(SSM_N_OCT):
        su_ref[o] = u[:, o * LANES:(o + 1) * LANES]
    chunk_rows = TM_PROJ // SSM_CHUNK
    for o in range(SSM_N_OCT):
        for t in range(SSM_CHUNK):
            lo = o * SSM_OCT_IN + t * LANES
            u_ref[:, lo:lo + LANES] = su_ref[o, pl.ds(t, chunk_rows, stride=SSM_CHUNK), :].astype(BF16)

    for which, scr in enumerate((sq_ref, sk_ref, sv_ref)):
        lo = SSM_WIDTH + which * ATTN_WIDTH
        res = _dot(un, w_ref[:, lo:lo + ATTN_WIDTH])
        if which == 0:
            res = res * (HEAD_DIM ** -0.5)
        for j in range(ATTN_WIDTH // LANES):
            scr[j] = res[:, j * LANES:(j + 1) * LANES]
        for g, (_, d) in enumerate(ATTN_PATTERNS):
            out = qkv_refs[which * N_PATTERNS + g]
            for jj in range(PATTERN_WIDTH // LANES):
                j = g * (PATTERN_WIDTH // LANES) + jj
                for r in range(d):
                    out[0, r, :, jj * LANES:(jj + 1) * LANES] = (
                        scr[j, pl.ds(r, TM_PROJ // d, stride=d), :].astype(BF16))


def _proj(x, norm_g, w_pre, bsz, seqlen):
    n = x.shape[0]
    tiles_per_seq = seqlen // TM_PROJ
    rows = lambda w: pl.BlockSpec((TM_PROJ, w), lambda i: (i, 0))
    qkv_specs, qkv_shapes = [], []
    for _ in range(3):
        for _, d in ATTN_PATTERNS:
            qkv_specs.append(pl.BlockSpec((1, d, TM_PROJ // d, PATTERN_WIDTH),
                                          lambda i: (i // tiles_per_seq, 0, i % tiles_per_seq, 0)))
            qkv_shapes.append(jax.ShapeDtypeStruct((bsz, d, seqlen // d, PATTERN_WIDTH), BF16))
    slabs = lambda k: pltpu.VMEM((k, TM_PROJ, LANES), F32)
    outs = pl.pallas_call(
        _proj_kernel,
        grid=(n // TM_PROJ,),
        in_specs=[rows(D_MODEL), _resident((1, D_MODEL)), _resident((D_MODEL, PRE_WIDTH))],
        out_specs=[pl.BlockSpec((TM_PROJ // SSM_CHUNK, SSM_ROW), lambda i: (i, 0))] + qkv_specs,
        out_shape=[jax.ShapeDtypeStruct((n // SSM_CHUNK, SSM_ROW), BF16)] + qkv_shapes,
        scratch_shapes=[slabs(SSM_N_OCT)] + [slabs(ATTN_WIDTH // LANES)] * 3,
        compiler_params=_params(1),
        name="proj",
    )(x, norm_g.reshape(1, D_MODEL), w_pre.astype(BF16))
    u = outs[0]
    q, k, v = (outs[1 + i * N_PATTERNS:1 + (i + 1) * N_PATTERNS] for i in range(3))
    return u, q, k, v


def _ssm_matrices(a_re, a_im, log_dt, b_re, b_im, c_re, c_im, d_skip):
    t_len = SSM_CHUNK
    dt = jnp.exp(log_dt)[:, None]
    mag = jnp.exp(a_re * dt)
    ang = a_im * dt
    abar_re = mag * jnp.cos(ang)
    abar_im = mag * jnp.sin(ang)
    nr = abar_re - 1.0
    ni = abar_im
    den = a_re * a_re + a_im * a_im
    coef_re = (nr * a_re + ni * a_im) / den
    coef_im = (ni * a_re - nr * a_im) / den
    bbar_re = coef_re[..., None] * b_re - coef_im[..., None] * b_im
    bbar_im = coef_re[..., None] * b_im + coef_im[..., None] * b_re
    pw_re = [jnp.ones_like(abar_re)]
    pw_im = [jnp.zeros_like(abar_re)]
    for _ in range(t_len):
        pr, pi = pw_re[-1], pw_im[-1]
        pw_re.append(pr * abar_re - pi * abar_im)
        pw_im.append(pr * abar_im + pi * abar_re)
    rev_re = jnp.stack(pw_re[t_len - 1::-1])
    rev_im = jnp.stack(pw_im[t_len - 1::-1])
    pw_re = jnp.stack(pw_re)
    pw_im = jnp.stack(pw_im)
    hp = lax.Precision.HIGHEST
    n_oct, oct_, grp, st = SSM_N_OCT, SSM_OCT, SSM_GROUP, SSM_STATE
    cp_re = c_re[None] * pw_re[:, :, None, :] - c_im[None] * pw_im[:, :, None, :]
    cp_im = c_re[None] * pw_im[:, :, None, :] + c_im[None] * pw_re[:, :, None, :]
    kern = (jnp.einsum("jghp,gpk->jghk", cp_re[:t_len], bbar_re, precision=hp)
            - jnp.einsum("jghp,gpk->jghk", cp_im[:t_len], bbar_im, precision=hp))
    kern = jnp.concatenate([kern[:1] + d_skip[None, :, :, None] * jnp.eye(grp, dtype=F32), kern[1:]], axis=0)

    def expand(compact, width):
        rows = compact.shape[-2]
        row_group = (np.arange(rows) // grp) % oct_
        col_group = np.arange(oct_ * width) // width
        tiled = jnp.tile(compact, (1,) * (compact.ndim - 1) + (oct_,))
        return jnp.where(row_group[:, None] == col_group[None, :], tiled, 0.0)

    lagblk = expand(jnp.swapaxes(kern, -1, -2).reshape(t_len, n_oct, LANES, grp), grp)
    lagblk = lagblk.transpose(1, 0, 2, 3)

    in_re = rev_re[..., None] * bbar_re[None] - rev_im[..., None] * bbar_im[None]
    in_im = rev_re[..., None] * bbar_im[None] + rev_im[..., None] * bbar_re[None]

    def into_layout(m):
        m = m.reshape(t_len, n_oct, oct_, st, grp).transpose(1, 0, 2, 4, 3)
        return expand(m.reshape(n_oct, SSM_OCT_IN, st), st)

    into = jnp.concatenate([into_layout(in_re), into_layout(in_im)], axis=-1)

    def outof_layout(m):
        m = m.reshape(t_len, n_oct, oct_, grp, st).transpose(1, 0, 2, 3, 4)
        return expand(m.reshape(n_oct, SSM_OCT_IN, st), st)

    outof_t = jnp.concatenate([outof_layout(cp_re[1:]), -outof_layout(cp_im[1:])], axis=-1)
    decay = jnp.stack([pw_re[t_len].reshape(n_oct, SSM_OCT_STATE), pw_im[t_len].reshape(n_oct, SSM_OCT_STATE)]
                      + [jnp.zeros((n_oct, SSM_OCT_STATE), F32)] * 6, axis=1)
    return lagblk.astype(BF16), into.astype(BF16), outof_t.astype(BF16), decay


def _ssm_kernel(x_ref, lag_ref, into_ref, outof_ref, decay_ref, y_ref, toep_ref, loc_ref, prev_ref, *, n_chunks):
    @pl.when(pl.program_id(1) == 0)
    def _build_toeplitz():
        zeros = jnp.zeros((LANES, LANES), BF16)
        for s in range(SSM_CHUNK):
            for t in range(SSM_CHUNK):
                toep_ref[s * LANES:(s + 1) * LANES, t * LANES:(t + 1) * LANES] = (
                    lag_ref[0, t - s] if t >= s else zeros)

    loc_ref[...] = _dot(x_ref[...], into_ref[0])
    ar = decay_ref[0, 0:1, :]
    ai = decay_ref[0, 1:2, :]

    def step(c, carry):
        s_re, s_im = carry
        row = pl.ds(c, 1)
        prev_ref[row, :SSM_OCT_STATE] = s_re
        prev_ref[row, SSM_OCT_STATE:] = s_im
        n_re = ar * s_re - ai * s_im + loc_ref[row, :SSM_OCT_STATE]
        n_im = ar * s_im + ai * s_re + loc_ref[row, SSM_OCT_STATE:]
        return n_re, n_im

    zero = jnp.zeros((1, SSM_OCT_STATE), F32)
    lax.fori_loop(0, n_chunks, step, (zero, zero), unroll=8)
    prev = prev_ref[...].astype(BF16)
    for j in range(SSM_OCT_IN // MXU_TILE):
        lo, hi = j * MXU_TILE, (j + 1) * MXU_TILE
        y = _dot(x_ref[:, :hi], toep_ref[:hi, lo:hi]) + _dot_nt(prev, outof_ref[0, lo:hi, :])
        y_ref[:, lo:hi] = jax.nn.gelu(y).astype(BF16)


def _ssm(u_rows, mats, bsz):
    lagblk, into, outof_t, decay = mats
    n_chunks = u_rows.shape[0] // bsz
    per_oct = lambda *shape: pl.BlockSpec((1,) + shape, lambda o, b: (o,) + (0,) * len(shape),
                                          pipeline_mode=pl.Buffered(1))
    rows = pl.BlockSpec((n_chunks, SSM_OCT_IN), lambda o, b: (b, o))
    return pl.pallas_call(
        functools.partial(_ssm_kernel, n_chunks=n_chunks),
        grid=(SSM_N_OCT, bsz),
        in_specs=[rows, per_oct(SSM_CHUNK, LANES, LANES), per_oct(SSM_OCT_IN, 2 * SSM_OCT_STATE),
                  per_oct(SSM_OCT_IN, 2 * SSM_OCT_STATE), per_oct(8, SSM_OCT_STATE)],
        out_specs=rows,
        out_shape=jax.ShapeDtypeStruct(u_rows.shape, BF16),
        scratch_shapes=[pltpu.VMEM((SSM_OCT_IN, SSM_OCT_IN), BF16),
                        pltpu.VMEM((n_chunks, 2 * SSM_OCT_STATE), F32),
                        pltpu.VMEM((n_chunks, 2 * SSM_OCT_STATE), F32)],
        compiler_params=_params(2),
        name="ssm",
    )(u_rows, lagblk, into, outof_t, decay)


def _t5_buckets(dist):
    dist = np.asarray(dist, np.int32)
    max_exact = REL_BUCKETS // 2
    safe = np.maximum(dist, 1).astype(np.float32)
    large = max_exact + (np.log(safe / max_exact) / np.log(REL_MAX_DIST / max_exact)
                         * (REL_BUCKETS - max_exact)).astype(np.int32)
    large = np.minimum(large, REL_BUCKETS - 1)
    return np.where(dist < max_exact, dist, large).astype(np.int32)


def _band_bias(rel_table, pattern):
    dilation = ATTN_PATTERNS[pattern][1]
    buckets = _t5_buckets(np.arange(WINDOW_KEYS + 1) * dilation)
    table = rel_table[:, pattern * HEADS_PER_PATTERN:(pattern + 1) * HEADS_PER_PATTERN].T
    runs = []
    for b in np.unique(buckets):
        runs.append(jnp.broadcast_to(table[:, b:b + 1], (HEADS_PER_PATTERN, int(np.sum(buckets == b)))))
    per_step = jnp.concatenate(runs, axis=1)
    period = 3 * SUB_Q
    masked = lambda n: jnp.full((HEADS_PER_PATTERN, n), NEG_BIG, F32)
    w = jnp.concatenate([masked(SUB_Q - 1), per_step[:, ::-1], masked(period - SUB_Q - WINDOW_KEYS)], axis=1)
    skew = jnp.tile(w, (1, SUB_Q))[:, :SUB_Q * (period - 1)].reshape(HEADS_PER_PATTERN, SUB_Q, period - 1)
    bias = skew[:, :, SUB_Q - 1:3 * SUB_Q - 1]
    first = jnp.where((np.arange(2 * SUB_Q) >= SUB_Q)[None, None, :], bias, NEG_BIG)
    return jnp.stack([bias, first])


def _attn_kernel(q_ref, kc_ref, kh_ref, vc_ref, vh_ref, bias_ref, o_ref):
    first = jnp.where(pl.program_id(2) == 0, 1, 0)
    lane = lax.broadcasted_iota(jnp.int32, (SUB_Q, 2 * HEAD_DIM), 1)
    low_half = lane < HEAD_DIM
    for j in range(TQ_ATTN // SUB_Q):
        rows = slice(j * SUB_Q, (j + 1) * SUB_Q)
        q = q_ref[0, 0, rows, :]
        if j == 0:
            k = jnp.concatenate([kh_ref[0, 0], kc_ref[0, 0, rows, :]], axis=0)
            v = jnp.concatenate([vh_ref[0, 0], vc_ref[0, 0, rows, :]], axis=0)
        else:
            k = kc_ref[0, 0, (j - 1) * SUB_Q:(j + 1) * SUB_Q, :]
            v = vc_ref[0, 0, (j - 1) * SUB_Q:(j + 1) * SUB_Q, :]
        for pair in range(HEADS_PER_PATTERN // 2):
            cols = slice(pair * 2 * HEAD_DIM, (pair + 1) * 2 * HEAD_DIM)
            q2, k2, v2 = q[:, cols], k[:, cols], v[:, cols]
            outs, lses = [], []
            for sub in range(2):
                head = 2 * pair + sub
                keep = low_half if sub == 0 else jnp.logical_not(low_half)
                qh = jnp.where(keep, q2, jnp.zeros_like(q2))
                s = _dot_nt(qh, k2)
                s = s + (bias_ref[first, head] if j == 0 else bias_ref[0, head])
                m = jnp.max(s, axis=-1, keepdims=True)
                p = jnp.exp(s - m)
                l = jnp.sum(p, axis=-1, keepdims=True)
                outs.append(_dot(p.astype(BF16), v2) / l)
                lses.append(jnp.broadcast_to(m + jnp.log(l), (SUB_Q, 2 * HEAD_DIM)))
            o_ref[0, 0, rows, cols] = jnp.where(low_half, outs[0], outs[1])
            o_ref[0, 0, rows, PATTERN_WIDTH + pair * 2 * HEAD_DIM:PATTERN_WIDTH + (pair + 1) * 2 * HEAD_DIM] = (
                jnp.where(low_half, lses[0], lses[1]))


def _dilated_attention(q, k, v, rel_table, pattern):
    bsz, d, res_len, _ = q.shape
    cur = pl.BlockSpec((1, 1, TQ_ATTN, PATTERN_WIDTH), lambda b, r, i: (b, r, i, 0))
    halo = pl.BlockSpec((1, 1, SUB_Q, PATTERN_WIDTH),
                        lambda b, r, i: (b, r, jnp.maximum(i * (TQ_ATTN // SUB_Q) - 1, 0), 0))
    return pl.pallas_call(
        _attn_kernel,
        grid=(bsz, d, res_len // TQ_ATTN),
        in_specs=[cur, cur, halo, cur, halo, _resident((2, HEADS_PER_PATTERN, SUB_Q, 2 * SUB_Q))],
        out_specs=pl.BlockSpec((1, 1, TQ_ATTN, 2 * PATTERN_WIDTH), lambda b, r, i: (b, r, i, 0)),
        out_shape=jax.ShapeDtypeStruct((bsz, d, res_len, 2 * PATTERN_WIDTH), F32),
        compiler_params=_params(3),
        name=f"attn_d{d}",
    )(q, k, k, v, v, _band_bias(rel_table, pattern))


def _memkv_kernel(mem_ref, g_ref, w_ref, k_ref, v_ref):
    mn = _rmsnorm(mem_ref[...], g_ref[...]).astype(BF16)
    k_ref[...] = _dot(mn, w_ref[:, :XATTN_WIDTH]).astype(BF16)
    v_ref[...] = _dot(mn, w_ref[:, XATTN_WIDTH:]).astype(BF16)


def _memkv(mem2d, norm_g, w_kv):
    rows = mem2d.shape[0]
    full = lambda *shape: pl.BlockSpec(shape, lambda i: (0,) * len(shape))
    return pl.pallas_call(
        _memkv_kernel,
        grid=(1,),
        in_specs=[full(rows, D_MODEL), full(1, D_MODEL), full(D_MODEL, 2 * XATTN_WIDTH)],
        out_specs=[full(rows, XATTN_WIDTH), full(rows, XATTN_WIDTH)],
        out_shape=[jax.ShapeDtypeStruct((rows, XATTN_WIDTH), BF16)] * 2,
        compiler_params=_params(1),
        name="memkv",
    )(mem2d, norm_g.reshape(1, D_MODEL), w_kv.astype(BF16))


def _merge_kernel(x_ref, y_ref, o0_ref, o1_ref, o2_ref, mk_ref, mv_ref, g_ref, wpost_ref, wglu_ref,
                  wau_ref, wxu_ref, wout_ref, out_ref, ys_ref, s1_ref, s2_ref):
    x = x_ref[...]
    un = _rmsnorm(x, g_ref[...]).astype(BF16)

    def gate(index):
        lo = XATTN_WIDTH + index * D_MODEL
        return jax.nn.sigmoid(_dot(un, wpost_ref[:, lo:lo + D_MODEL]))

    chunk_rows = TM_MERGE // SSM_CHUNK
    for o in range(SSM_N_OCT):
        for t in range(SSM_CHUNK):
            lo = o * SSM_OCT_IN + t * LANES
            ys_ref[o, pl.ds(t, chunk_rows, stride=SSM_CHUNK), :] = y_ref[:, lo:lo + LANES].astype(F32)
    ys = jnp.concatenate([ys_ref[o] for o in range(SSM_N_OCT)], axis=-1).astype(BF16)
    glu = _dot(ys, wglu_ref[:, :D_MODEL]) * jax.nn.sigmoid(_dot(ys, wglu_ref[:, D_MODEL:]))
    merged = gate(0) * glu

    slabs = 2 * PATTERN_WIDTH // LANES
    for o_ref, s_ref in ((o1_ref, s1_ref), (o2_ref, s2_ref)):
        d = o_ref.shape[1]
        for r in range(d):
            for j in range(slabs):
                s_ref[j, pl.ds(r, TM_MERGE // d, stride=d), :] = o_ref[0, r, :, j * LANES:(j + 1) * LANES]
    half = slabs // 2
    nat = lambda s_ref, lo: jnp.concatenate([s_ref[j] for j in range(lo, lo + half)], axis=-1)
    outs = [o0_ref[0, 0, :, :PATTERN_WIDTH], nat(s1_ref, 0), nat(s2_ref, 0)]
    lses = [o0_ref[0, 0, :, PATTERN_WIDTH:], nat(s1_ref, half), nat(s2_ref, half)]
    top = jnp.maximum(jnp.maximum(lses[0], lses[1]), lses[2])
    ws = [jnp.exp(l - top) for l in lses]
    att = (ws[0] * outs[0] + ws[1] * outs[1] + ws[2] * outs[2]) / (ws[0] + ws[1] + ws[2])
    merged = merged + gate(1) * _dot(att.astype(BF16), wau_ref[...])

    xq = _dot(un, wpost_ref[:, :XATTN_WIDTH]).astype(BF16)
    heads = []
    for h in range(XATTN_HEADS):
        cols = slice(h * XATTN_HEAD_DIM, (h + 1) * XATTN_HEAD_DIM)
        s = _dot_nt(xq[:, cols], mk_ref[0, :, cols]) * (XATTN_HEAD_DIM ** -0.5)
        p = jnp.exp(s - jnp.max(s, axis=-1, keepdims=True))
        heads.append(_dot(p.astype(BF16), mv_ref[0, :, cols]) / jnp.sum(p, axis=-1, keepdims=True))
    xo = jnp.concatenate(heads, axis=-1).astype(BF16)
    merged = merged + gate(2) * _dot(xo, wxu_ref[...])

    out_ref[...] = x + _dot(merged.astype(BF16), wout_ref[...])


def _merge(x, y_rows, attn_outs, mk, mv, norm_g, w_post, w_glu, w_au, w_xu, w_out, seqlen):
    n = x.shape[0]
    tiles_per_seq = seqlen // TM_MERGE
    rows = lambda w: pl.BlockSpec((TM_MERGE, w), lambda i: (i, 0))
    mem = pl.BlockSpec((1, MEM_LEN, XATTN_WIDTH), lambda i: (i // tiles_per_seq, 0, 0))
    attn_specs = [pl.BlockSpec((1, d, TM_MERGE // d, 2 * PATTERN_WIDTH),
                               lambda i: (i // tiles_per_seq, 0, i % tiles_per_seq, 0))
                  for _, d in ATTN_PATTERNS]
    slabs = lambda k: pltpu.VMEM((k, TM_MERGE, LANES), F32)
    return pl.pallas_call(
        _merge_kernel,
        grid=(n // TM_MERGE,),
        in_specs=[rows(D_MODEL), pl.BlockSpec((TM_MERGE // SSM_CHUNK, SSM_ROW), lambda i: (i, 0))] + attn_specs
                 + [mem, mem, _resident((1, D_MODEL)), _resident((D_MODEL, POST_WIDTH)),
                    _resident((SSM_WIDTH, 2 * D_MODEL)), _resident((PATTERN_WIDTH, D_MODEL)),
                    _resident((XATTN_WIDTH, D_MODEL)), _resident((D_MODEL, D_MODEL))],
        out_specs=rows(D_MODEL),
        out_shape=jax.ShapeDtypeStruct((n, D_MODEL), F32),
        scratch_shapes=[slabs(SSM_N_OCT), slabs(2 * PATTERN_WIDTH // LANES), slabs(2 * PATTERN_WIDTH // LANES)],
        compiler_params=_params(1),
        name="merge",
    )(x, y_rows, *attn_outs, mk, mv, norm_g.reshape(1, D_MODEL), w_post.astype(BF16), w_glu.astype(BF16),
      w_au.astype(BF16), w_xu.astype(BF16), w_out.astype(BF16))


def _layer(x, mem, rel_table, ffn1_norm, ffn1_w_in, ffn1_w_down, mix_norm, w_in,
           ssm_a_re, ssm_a_im, ssm_log_dt, ssm_b_re, ssm_b_im, ssm_c_re, ssm_c_im, ssm_d, ssm_w_glu,
           attn_w_up, mem_norm, xattn_w_kv, xattn_w_up, w_out, ffn2_norm, ffn2_w_in, ffn2_w_down,
           final_norm, bsz, seqlen):
    x = _ffn(x, ffn1_norm, ffn1_w_in, ffn1_w_down)
    u_rows, q, k, v = _proj(x, mix_norm, w_in[:, :PRE_WIDTH], bsz, seqlen)
    mats = _ssm_matrices(ssm_a_re, ssm_a_im, ssm_log_dt, ssm_b_re, ssm_b_im, ssm_c_re, ssm_c_im, ssm_d)
    y_rows = _ssm(u_rows, mats, bsz)
    attn_outs = [_dilated_attention(q[g], k[g], v[g], rel_table, g) for g in range(N_PATTERNS)]
    mk, mv = _memkv(mem.reshape(bsz * MEM_LEN, D_MODEL), mem_norm, xattn_w_kv)
    mk = mk.reshape(bsz, MEM_LEN, XATTN_WIDTH)
    mv = mv.reshape(bsz, MEM_LEN, XATTN_WIDTH)
    x = _merge(x, y_rows, attn_outs, mk, mv, mix_norm, w_in[:, PRE_WIDTH:], ssm_w_glu, attn_w_up,
               xattn_w_up, w_out, seqlen)
    return _ffn(x, ffn2_norm, ffn2_w_in, ffn2_w_down, final_g=final_norm)


def kernel(x, mem, ffn1_norm, ffn1_w_in, ffn1_w_down, mix_norm, w_in, ssm_a_re, ssm_a_im, ssm_log_dt,
           ssm_b_re, ssm_b_im, ssm_c_re, ssm_c_im, ssm_d, ssm_w_glu, rel_table, attn_w_up, mem_norm,
           xattn_w_kv, xattn_w_up, w_out, ffn2_norm, ffn2_w_in, ffn2_w_down, final_norm):
    bsz, seqlen, _ = x.shape
    assert ffn1_norm.shape[0] == 1, "single-layer trunk"
    assert seqlen % (TQ_ATTN * ATTN_PATTERNS[-1][1]) == 0 and seqlen % TM_PROJ == 0 and seqlen % TM_MERGE == 0
    h = _layer(x.reshape(bsz * seqlen, D_MODEL), mem, rel_table, ffn1_norm[0], ffn1_w_in[0], ffn1_w_down[0],
               mix_norm[0], w_in[0], ssm_a_re[0], ssm_a_im[0], ssm_log_dt[0], ssm_b_re[0], ssm_b_im[0],
               ssm_c_re[0], ssm_c_im[0], ssm_d[0], ssm_w_glu[0], attn_w_up[0], mem_norm[0], xattn_w_kv[0],
               xattn_w_up[0], w_out[0], ffn2_norm[0], ffn2_w_in[0], ffn2_w_down[0], final_norm,
               bsz, seqlen)
    return h.reshape(bsz, seqlen, D_MODEL)
```

```python
import functools

import numpy as np
import jax
import jax.numpy as jnp
from jax import lax
from jax.experimental import pallas as pl
from jax.experimental.pallas import tpu as pltpu

F32 = jnp.float32
BF16 = jnp.bfloat16

LANES = 128
MXU_TILE = 256

D_MODEL = 1024
D_FF = 2816
EPS = 1e-6

HEAD_DIM = 64
ATTN_PATTERNS = ((128, 1), (512, 4), (2048, 16))
N_PATTERNS = 3
HEADS_PER_PATTERN = 4
N_ATTN_HEADS = N_PATTERNS * HEADS_PER_PATTERN
ATTN_WIDTH = N_ATTN_HEADS * HEAD_DIM
PATTERN_WIDTH = HEADS_PER_PATTERN * HEAD_DIM
WINDOW_KEYS = 128
REL_BUCKETS = 32
REL_MAX_DIST = 2048

SSM_GROUP = 16
SSM_STATE = 64
SSM_WIDTH = 512
SSM_GROUPS = SSM_WIDTH // SSM_GROUP
SSM_CHUNK = 16
SSM_OCT = LANES // SSM_GROUP
SSM_N_OCT = SSM_WIDTH // LANES
SSM_OCT_IN = SSM_CHUNK * LANES
SSM_OCT_STATE = SSM_OCT * SSM_STATE
SSM_ROW = SSM_CHUNK * SSM_WIDTH

MEM_LEN = 256
XATTN_HEADS = 4
XATTN_HEAD_DIM = 128
XATTN_WIDTH = XATTN_HEADS * XATTN_HEAD_DIM

PRE_WIDTH = SSM_WIDTH + 3 * ATTN_WIDTH
POST_WIDTH = XATTN_WIDTH + 3 * D_MODEL

NEG_BIG = -1e30

VMEM_LIMIT_BYTES = 56 * 1024 * 1024

TM_FFN = 512
FFN_CHUNKS = ((0, 1536), (1536, D_FF))
TM_PROJ = 512
TM_MERGE = 256
TQ_ATTN = 512
SSM_SEQ_PER_STEP = 2
SUB_Q = 128


def _rmsnorm(x, g):
    return x * lax.rsqrt(jnp.mean(x * x, axis=-1, keepdims=True) + EPS) * g


def _dot(a, b):
    return jnp.dot(a, b, preferred_element_type=F32)


def _dot_nt(a, b):
    return lax.dot_general(a, b, (((1,), (1,)), ((), ())), preferred_element_type=F32)


def _resident(shape):
    nd = len(shape)
    return pl.BlockSpec(shape, lambda *_: (0,) * nd, pipeline_mode=pl.Buffered(1))


def _params(n_axes):
    return pltpu.CompilerParams(dimension_semantics=("arbitrary",) * n_axes,
                                vmem_limit_bytes=VMEM_LIMIT_BYTES)


def _ffn_kernel(x_ref, g_ref, win_ref, wdn_ref, *rest, final_norm):
    if final_norm:
        fg_ref, o_ref = rest
    else:
        (o_ref,) = rest
    x = x_ref[...]
    un = _rmsnorm(x, g_ref[...]).astype(BF16)
    acc = None
    for lo, hi in FFN_CHUNKS:
        a = _dot(un, win_ref[:, lo:hi])
        b = _dot(un, win_ref[:, D_FF + lo:D_FF + hi])
        h = (a * jax.nn.sigmoid(a) * b).astype(BF16)
        part = _dot(h, wdn_ref[lo:hi, :])
        acc = part if acc is None else acc + part
    y = x + 0.5 * acc
    if final_norm:
        y = _rmsnorm(y, fg_ref[...])
    o_ref[...] = y


def _ffn(x, norm_g, w_in, w_down, final_g=None):
    n = x.shape[0]
    row = pl.BlockSpec((TM_FFN, D_MODEL), lambda i: (i, 0))
    in_specs = [row, _resident((1, D_MODEL)), _resident((D_MODEL, 2 * D_FF)), _resident((D_FF, D_MODEL))]
    args = [x, norm_g.reshape(1, D_MODEL), w_in.astype(BF16), w_down.astype(BF16)]
    if final_g is not None:
        in_specs.append(_resident((1, D_MODEL)))
        args.append(final_g.reshape(1, D_MODEL))
    return pl.pallas_call(
        functools.partial(_ffn_kernel, final_norm=final_g is not None),
        grid=(n // TM_FFN,),
        in_specs=in_specs,
        out_specs=row,
        out_shape=jax.ShapeDtypeStruct((n, D_MODEL), F32),
        compiler_params=_params(1),
        name="ffn_final" if final_g is not None else "ffn",
    )(*args)


def _proj_kernel(x_ref, g_ref, w_ref, u_ref, *rest):
    qkv_refs, (su_ref, sq_ref, sk_ref, sv_ref) = rest[:3 * N_PATTERNS], rest[3 * N_PATTERNS:]
    un = _rmsnorm(x_ref[...], g_ref[...]).astype(BF16)

    u = _dot(un, w_ref[:, :SSM_WIDTH])
    for o in range(SSM_N_OCT):
        su_ref[o] = u[:, o * LANES:(o + 1) * LANES]
    chunk_rows = TM_PROJ // SSM_CHUNK
    for o in range(SSM_N_OCT):
        for t in range(SSM_CHUNK):
            lo = o * SSM_OCT_IN + t * LANES
            u_ref[:, lo:lo + LANES] = su_ref[o, pl.ds(t, chunk_rows, stride=SSM_CHUNK), :].astype(BF16)

    for which, scr in enumerate((sq_ref, sk_ref, sv_ref)):
        lo = SSM_WIDTH + which * ATTN_WIDTH
        res = _dot(un, w_ref[:, lo:lo + ATTN_WIDTH])
        if which == 0:
            res = res * (HEAD_DIM ** -0.5)
        for j in range(ATTN_WIDTH // LANES):
            scr[j] = res[:, j * LANES:(j + 1) * LANES]
        for g, (_, d) in enumerate(ATTN_PATTERNS):
            out = qkv_refs[which * N_PATTERNS + g]
            for jj in range(PATTERN_WIDTH // LANES):
                j = g * (PATTERN_WIDTH // LANES) + jj
                for r in range(d):
                    out[0, r, :, jj * LANES:(jj + 1) * LANES] = (
                        scr[j, pl.ds(r, TM_PROJ // d, stride=d), :].astype(BF16))


def _proj(x, norm_g, w_pre, bsz, seqlen):
    n = x.shape[0]
    tiles_per_seq = seqlen // TM_PROJ
    rows = lambda w: pl.BlockSpec((TM_PROJ, w), lambda i: (i, 0))
    qkv_specs, qkv_shapes = [], []
    for _ in range(3):
        for _, d in ATTN_PATTERNS:
            qkv_specs.append(pl.BlockSpec((1, d, TM_PROJ // d, PATTERN_WIDTH),
                                          lambda i: (i // tiles_per_seq, 0, i % tiles_per_seq, 0)))
            qkv_shapes.append(jax.ShapeDtypeStruct((bsz, d, seqlen // d, PATTERN_WIDTH), BF16))
    slabs = lambda k: pltpu.VMEM((k, TM_PROJ, LANES), F32)
    outs = pl.pallas_call(
        _proj_kernel,
        grid=(n // TM_PROJ,),
        in_specs=[rows(D_MODEL), _resident((1, D_MODEL)), _resident((D_MODEL, PRE_WIDTH))],
        out_specs=[pl.BlockSpec((TM_PROJ // SSM_CHUNK, SSM_ROW), lambda i: (i, 0))] + qkv_specs,
        out_shape=[jax.ShapeDtypeStruct((n // SSM_CHUNK, SSM_ROW), BF16)] + qkv_shapes,
        scratch_shapes=[slabs(SSM_N_OCT)] + [slabs(ATTN_WIDTH // LANES)] * 3,
        compiler_params=_params(1),
        name="proj",
    )(x, norm_g.reshape(1, D_MODEL), w_pre.astype(BF16))
    u = outs[0]
    q, k, v = (outs[1 + i * N_PATTERNS:1 + (i + 1) * N_PATTERNS] for i in range(3))
    return u, q, k, v


def _ssm_compact(a_re, a_im, log_dt, b_re, b_im, c_re, c_im, d_skip):
    t_len = SSM_CHUNK
    dt = jnp.exp(log_dt)[:, None]
    mag = jnp.exp(a_re * dt)
    ang = a_im * dt
    abar_re = mag * jnp.cos(ang)
    abar_im = mag * jnp.sin(ang)
    nr = abar_re - 1.0
    ni = abar_im
    den = a_re * a_re + a_im * a_im
    coef_re = (nr * a_re + ni * a_im) / den
    coef_im = (ni * a_re - nr * a_im) / den
    bbar_re = coef_re[..., None] * b_re - coef_im[..., None] * b_im
    bbar_im = coef_re[..., None] * b_im + coef_im[..., None] * b_re
    bt_re = jnp.swapaxes(bbar_re, -1, -2)
    bt_im = jnp.swapaxes(bbar_im, -1, -2)
    pw_re = [jnp.ones_like(abar_re)]
    pw_im = [jnp.zeros_like(abar_re)]
    for _ in range(t_len):
        pr, pi = pw_re[-1], pw_im[-1]
        pw_re.append(pr * abar_re - pi * abar_im)
        pw_im.append(pr * abar_im + pi * abar_re)
    rev_re = jnp.stack(pw_re[t_len - 1::-1])[:, :, None, :]
    rev_im = jnp.stack(pw_im[t_len - 1::-1])[:, :, None, :]
    pw_re = jnp.stack(pw_re)
    pw_im = jnp.stack(pw_im)
    n_oct, oct_, grp, st = SSM_N_OCT, SSM_OCT, SSM_GROUP, SSM_STATE

    def per_octet(re, im):
        both = jnp.concatenate([re, im], axis=-1)
        n = both.shape[0]
        return both.reshape(n, n_oct, oct_ * grp, 2 * st).transpose(1, 0, 2, 3).reshape(n_oct, n * LANES, 2 * st)

    into_c = per_octet(rev_re * bt_re[None] - rev_im * bt_im[None], rev_re * bt_im[None] + rev_im * bt_re[None])
    cp_re = c_re[None] * pw_re[:, :, None, :] - c_im[None] * pw_im[:, :, None, :]
    cp_im = c_re[None] * pw_im[:, :, None, :] + c_im[None] * pw_re[:, :, None, :]
    outof_c = per_octet(cp_re, -cp_im)
    decay = jnp.stack([pw_re[t_len].reshape(n_oct, SSM_OCT_STATE), pw_im[t_len].reshape(n_oct, SSM_OCT_STATE)]
                      + [jnp.zeros((n_oct, SSM_OCT_STATE), F32)] * 6, axis=1)
    skip = d_skip.reshape(n_oct, 1, LANES)
    return into_c.astype(BF16), outof_c.astype(BF16), decay, skip


def _ssm_expanders():
    row = np.arange(LANES)
    col = np.arange(2 * SSM_OCT_STATE)
    spread = (row[:, None] // SSM_STATE == col[None, :] // SSM_OCT_STATE) & (
        row[:, None] % SSM_STATE == col[None, :] % SSM_STATE)
    keep = row[:, None] // SSM_GROUP == (col[None, :] % SSM_OCT_STATE) // SSM_STATE
    return jnp.asarray(spread, BF16), jnp.asarray(keep, F32)


def _ssm_kernel(x_ref, intoc_ref, outofc_ref, decay_ref, skip_ref, spread_ref, keep_ref, y_ref,
                toep_ref, into_ref, outof_ref, loc_ref, prev_ref, *, n_chunks, n_seq):
    blk = lambda i: slice(i * LANES, (i + 1) * LANES)

    @pl.when(pl.program_id(1) == 0)
    def _build_octet_matrices():
        spread, keep = spread_ref[...], keep_ref[...]
        for s in range(SSM_CHUNK):
            into_ref[blk(s), :] = (_dot(intoc_ref[0, blk(s), :], spread) * keep).astype(BF16)
        for j in range(SSM_CHUNK + 1):
            outof_ref[blk(j), :] = (_dot(outofc_ref[0, blk(j), :], spread) * keep).astype(BF16)
        bbar = into_ref[blk(SSM_CHUNK - 1), :]
        diag = (lax.broadcasted_iota(jnp.int32, (LANES, LANES), 0)
                == lax.broadcasted_iota(jnp.int32, (LANES, LANES), 1))
        zeros = jnp.zeros((LANES, LANES), BF16)
        for j in range(SSM_CHUNK):
            lag = _dot_nt(bbar, outof_ref[blk(j), :])
            if j == 0:
                lag = lag + jnp.where(diag, skip_ref[0], 0.0)
            lag = lag.astype(BF16)
            for s in range(SSM_CHUNK - j):
                toep_ref[blk(s), blk(s + j)] = lag
            if j > 0:
                for s in range(j, SSM_CHUNK):
                    toep_ref[blk(s), blk(s - j)] = zeros

    loc_ref[...] = _dot(x_ref[...], into_ref[...])
    ar = decay_ref[0, 0:1, :]
    ai = decay_ref[0, 1:2, :]

    def step(c, carry):
        new = []
        for q in range(n_seq):
            s_re, s_im = carry[2 * q], carry[2 * q + 1]
            row = pl.ds(q * n_chunks + c, 1)
            prev_ref[row, :SSM_OCT_STATE] = s_re
            prev_ref[row, SSM_OCT_STATE:] = s_im
            new.append(ar * s_re - ai * s_im + loc_ref[row, :SSM_OCT_STATE])
            new.append(ar * s_im + ai * s_re + loc_ref[row, SSM_OCT_STATE:])
        return tuple(new)

    zero = jnp.zeros((1, SSM_OCT_STATE), F32)
    lax.fori_loop(0, n_chunks, step, (zero,) * (2 * n_seq), unroll=8)
    prev = prev_ref[...].astype(BF16)
    for j in range(SSM_OCT_IN // MXU_TILE):
        lo, hi = j * MXU_TILE, (j + 1) * MXU_TILE
        y = _dot(x_ref[:, :hi], toep_ref[:hi, lo:hi]) + _dot_nt(prev, outof_ref[lo + LANES:hi + LANES, :])
        y_ref[:, lo:hi] = jax.nn.gelu(y).astype(BF16)


def _ssm(u_rows, compact, bsz):
    into_c, outof_c, decay, skip = compact
    spread, keep = _ssm_expanders()
    n_chunks = u_rows.shape[0] // bsz
    n_seq = SSM_SEQ_PER_STEP
    per_oct = lambda *shape: pl.BlockSpec((1,) + shape, lambda o, b: (o,) + (0,) * len(shape))
    rows = pl.BlockSpec((n_seq * n_chunks, SSM_OCT_IN), lambda o, b: (b, o))
    wide = 2 * SSM_OCT_STATE
    return pl.pallas_call(
        functools.partial(_ssm_kernel, n_chunks=n_chunks, n_seq=n_seq),
        grid=(SSM_N_OCT, bsz // n_seq),
        in_specs=[rows, per_oct(SSM_OCT_IN, LANES), per_oct(SSM_OCT_IN + LANES, LANES),
                  per_oct(8, SSM_OCT_STATE), per_oct(1, LANES), _resident((LANES, wide)), _resident((LANES, wide))],
        out_specs=rows,
        out_shape=jax.ShapeDtypeStruct(u_rows.shape, BF16),
        scratch_shapes=[pltpu.VMEM((SSM_OCT_IN, SSM_OCT_IN), BF16),
                        pltpu.VMEM((SSM_OCT_IN, wide), BF16),
                        pltpu.VMEM((SSM_OCT_IN + LANES, wide), BF16),
                        pltpu.VMEM((n_seq * n_chunks, wide), F32),
                        pltpu.VMEM((n_seq * n_chunks, wide), F32)],
        compiler_params=_params(2),
        name="ssm",
    )(u_rows, into_c, outof_c, decay, skip, spread, keep)


def _t5_buckets(dist):
    dist = np.asarray(dist, np.int32)
    max_exact = REL_BUCKETS // 2
    safe = np.maximum(dist, 1).astype(np.float32)
    large = max_exact + (np.log(safe / max_exact) / np.log(REL_MAX_DIST / max_exact)
                         * (REL_BUCKETS - max_exact)).astype(np.int32)
    large = np.minimum(large, REL_BUCKETS - 1)
    return np.where(dist < max_exact, dist, large).astype(np.int32)


def _band_bias(rel_table, pattern):
    dilation = ATTN_PATTERNS[pattern][1]
    buckets = _t5_buckets(np.arange(WINDOW_KEYS + 1) * dilation)
    table = rel_table[:, pattern * HEADS_PER_PATTERN:(pattern + 1) * HEADS_PER_PATTERN].T
    runs = []
    for b in np.unique(buckets):
        runs.append(jnp.broadcast_to(table[:, b:b + 1], (HEADS_PER_PATTERN, int(np.sum(buckets == b)))))
    per_step = jnp.concatenate(runs, axis=1)
    period = 3 * SUB_Q
    masked = lambda n: jnp.full((HEADS_PER_PATTERN, n), NEG_BIG, F32)
    w = jnp.concatenate([masked(SUB_Q - 1), per_step[:, ::-1], masked(period - SUB_Q - WINDOW_KEYS)], axis=1)
    skew = jnp.tile(w, (1, SUB_Q))[:, :SUB_Q * (period - 1)].reshape(HEADS_PER_PATTERN, SUB_Q, period - 1)
    bias = skew[:, :, SUB_Q - 1:3 * SUB_Q - 1]
    first = jnp.where((np.arange(2 * SUB_Q) >= SUB_Q)[None, None, :], bias, NEG_BIG)
    return jnp.stack([bias, first])


def _attn_kernel(q_ref, kc_ref, kh_ref, vc_ref, vh_ref, bias_ref, o_ref):
    first = jnp.where(pl.program_id(2) == 0, 1, 0)
    lane = lax.broadcasted_iota(jnp.int32, (SUB_Q, 2 * HEAD_DIM), 1)
    low_half = lane < HEAD_DIM
    for j in range(TQ_ATTN // SUB_Q):
        rows = slice(j * SUB_Q, (j + 1) * SUB_Q)
        q = q_ref[0, 0, rows, :]
        if j == 0:
            k = jnp.concatenate([kh_ref[0, 0], kc_ref[0, 0, rows, :]], axis=0)
            v = jnp.concatenate([vh_ref[0, 0], vc_ref[0, 0, rows, :]], axis=0)
        else:
            k = kc_ref[0, 0, (j - 1) * SUB_Q:(j + 1) * SUB_Q, :]
            v = vc_ref[0, 0, (j - 1) * SUB_Q:(j + 1) * SUB_Q, :]
        for pair in range(HEADS_PER_PATTERN // 2):
            cols = slice(pair * 2 * HEAD_DIM, (pair + 1) * 2 * HEAD_DIM)
            q2, k2, v2 = q[:, cols], k[:, cols], v[:, cols]
            outs, lses = [], []
            for sub in range(2):
                head = 2 * pair + sub
                keep = low_half if sub == 0 else jnp.logical_not(low_half)
                qh = jnp.where(keep, q2, jnp.zeros_like(q2))
                s = _dot_nt(qh, k2)
                s = s + (bias_ref[first, head] if j == 0 else bias_ref[0, head])
                m = jnp.max(s, axis=-1, keepdims=True)
                p = jnp.exp(s - m)
                l = jnp.sum(p, axis=-1, keepdims=True)
                outs.append(_dot(p.astype(BF16), v2) / l)
                lses.append(jnp.broadcast_to(m + jnp.log(l), (SUB_Q, 2 * HEAD_DIM)))
            o_ref[0, 0, rows, cols] = jnp.where(low_half, outs[0], outs[1])
            o_ref[0, 0, rows, PATTERN_WIDTH + pair * 2 * HEAD_DIM:PATTERN_WIDTH + (pair + 1) * 2 * HEAD_DIM] = (
                jnp.where(low_half, lses[0], lses[1]))


def _dilated_attention(q, k, v, rel_table, pattern):
    bsz, d, res_len, _ = q.shape
    cur = pl.BlockSpec((1, 1, TQ_ATTN, PATTERN_WIDTH), lambda b, r, i: (b, r, i, 0))
    halo = pl.BlockSpec((1, 1, SUB_Q, PATTERN_WIDTH),
                        lambda b, r, i: (b, r, jnp.maximum(i * (TQ_ATTN // SUB_Q) - 1, 0), 0))
    return pl.pallas_call(
        _attn_kernel,
        grid=(bsz, d, res_len // TQ_ATTN),
        in_specs=[cur, cur, halo, cur, halo, _resident((2, HEADS_PER_PATTERN, SUB_Q, 2 * SUB_Q))],
        out_specs=pl.BlockSpec((1, 1, TQ_ATTN, 2 * PATTERN_WIDTH), lambda b, r, i: (b, r, i, 0)),
        out_shape=jax.ShapeDtypeStruct((bsz, d, res_len, 2 * PATTERN_WIDTH), F32),
        compiler_params=_params(3),
        name=f"attn_d{d}",
    )(q, k, k, v, v, _band_bias(rel_table, pattern))


def _memkv_kernel(mem_ref, g_ref, w_ref, k_ref, v_ref):
    mn = _rmsnorm(mem_ref[...], g_ref[...]).astype(BF16)
    k_ref[...] = _dot(mn, w_ref[:, :XATTN_WIDTH]).astype(BF16)
    v_ref[...] = _dot(mn, w_ref[:, XATTN_WIDTH:]).astype(BF16)


def _memkv(mem2d, norm_g, w_kv):
    rows = mem2d.shape[0]
    full = lambda *shape: pl.BlockSpec(shape, lambda i: (0,) * len(shape))
    return pl.pallas_call(
        _memkv_kernel,
        grid=(1,),
        in_specs=[full(rows, D_MODEL), full(1, D_MODEL), full(D_MODEL, 2 * XATTN_WIDTH)],
        out_specs=[full(rows, XATTN_WIDTH), full(rows, XATTN_WIDTH)],
        out_shape=[jax.ShapeDtypeStruct((rows, XATTN_WIDTH), BF16)] * 2,
        compiler_params=_params(1),
        name="memkv",
    )(mem2d, norm_g.reshape(1, D_MODEL), w_kv.astype(BF16))


def _merge_kernel(x_ref, y_ref, o0_ref, o1_ref, o2_ref, mk_ref, mv_ref, g_ref, wpost_ref, wglu_ref,
                  wau_ref, wxu_ref, wout_ref, out_ref, ys_ref, s1_ref, s2_ref):
    x = x_ref[...]
    un = _rmsnorm(x, g_ref[...]).astype(BF16)

    def gate(index):
        lo = XATTN_WIDTH + index * D_MODEL
        return jax.nn.sigmoid(_dot(un, wpost_ref[:, lo:lo + D_MODEL]))

    chunk_rows = TM_MERGE // SSM_CHUNK
    for o in range(SSM_N_OCT):
        for t in range(SSM_CHUNK):
            lo = o * SSM_OCT_IN + t * LANES
            ys_ref[o, pl.ds(t, chunk_rows, stride=SSM_CHUNK), :] = y_ref[:, lo:lo + LANES].astype(F32)
    ys = jnp.concatenate([ys_ref[o] for o in range(SSM_N_OCT)], axis=-1).astype(BF16)
    glu = _dot(ys, wglu_ref[:, :D_MODEL]) * jax.nn.sigmoid(_dot(ys, wglu_ref[:, D_MODEL:]))
    merged = gate(0) * glu

    slabs = 2 * PATTERN_WIDTH // LANES
    for o_ref, s_ref in ((o1_ref, s1_ref), (o2_ref, s2_ref)):
        d = o_ref.shape[1]
        for r in range(d):
            for j in range(slabs):
                s_ref[j, pl.ds(r, TM_MERGE // d, stride=d), :] = o_ref[0, r, :, j * LANES:(j + 1) * LANES]
    half = slabs // 2
    nat = lambda s_ref, lo: jnp.concatenate([s_ref[j] for j in range(lo, lo + half)], axis=-1)
    outs = [o0_ref[0, 0, :, :PATTERN_WIDTH], nat(s1_ref, 0), nat(s2_ref, 0)]
    lses = [o0_ref[0, 0, :, PATTERN_WIDTH:], nat(s1_ref, half), nat(s2_ref, half)]
    top = jnp.maximum(jnp.maximum(lses[0], lses[1]), lses[2])
    ws = [jnp.exp(l - top) for l in lses]
    att = (ws[0] * outs[0] + ws[1] * outs[1] + ws[2] * outs[2]) / (ws[0] + ws[1] + ws[2])
    merged = merged + gate(1) * _dot(att.astype(BF16), wau_ref[...])

    xq = _dot(un, wpost_ref[:, :XATTN_WIDTH]).astype(BF16)
    heads = []
    for h in range(XATTN_HEADS):
        cols = slice(h * XATTN_HEAD_DIM, (h + 1) * XATTN_HEAD_DIM)
        s = _dot_nt(xq[:, cols], mk_ref[0, :, cols]) * (XATTN_HEAD_DIM ** -0.5)
        p = jnp.exp(s - jnp.max(s, axis=-1, keepdims=True))
        heads.append(_dot(p.astype(BF16), mv_ref[0, :, cols]) / jnp.sum(p, axis=-1, keepdims=True))
    xo = jnp.concatenate(heads, axis=-1).astype(BF16)
    merged = merged + gate(2) * _dot(xo, wxu_ref[...])

    out_ref[...] = x + _dot(merged.astype(BF16), wout_ref[...])


def _merge(x, y_rows, attn_outs, mk, mv, norm_g, w_post, w_glu, w_au, w_xu, w_out, seqlen):
    n = x.shape[0]
    tiles_per_seq = seqlen // TM_MERGE
    rows = lambda w: pl.BlockSpec((TM_MERGE, w), lambda i: (i, 0))
    mem = pl.BlockSpec((1, MEM_LEN, XATTN_WIDTH), lambda i: (i // tiles_per_seq, 0, 0))
    attn_specs = [pl.BlockSpec((1, d, TM_MERGE // d, 2 * PATTERN_WIDTH),
                               lambda i: (i // tiles_per_seq, 0, i % tiles_per_seq, 0))
                  for _, d in ATTN_PATTERNS]
    slabs = lambda k: pltpu.VMEM((k, TM_MERGE, LANES), F32)
    return pl.pallas_call(
        _merge_kernel,
        grid=(n // TM_MERGE,),
        in_specs=[rows(D_MODEL), pl.BlockSpec((TM_MERGE // SSM_CHUNK, SSM_ROW), lambda i: (i, 0))] + attn_specs
                 + [mem, mem, _resident((1, D_MODEL)), _resident((D_MODEL, POST_WIDTH)),
                    _resident((SSM_WIDTH, 2 * D_MODEL)), _resident((PATTERN_WIDTH, D_MODEL)),
                    _resident((XATTN_WIDTH, D_MODEL)), _resident((D_MODEL, D_MODEL))],
        out_specs=rows(D_MODEL),
        out_shape=jax.ShapeDtypeStruct((n, D_MODEL), F32),
        scratch_shapes=[slabs(SSM_N_OCT), slabs(2 * PATTERN_WIDTH // LANES), slabs(2 * PATTERN_WIDTH // LANES)],
        compiler_params=_params(1),
        name="merge",
    )(x, y_rows, *attn_outs, mk, mv, norm_g.reshape(1, D_MODEL), w_post.astype(BF16), w_glu.astype(BF16),
      w_au.astype(BF16), w_xu.astype(BF16), w_out.astype(BF16))


def _layer(x, mem, rel_table, ffn1_norm, ffn1_w_in, ffn1_w_down, mix_norm, w_in,
           ssm_a_re, ssm_a_im, ssm_log_dt, ssm_b_re, ssm_b_im, ssm_c_re, ssm_c_im, ssm_d, ssm_w_glu,
           attn_w_up, mem_norm, xattn_w_kv, xattn_w_up, w_out, ffn2_norm, ffn2_w_in, ffn2_w_down,
           final_norm, bsz, seqlen):
    x = _ffn(x, ffn1_norm, ffn1_w_in, ffn1_w_down)
    u_rows, q, k, v = _proj(x, mix_norm, w_in[:, :PRE_WIDTH], bsz, seqlen)
    compact = _ssm_compact(ssm_a_re, ssm_a_im, ssm_log_dt, ssm_b_re, ssm_b_im, ssm_c_re, ssm_c_im, ssm_d)
    y_rows = _ssm(u_rows, compact, bsz)
    attn_outs = [_dilated_attention(q[g], k[g], v[g], rel_table, g) for g in range(N_PATTERNS)]
    mk, mv = _memkv(mem.reshape(bsz * MEM_LEN, D_MODEL), mem_norm, xattn_w_kv)
    mk = mk.reshape(bsz, MEM_LEN, XATTN_WIDTH)
    mv = mv.reshape(bsz, MEM_LEN, XATTN_WIDTH)
    x = _merge(x, y_rows, attn_outs, mk, mv, mix_norm, w_in[:, PRE_WIDTH:], ssm_w_glu, attn_w_up,
               xattn_w_up, w_out, seqlen)
    return _ffn(x, ffn2_norm, ffn2_w_in, ffn2_w_down, final_g=final_norm)


def kernel(x, mem, ffn1_norm, ffn1_w_in, ffn1_w_down, mix_norm, w_in, ssm_a_re, ssm_a_im, ssm_log_dt,
           ssm_b_re, ssm_b_im, ssm_c_re, ssm_c_im, ssm_d, ssm_w_glu, rel_table, attn_w_up, mem_norm,
           xattn_w_kv, xattn_w_up, w_out, ffn2_norm, ffn2_w_in, ffn2_w_down, final_norm):
    bsz, seqlen, _ = x.shape
    assert ffn1_norm.shape[0] == 1, "single-layer trunk"
    assert seqlen % (TQ_ATTN * ATTN_PATTERNS[-1][1]) == 0 and seqlen % TM_PROJ == 0 and seqlen % TM_MERGE == 0
    h = _layer(x.reshape(bsz * seqlen, D_MODEL), mem, rel_table, ffn1_norm[0], ffn1_w_in[0], ffn1_w_down[0],
               mix_norm[0], w_in[0], ssm_a_re[0], ssm_a_im[0], ssm_log_dt[0], ssm_b_re[0], ssm_b_im[0],
               ssm_c_re[0], ssm_c_im[0], ssm_d[0], ssm_w_glu[0], attn_w_up[0], mem_norm[0], xattn_w_kv[0],
               xattn_w_up[0], w_out[0], ffn2_norm[0], ffn2_w_in[0], ffn2_w_down[0], final_norm,
               bsz, seqlen)
    return h.reshape(bsz, seqlen, D_MODEL)
```

```python
import functools

import numpy as np
import jax
import jax.numpy as jnp
from jax import lax
from jax.experimental import pallas as pl
from jax.experimental.pallas import tpu as pltpu

F32 = jnp.float32
BF16 = jnp.bfloat16

LANES = 128
MXU_TILE = 256

D_MODEL = 1024
D_FF = 2816
EPS = 1e-6

HEAD_DIM = 64
ATTN_PATTERNS = ((128, 1), (512, 4), (2048, 16))
N_PATTERNS = 3
HEADS_PER_PATTERN = 4
N_ATTN_HEADS = N_PATTERNS * HEADS_PER_PATTERN
ATTN_WIDTH = N_ATTN_HEADS * HEAD_DIM
PATTERN_WIDTH = HEADS_PER_PATTERN * HEAD_DIM
WINDOW_KEYS = 128
REL_BUCKETS = 32
REL_MAX_DIST = 2048

SSM_GROUP = 16
SSM_STATE = 64
SSM_WIDTH = 512
SSM_GROUPS = SSM_WIDTH // SSM_GROUP
SSM_CHUNK = 16
SSM_OCT = LANES // SSM_GROUP
SSM_N_OCT = SSM_WIDTH // LANES
SSM_OCT_IN = SSM_CHUNK * LANES
SSM_OCT_STATE = SSM_OCT * SSM_STATE
SSM_ROW = SSM_CHUNK * SSM_WIDTH

MEM_LEN = 256
XATTN_HEADS = 4
XATTN_HEAD_DIM = 128
XATTN_WIDTH = XATTN_HEADS * XATTN_HEAD_DIM

PRE_WIDTH = SSM_WIDTH + 3 * ATTN_WIDTH
POST_WIDTH = XATTN_WIDTH + 3 * D_MODEL

NEG_BIG = -1e30

VMEM_LIMIT_BYTES = 56 * 1024 * 1024

TM_FFN = 1024
FFN_HALVES = 2
FFN_CHUNKS = ((0, 1536), (1536, D_FF))
TM_PROJ = 512
TM_MERGE = 512
ATTN_STEP_ROWS = 1024
SSM_SEQ_PER_STEP = 2
SUB_Q = 128


def _rmsnorm(x, g):
    return x * lax.rsqrt(jnp.mean(x * x, axis=-1, keepdims=True) + EPS) * g


def _dot(a, b):
    return jnp.dot(a, b, preferred_element_type=F32)


def _dot_nt(a, b):
    return lax.dot_general(a, b, (((1,), (1,)), ((), ())), preferred_element_type=F32)


def _resident(shape):
    nd = len(shape)
    return pl.BlockSpec(shape, lambda *_: (0,) * nd, pipeline_mode=pl.Buffered(1))


def _params(n_axes):
    return pltpu.CompilerParams(dimension_semantics=("arbitrary",) * n_axes,
                                vmem_limit_bytes=VMEM_LIMIT_BYTES)


def _ffn_kernel(x_ref, g_ref, win_ref, wdn_ref, *rest, final_norm):
    if final_norm:
        fg_ref, o_ref = rest
    else:
        (o_ref,) = rest
    for half in range(FFN_HALVES):
        rows = slice(half * TM_FFN // FFN_HALVES, (half + 1) * TM_FFN // FFN_HALVES)
        x = x_ref[rows, :]
        un = _rmsnorm(x, g_ref[...]).astype(BF16)
        acc = None
        for lo, hi in FFN_CHUNKS:
            a = _dot(un, win_ref[:, lo:hi])
            b = _dot(un, win_ref[:, D_FF + lo:D_FF + hi])
            h = (a * jax.nn.sigmoid(a) * b).astype(BF16)
            part = _dot(h, wdn_ref[lo:hi, :])
            acc = part if acc is None else acc + part
        y = x + 0.5 * acc
        if final_norm:
            y = _rmsnorm(y, fg_ref[...])
        o_ref[rows, :] = y


def _ffn(x, norm_g, w_in, w_down, final_g=None):
    n = x.shape[0]
    row = pl.BlockSpec((TM_FFN, D_MODEL), lambda i: (i, 0))
    in_specs = [row, _resident((1, D_MODEL)), _resident((D_MODEL, 2 * D_FF)), _resident((D_FF, D_MODEL))]
    args = [x, norm_g.reshape(1, D_MODEL), w_in.astype(BF16), w_down.astype(BF16)]
    if final_g is not None:
        in_specs.append(_resident((1, D_MODEL)))
        args.append(final_g.reshape(1, D_MODEL))
    return pl.pallas_call(
        functools.partial(_ffn_kernel, final_norm=final_g is not None),
        grid=(n // TM_FFN,),
        in_specs=in_specs,
        out_specs=row,
        out_shape=jax.ShapeDtypeStruct((n, D_MODEL), F32),
        compiler_params=_params(1),
        name="ffn_final" if final_g is not None else "ffn",
    )(*args)


def _proj_kernel(x_ref, g_ref, w_ref, u_ref, *rest):
    qkv_refs, (su_ref, sq_ref, sk_ref, sv_ref) = rest[:3 * N_PATTERNS], rest[3 * N_PATTERNS:]
    un = _rmsnorm(x_ref[...], g_ref[...]).astype(BF16)

    u = _dot(un, w_ref[:, :SSM_WIDTH])
    for o in range(SSM_N_OCT):
        su_ref[o] = u[:, o * LANES:(o + 1) * LANES]
    chunk_rows = TM_PROJ // SSM_CHUNK
    for o in range(SSM_N_OCT):
        for t in range(SSM_CHUNK):
            lo = o * SSM_OCT_IN + t * LANES
            u_ref[:, lo:lo + LANES] = su_ref[o, pl.ds(t, chunk_rows, stride=SSM_CHUNK), :].astype(BF16)

    for which, scr in enumerate((sq_ref, sk_ref, sv_ref)):
        lo = SSM_WIDTH + which * ATTN_WIDTH
        res = _dot(un, w_ref[:, lo:lo + ATTN_WIDTH])
        if which == 0:
            res = res * (HEAD_DIM ** -0.5)
        for j in range(ATTN_WIDTH // LANES):
            scr[j] = res[:, j * LANES:(j + 1) * LANES]
        for g, (_, d) in enumerate(ATTN_PATTERNS):
            out = qkv_refs[which * N_PATTERNS + g]
            for jj in range(PATTERN_WIDTH // LANES):
                j = g * (PATTERN_WIDTH // LANES) + jj
                for r in range(d):
                    out[0, r, :, jj * LANES:(jj + 1) * LANES] = (
                        scr[j, pl.ds(r, TM_PROJ // d, stride=d), :].astype(BF16))


def _proj(x, norm_g, w_pre, bsz, seqlen):
    n = x.shape[0]
    tiles_per_seq = seqlen // TM_PROJ
    rows = lambda w: pl.BlockSpec((TM_PROJ, w), lambda i: (i, 0))
    qkv_specs, qkv_shapes = [], []
    for _ in range(3):
        for _, d in ATTN_PATTERNS:
            qkv_specs.append(pl.BlockSpec((1, d, TM_PROJ // d, PATTERN_WIDTH),
                                          lambda i: (i // tiles_per_seq, 0, i % tiles_per_seq, 0)))
            qkv_shapes.append(jax.ShapeDtypeStruct((bsz, d, seqlen // d, PATTERN_WIDTH), BF16))
    slabs = lambda k: pltpu.VMEM((k, TM_PROJ, LANES), F32)
    outs = pl.pallas_call(
        _proj_kernel,
        grid=(n // TM_PROJ,),
        in_specs=[rows(D_MODEL), _resident((1, D_MODEL)), _resident((D_MODEL, PRE_WIDTH))],
        out_specs=[pl.BlockSpec((TM_PROJ // SSM_CHUNK, SSM_ROW), lambda i: (i, 0))] + qkv_specs,
        out_shape=[jax.ShapeDtypeStruct((n // SSM_CHUNK, SSM_ROW), BF16)] + qkv_shapes,
        scratch_shapes=[slabs(SSM_N_OCT)] + [slabs(ATTN_WIDTH // LANES)] * 3,
        compiler_params=_params(1),
        name="proj",
    )(x, norm_g.reshape(1, D_MODEL), w_pre.astype(BF16))
    u = outs[0]
    q, k, v = (outs[1 + i * N_PATTERNS:1 + (i + 1) * N_PATTERNS] for i in range(3))
    return u, q, k, v


def _ssm_compact(a_re, a_im, log_dt, b_re, b_im, c_re, c_im, d_skip):
    t_len = SSM_CHUNK
    dt = jnp.exp(log_dt)[:, None]
    mag = jnp.exp(a_re * dt)
    ang = a_im * dt
    abar_re = mag * jnp.cos(ang)
    abar_im = mag * jnp.sin(ang)
    nr = abar_re - 1.0
    ni = abar_im
    den = a_re * a_re + a_im * a_im
    coef_re = (nr * a_re + ni * a_im) / den
    coef_im = (ni * a_re - nr * a_im) / den
    bbar_re = coef_re[..., None] * b_re - coef_im[..., None] * b_im
    bbar_im = coef_re[..., None] * b_im + coef_im[..., None] * b_re
    bt_re = jnp.swapaxes(bbar_re, -1, -2)
    bt_im = jnp.swapaxes(bbar_im, -1, -2)
    pw_re = [jnp.ones_like(abar_re)]
    pw_im = [jnp.zeros_like(abar_re)]
    for _ in range(t_len):
        pr, pi = pw_re[-1], pw_im[-1]
        pw_re.append(pr * abar_re - pi * abar_im)
        pw_im.append(pr * abar_im + pi * abar_re)
    rev_re = jnp.stack(pw_re[t_len - 1::-1])[:, :, None, :]
    rev_im = jnp.stack(pw_im[t_len - 1::-1])[:, :, None, :]
    pw_re = jnp.stack(pw_re)
    pw_im = jnp.stack(pw_im)
    n_oct, oct_, grp, st = SSM_N_OCT, SSM_OCT, SSM_GROUP, SSM_STATE

    def per_octet(re, im):
        both = jnp.concatenate([re, im], axis=-1)
        n = both.shape[0]
        return both.reshape(n, n_oct, oct_ * grp, 2 * st).transpose(1, 0, 2, 3).reshape(n_oct, n * LANES, 2 * st)

    into_c = per_octet(rev_re * bt_re[None] - rev_im * bt_im[None], rev_re * bt_im[None] + rev_im * bt_re[None])
    cp_re = c_re[None] * pw_re[:, :, None, :] - c_im[None] * pw_im[:, :, None, :]
    cp_im = c_re[None] * pw_im[:, :, None, :] + c_im[None] * pw_re[:, :, None, :]
    outof_c = per_octet(cp_re, -cp_im)
    decay = jnp.stack([pw_re[t_len].reshape(n_oct, SSM_OCT_STATE), pw_im[t_len].reshape(n_oct, SSM_OCT_STATE)]
                      + [jnp.zeros((n_oct, SSM_OCT_STATE), F32)] * 6, axis=1)
    skip = d_skip.reshape(n_oct, 1, LANES)
    return into_c.astype(BF16), outof_c.astype(BF16), decay, skip


def _ssm_expanders():
    row = np.arange(LANES)
    col = np.arange(2 * SSM_OCT_STATE)
    spread = (row[:, None] // SSM_STATE == col[None, :] // SSM_OCT_STATE) & (
        row[:, None] % SSM_STATE == col[None, :] % SSM_STATE)
    keep = row[:, None] // SSM_GROUP == (col[None, :] % SSM_OCT_STATE) // SSM_STATE
    return jnp.asarray(spread, BF16), jnp.asarray(keep, F32)


def _ssm_kernel(x_ref, intoc_ref, outofc_ref, decay_ref, skip_ref, spread_ref, keep_ref, y_ref,
                toep_ref, into_ref, outof_ref, loc_ref, prev_ref, *, n_chunks, n_seq):
    blk = lambda i: slice(i * LANES, (i + 1) * LANES)

    @pl.when(pl.program_id(1) == 0)
    def _build_octet_matrices():
        spread, keep = spread_ref[...], keep_ref[...]
        for s in range(SSM_CHUNK):
            into_ref[blk(s), :] = (_dot(intoc_ref[0, blk(s), :], spread) * keep).astype(BF16)
        for j in range(SSM_CHUNK + 1):
            outof_ref[blk(j), :] = (_dot(outofc_ref[0, blk(j), :], spread) * keep).astype(BF16)
        bbar = into_ref[blk(SSM_CHUNK - 1), :]
        diag = (lax.broadcasted_iota(jnp.int32, (LANES, LANES), 0)
                == lax.broadcasted_iota(jnp.int32, (LANES, LANES), 1))
        zeros = jnp.zeros((LANES, LANES), BF16)
        for j in range(SSM_CHUNK):
            lag = _dot_nt(bbar, outof_ref[blk(j), :])
            if j == 0:
                lag = lag + jnp.where(diag, skip_ref[0], 0.0)
            lag = lag.astype(BF16)
            for s in range(SSM_CHUNK - j):
                toep_ref[blk(s), blk(s + j)] = lag
            if j > 0:
                for s in range(j, SSM_CHUNK):
                    toep_ref[blk(s), blk(s - j)] = zeros

    loc_ref[...] = _dot(x_ref[...], into_ref[...])
    ar = decay_ref[0, 0:1, :]
    ai = decay_ref[0, 1:2, :]

    def step(c, carry):
        new = []
        for q in range(n_seq):
            s_re, s_im = carry[2 * q], carry[2 * q + 1]
            row = pl.ds(q * n_chunks + c, 1)
            prev_ref[row, :SSM_OCT_STATE] = s_re
            prev_ref[row, SSM_OCT_STATE:] = s_im
            new.append(ar * s_re - ai * s_im + loc_ref[row, :SSM_OCT_STATE])
            new.append(ar * s_im + ai * s_re + loc_ref[row, SSM_OCT_STATE:])
        return tuple(new)

    zero = jnp.zeros((1, SSM_OCT_STATE), F32)
    lax.fori_loop(0, n_chunks, step, (zero,) * (2 * n_seq), unroll=8)
    prev = prev_ref[...].astype(BF16)
    for j in range(SSM_OCT_IN // MXU_TILE):
        lo, hi = j * MXU_TILE, (j + 1) * MXU_TILE
        y = _dot(x_ref[:, :hi], toep_ref[:hi, lo:hi]) + _dot_nt(prev, outof_ref[lo + LANES:hi + LANES, :])
        y_ref[:, lo:hi] = jax.nn.gelu(y).astype(BF16)


def _ssm(u_rows, compact, bsz):
    into_c, outof_c, decay, skip = compact
    spread, keep = _ssm_expanders()
    n_chunks = u_rows.shape[0] // bsz
    n_seq = SSM_SEQ_PER_STEP
    per_oct = lambda *shape: pl.BlockSpec((1,) + shape, lambda o, b: (o,) + (0,) * len(shape))
    rows = pl.BlockSpec((n_seq * n_chunks, SSM_OCT_IN), lambda o, b: (b, o))
    wide = 2 * SSM_OCT_STATE
    return pl.pallas_call(
        functools.partial(_ssm_kernel, n_chunks=n_chunks, n_seq=n_seq),
        grid=(SSM_N_OCT, bsz // n_seq),
        in_specs=[rows, per_oct(SSM_OCT_IN, LANES), per_oct(SSM_OCT_IN + LANES, LANES),
                  per_oct(8, SSM_OCT_STATE), per_oct(1, LANES), _resident((LANES, wide)), _resident((LANES, wide))],
        out_specs=rows,
        out_shape=jax.ShapeDtypeStruct(u_rows.shape, BF16),
        scratch_shapes=[pltpu.VMEM((SSM_OCT_IN, SSM_OCT_IN), BF16),
                        pltpu.VMEM((SSM_OCT_IN, wide), BF16),
                        pltpu.VMEM((SSM_OCT_IN + LANES, wide), BF16),
                        pltpu.VMEM((n_seq * n_chunks, wide), F32),
                        pltpu.VMEM((n_seq * n_chunks, wide), F32)],
        compiler_params=_params(2),
        name="ssm",
    )(u_rows, into_c, outof_c, decay, skip, spread, keep)


def _t5_buckets(dist):
    dist = np.asarray(dist, np.int32)
    max_exact = REL_BUCKETS // 2
    safe = np.maximum(dist, 1).astype(np.float32)
    large = max_exact + (np.log(safe / max_exact) / np.log(REL_MAX_DIST / max_exact)
                         * (REL_BUCKETS - max_exact)).astype(np.int32)
    large = np.minimum(large, REL_BUCKETS - 1)
    return np.where(dist < max_exact, dist, large).astype(np.int32)


def _band_bias(rel_table, pattern):
    dilation = ATTN_PATTERNS[pattern][1]
    buckets = _t5_buckets(np.arange(WINDOW_KEYS + 1) * dilation)
    table = rel_table[:, pattern * HEADS_PER_PATTERN:(pattern + 1) * HEADS_PER_PATTERN].T
    runs = []
    for b in np.unique(buckets):
        runs.append(jnp.broadcast_to(table[:, b:b + 1], (HEADS_PER_PATTERN, int(np.sum(buckets == b)))))
    per_step = jnp.concatenate(runs, axis=1)
    period = 3 * SUB_Q
    masked = lambda n: jnp.full((HEADS_PER_PATTERN, n), NEG_BIG, F32)
    w = jnp.concatenate([masked(SUB_Q - 1), per_step[:, ::-1], masked(period - SUB_Q - WINDOW_KEYS)], axis=1)
    skew = jnp.tile(w, (1, SUB_Q))[:, :SUB_Q * (period - 1)].reshape(HEADS_PER_PATTERN, SUB_Q, period - 1)
    bias = skew[:, :, SUB_Q - 1:3 * SUB_Q - 1]
    first = jnp.where((np.arange(2 * SUB_Q) >= SUB_Q)[None, None, :], bias, NEG_BIG)
    return jnp.stack([bias, first])


def _attn_kernel(q_ref, kc_ref, kh_ref, vc_ref, vh_ref, bias_ref, o_ref):
    n_res, tq = q_ref.shape[1], q_ref.shape[2]
    first = jnp.where(pl.program_id(2) == 0, 1, 0)
    lane = lax.broadcasted_iota(jnp.int32, (SUB_Q, 2 * HEAD_DIM), 1)
    low_half = lane < HEAD_DIM
    for r in range(n_res):
        for j in range(tq // SUB_Q):
            rows = slice(j * SUB_Q, (j + 1) * SUB_Q)
            q = q_ref[0, r, rows, :]
            if j == 0:
                k = jnp.concatenate([kh_ref[0, r], kc_ref[0, r, rows, :]], axis=0)
                v = jnp.concatenate([vh_ref[0, r], vc_ref[0, r, rows, :]], axis=0)
            else:
                k = kc_ref[0, r, (j - 1) * SUB_Q:(j + 1) * SUB_Q, :]
                v = vc_ref[0, r, (j - 1) * SUB_Q:(j + 1) * SUB_Q, :]
            for pair in range(HEADS_PER_PATTERN // 2):
                cols = slice(pair * 2 * HEAD_DIM, (pair + 1) * 2 * HEAD_DIM)
                q2, k2, v2 = q[:, cols], k[:, cols], v[:, cols]
                outs, lses = [], []
                for sub in range(2):
                    head = 2 * pair + sub
                    keep = low_half if sub == 0 else jnp.logical_not(low_half)
                    qh = jnp.where(keep, q2, jnp.zeros_like(q2))
                    s = _dot_nt(qh, k2)
                    s = s + (bias_ref[first, head] if j == 0 else bias_ref[0, head])
                    m = jnp.max(s, axis=-1, keepdims=True)
                    p = jnp.exp(s - m)
                    l = jnp.sum(p, axis=-1, keepdims=True)
                    outs.append(_dot(p.astype(BF16), v2) / l)
                    lses.append(jnp.broadcast_to(m + jnp.log(l), (SUB_Q, 2 * HEAD_DIM)))
                o_ref[0, r, rows, cols] = jnp.where(low_half, outs[0], outs[1])
                o_ref[0, r, rows, PATTERN_WIDTH + pair * 2 * HEAD_DIM:PATTERN_WIDTH + (pair + 1) * 2 * HEAD_DIM] = (
                    jnp.where(low_half, lses[0], lses[1]))


def _dilated_attention(q, k, v, rel_table, pattern):
    bsz, d, res_len, _ = q.shape
    tq = min(res_len, ATTN_STEP_ROWS)
    n_res = ATTN_STEP_ROWS // tq
    cur = pl.BlockSpec((1, n_res, tq, PATTERN_WIDTH), lambda b, r, i: (b, r, i, 0))
    halo = pl.BlockSpec((1, n_res, SUB_Q, PATTERN_WIDTH),
                        lambda b, r, i: (b, r, jnp.maximum(i * (tq // SUB_Q) - 1, 0), 0))
    return pl.pallas_call(
        _attn_kernel,
        grid=(bsz, d // n_res, res_len // tq),
        in_specs=[cur, cur, halo, cur, halo, _resident((2, HEADS_PER_PATTERN, SUB_Q, 2 * SUB_Q))],
        out_specs=pl.BlockSpec((1, n_res, tq, 2 * PATTERN_WIDTH), lambda b, r, i: (b, r, i, 0)),
        out_shape=jax.ShapeDtypeStruct((bsz, d, res_len, 2 * PATTERN_WIDTH), F32),
        compiler_params=_params(3),
        name=f"attn_d{d}",
    )(q, k, k, v, v, _band_bias(rel_table, pattern))


def _memkv_kernel(mem_ref, g_ref, w_ref, k_ref, v_ref):
    mn = _rmsnorm(mem_ref[...], g_ref[...]).astype(BF16)
    k_ref[...] = _dot(mn, w_ref[:, :XATTN_WIDTH]).astype(BF16)
    v_ref[...] = _dot(mn, w_ref[:, XATTN_WIDTH:]).astype(BF16)


def _memkv(mem2d, norm_g, w_kv):
    rows = mem2d.shape[0]
    full = lambda *shape: pl.BlockSpec(shape, lambda i: (0,) * len(shape))
    return pl.pallas_call(
        _memkv_kernel,
        grid=(1,),
        in_specs=[full(rows, D_MODEL), full(1, D_MODEL), full(D_MODEL, 2 * XATTN_WIDTH)],
        out_specs=[full(rows, XATTN_WIDTH), full(rows, XATTN_WIDTH)],
        out_shape=[jax.ShapeDtypeStruct((rows, XATTN_WIDTH), BF16)] * 2,
        compiler_params=_params(1),
        name="memkv",
    )(mem2d, norm_g.reshape(1, D_MODEL), w_kv.astype(BF16))


def _merge_kernel(x_ref, y_ref, o0_ref, o1_ref, o2_ref, mk_ref, mv_ref, g_ref, wpost_ref, wglu_ref,
                  wau_ref, wxu_ref, wout_ref, out_ref, ys_ref, s1_ref, s2_ref):
    x = x_ref[...]
    un = _rmsnorm(x, g_ref[...]).astype(BF16)

    def gate(index):
        lo = XATTN_WIDTH + index * D_MODEL
        return jax.nn.sigmoid(_dot(un, wpost_ref[:, lo:lo + D_MODEL]))

    chunk_rows = TM_MERGE // SSM_CHUNK
    for o in range(SSM_N_OCT):
        for t in range(SSM_CHUNK):
            lo = o * SSM_OCT_IN + t * LANES
            ys_ref[o, pl.ds(t, chunk_rows, stride=SSM_CHUNK), :] = y_ref[:, lo:lo + LANES].astype(F32)
    ys = jnp.concatenate([ys_ref[o] for o in range(SSM_N_OCT)], axis=-1).astype(BF16)
    glu = _dot(ys, wglu_ref[:, :D_MODEL]) * jax.nn.sigmoid(_dot(ys, wglu_ref[:, D_MODEL:]))
    merged = gate(0) * glu

    slabs = 2 * PATTERN_WIDTH // LANES
    for o_ref, s_ref in ((o1_ref, s1_ref), (o2_ref, s2_ref)):
        d = o_ref.shape[1]
        for r in range(d):
            for j in range(slabs):
                s_ref[j, pl.ds(r, TM_MERGE // d, stride=d), :] = o_ref[0, r, :, j * LANES:(j + 1) * LANES]
    half = slabs // 2
    nat = lambda s_ref, lo: jnp.concatenate([s_ref[j] for j in range(lo, lo + half)], axis=-1)
    outs = [o0_ref[0, 0, :, :PATTERN_WIDTH], nat(s1_ref, 0), nat(s2_ref, 0)]
    lses = [o0_ref[0, 0, :, PATTERN_WIDTH:], nat(s1_ref, half), nat(s2_ref, half)]
    top = jnp.maximum(jnp.maximum(lses[0], lses[1]), lses[2])
    ws = [jnp.exp(l - top) for l in lses]
    att = (ws[0] * outs[0] + ws[1] * outs[1] + ws[2] * outs[2]) / (ws[0] + ws[1] + ws[2])
    merged = merged + gate(1) * _dot(att.astype(BF16), wau_ref[...])

    xq = _dot(un, wpost_ref[:, :XATTN_WIDTH]).astype(BF16)
    heads = []
    for h in range(XATTN_HEADS):
        cols = slice(h * XATTN_HEAD_DIM, (h + 1) * XATTN_HEAD_DIM)
        s = _dot_nt(xq[:, cols], mk_ref[0, :, cols]) * (XATTN_HEAD_DIM ** -0.5)
        p = jnp.exp(s - jnp.max(s, axis=-1, keepdims=True))
        heads.append(_dot(p.astype(BF16), mv_ref[0, :, cols]) / jnp.sum(p, axis=-1, keepdims=True))
    xo = jnp.concatenate(heads, axis=-1).astype(BF16)
    merged = merged + gate(2) * _dot(xo, wxu_ref[...])

    out_ref[...] = x + _dot(merged.astype(BF16), wout_ref[...])


def _merge(x, y_rows, attn_outs, mk, mv, norm_g, w_post, w_glu, w_au, w_xu, w_out, seqlen):
    n = x.shape[0]
    tiles_per_seq = seqlen // TM_MERGE
    rows = lambda w: pl.BlockSpec((TM_MERGE, w), lambda i: (i, 0))
    mem = pl.BlockSpec((1, MEM_LEN, XATTN_WIDTH), lambda i: (i // tiles_per_seq, 0, 0))
    attn_specs = [pl.BlockSpec((1, d, TM_MERGE // d, 2 * PATTERN_WIDTH),
                               lambda i: (i // tiles_per_seq, 0, i % tiles_per_seq, 0))
                  for _, d in ATTN_PATTERNS]
    slabs = lambda k: pltpu.VMEM((k, TM_MERGE, LANES), F32)
    return pl.pallas_call(
        _merge_kernel,
        grid=(n // TM_MERGE,),
        in_specs=[rows(D_MODEL), pl.BlockSpec((TM_MERGE // SSM_CHUNK, SSM_ROW), lambda i: (i, 0))] + attn_specs
                 + [mem, mem, _resident((1, D_MODEL)), _resident((D_MODEL, POST_WIDTH)),
                    _resident((SSM_WIDTH, 2 * D_MODEL)), _resident((PATTERN_WIDTH, D_MODEL)),
                    _resident((XATTN_WIDTH, D_MODEL)), _resident((D_MODEL, D_MODEL))],
        out_specs=rows(D_MODEL),
        out_shape=jax.ShapeDtypeStruct((n, D_MODEL), F32),
        scratch_shapes=[slabs(SSM_N_OCT), slabs(2 * PATTERN_WIDTH // LANES), slabs(2 * PATTERN_WIDTH // LANES)],
        compiler_params=_params(1),
        name="merge",
    )(x, y_rows, *attn_outs, mk, mv, norm_g.reshape(1, D_MODEL), w_post.astype(BF16), w_glu.astype(BF16),
      w_au.astype(BF16), w_xu.astype(BF16), w_out.astype(BF16))


def _layer(x, mem, rel_table, ffn1_norm, ffn1_w_in, ffn1_w_down, mix_norm, w_in,
           ssm_a_re, ssm_a_im, ssm_log_dt, ssm_b_re, ssm_b_im, ssm_c_re, ssm_c_im, ssm_d, ssm_w_glu,
           attn_w_up, mem_norm, xattn_w_kv, xattn_w_up, w_out, ffn2_norm, ffn2_w_in, ffn2_w_down,
           final_norm, bsz, seqlen):
    x = _ffn(x, ffn1_norm, ffn1_w_in, ffn1_w_down)
    u_rows, q, k, v = _proj(x, mix_norm, w_in[:, :PRE_WIDTH], bsz, seqlen)
    compact = _ssm_compact(ssm_a_re, ssm_a_im, ssm_log_dt, ssm_b_re, ssm_b_im, ssm_c_re, ssm_c_im, ssm_d)
    y_rows = _ssm(u_rows, compact, bsz)
    attn_outs = [_dilated_attention(q[g], k[g], v[g], rel_table, g) for g in range(N_PATTERNS)]
    mk, mv = _memkv(mem.reshape(bsz * MEM_LEN, D_MODEL), mem_norm, xattn_w_kv)
    mk = mk.reshape(bsz, MEM_LEN, XATTN_WIDTH)
    mv = mv.reshape(bsz, MEM_LEN, XATTN_WIDTH)
    x = _merge(x, y_rows, attn_outs, mk, mv, mix_norm, w_in[:, PRE_WIDTH:], ssm_w_glu, attn_w_up,
               xattn_w_up, w_out, seqlen)
    return _ffn(x, ffn2_norm, ffn2_w_in, ffn2_w_down, final_g=final_norm)


def kernel(x, mem, ffn1_norm, ffn1_w_in, ffn1_w_down, mix_norm, w_in, ssm_a_re, ssm_a_im, ssm_log_dt,
           ssm_b_re, ssm_b_im, ssm_c_re, ssm_c_im, ssm_d, ssm_w_glu, rel_table, attn_w_up, mem_norm,
           xattn_w_kv, xattn_w_up, w_out, ffn2_norm, ffn2_w_in, ffn2_w_down, final_norm):
    bsz, seqlen, _ = x.shape
    assert ffn1_norm.shape[0] == 1, "single-layer trunk"
    assert seqlen % ATTN_STEP_ROWS == 0 and (seqlen // ATTN_PATTERNS[-1][1]) % SUB_Q == 0
    assert seqlen % TM_PROJ == 0 and seqlen % TM_MERGE == 0
    h = _layer(x.reshape(bsz * seqlen, D_MODEL), mem, rel_table, ffn1_norm[0], ffn1_w_in[0], ffn1_w_down[0],
               mix_norm[0], w_in[0], ssm_a_re[0], ssm_a_im[0], ssm_log_dt[0], ssm_b_re[0], ssm_b_im[0],
               ssm_c_re[0], ssm_c_im[0], ssm_d[0], ssm_w_glu[0], attn_w_up[0], mem_norm[0], xattn_w_kv[0],
               xattn_w_up[0], w_out[0], ffn2_norm[0], ffn2_w_in[0], ffn2_w_down[0], final_norm,
               bsz, seqlen)
    return h.reshape(bsz, seqlen, D_MODEL)
```

```python
import functools

import numpy as np
import jax
import jax.numpy as jnp
from jax import lax
from jax.experimental import pallas as pl
from jax.experimental.pallas import tpu as pltpu

F32 = jnp.float32
BF16 = jnp.bfloat16

LANES = 128
MXU_TILE = 256

D_MODEL = 1024
D_FF = 2816
EPS = 1e-6

HEAD_DIM = 64
ATTN_PATTERNS = ((128, 1), (512, 4), (2048, 16))
N_PATTERNS = 3
HEADS_PER_PATTERN = 4
N_ATTN_HEADS = N_PATTERNS * HEADS_PER_PATTERN
ATTN_WIDTH = N_ATTN_HEADS * HEAD_DIM
PATTERN_WIDTH = HEADS_PER_PATTERN * HEAD_DIM
WINDOW_KEYS = 128
REL_BUCKETS = 32
REL_MAX_DIST = 2048

SSM_GROUP = 16
SSM_STATE = 64
SSM_WIDTH = 512
SSM_GROUPS = SSM_WIDTH // SSM_GROUP
SSM_CHUNK = 16
SSM_OCT = LANES // SSM_GROUP
SSM_N_OCT = SSM_WIDTH // LANES
SSM_OCT_IN = SSM_CHUNK * LANES
SSM_OCT_STATE = SSM_OCT * SSM_STATE
SSM_ROW = SSM_CHUNK * SSM_WIDTH

MEM_LEN = 256
XATTN_HEADS = 4
XATTN_HEAD_DIM = 128
XATTN_WIDTH = XATTN_HEADS * XATTN_HEAD_DIM

PRE_WIDTH = SSM_WIDTH + 3 * ATTN_WIDTH
POST_WIDTH = XATTN_WIDTH + 3 * D_MODEL

NEG_BIG = -1e30

VMEM_LIMIT_BYTES = 56 * 1024 * 1024

TM_FFN = 1024
FFN_HALVES = 2
FFN_CHUNKS = ((0, 1536), (1536, D_FF))
TM_PROJ = 1024
PROJ_HALVES = 2
TM_MERGE = 512
ATTN_STEP_ROWS = 1024
SSM_SEQ_PER_STEP = 2
SUB_Q = 128


def _rmsnorm(x, g):
    return x * lax.rsqrt(jnp.mean(x * x, axis=-1, keepdims=True) + EPS) * g


def _dot(a, b):
    return jnp.dot(a, b, preferred_element_type=F32)


def _dot_nt(a, b):
    return lax.dot_general(a, b, (((1,), (1,)), ((), ())), preferred_element_type=F32)


def _resident(shape):
    nd = len(shape)
    return pl.BlockSpec(shape, lambda *_: (0,) * nd, pipeline_mode=pl.Buffered(1))


def _params(n_axes):
    return pltpu.CompilerParams(dimension_semantics=("arbitrary",) * n_axes,
                                vmem_limit_bytes=VMEM_LIMIT_BYTES)


def _ffn_kernel(x_ref, g_ref, win_ref, wdn_ref, *rest, final_norm):
    if final_norm:
        fg_ref, o_ref = rest
    else:
        (o_ref,) = rest
    for half in range(FFN_HALVES):
        rows = slice(half * TM_FFN // FFN_HALVES, (half + 1) * TM_FFN // FFN_HALVES)
        x = x_ref[rows, :]
        un = _rmsnorm(x, g_ref[...]).astype(BF16)
        acc = None
        for lo, hi in FFN_CHUNKS:
            a = _dot(un, win_ref[:, lo:hi])
            b = _dot(un, win_ref[:, D_FF + lo:D_FF + hi])
            h = (a * jax.nn.sigmoid(a) * b).astype(BF16)
            part = _dot(h, wdn_ref[lo:hi, :])
            acc = part if acc is None else acc + part
        y = x + 0.5 * acc
        if final_norm:
            y = _rmsnorm(y, fg_ref[...])
        o_ref[rows, :] = y


def _ffn(x, norm_g, w_in, w_down, final_g=None):
    n = x.shape[0]
    row = pl.BlockSpec((TM_FFN, D_MODEL), lambda i: (i, 0))
    in_specs = [row, _resident((1, D_MODEL)), _resident((D_MODEL, 2 * D_FF)), _resident((D_FF, D_MODEL))]
    args = [x, norm_g.reshape(1, D_MODEL), w_in.astype(BF16), w_down.astype(BF16)]
    if final_g is not None:
        in_specs.append(_resident((1, D_MODEL)))
        args.append(final_g.reshape(1, D_MODEL))
    return pl.pallas_call(
        functools.partial(_ffn_kernel, final_norm=final_g is not None),
        grid=(n // TM_FFN,),
        in_specs=in_specs,
        out_specs=row,
        out_shape=jax.ShapeDtypeStruct((n, D_MODEL), F32),
        compiler_params=_params(1),
        name="ffn_final" if final_g is not None else "ffn",
    )(*args)


def _proj_kernel(x_ref, g_ref, w_ref, u_ref, *rest):
    qkv_refs, slabs = rest[:3 * N_PATTERNS], rest[3 * N_PATTERNS:]
    n_rows = TM_PROJ // PROJ_HALVES
    for half in range(PROJ_HALVES):
        base = half * n_rows
        su_ref, sq_ref, sk_ref, sv_ref = slabs[4 * half:4 * half + 4]
        un = _rmsnorm(x_ref[base:base + n_rows, :], g_ref[...]).astype(BF16)

        full = _dot(un, w_ref[...])
        u = full[:, :SSM_WIDTH]
        for o in range(SSM_N_OCT):
            su_ref[o] = u[:, o * LANES:(o + 1) * LANES]
        chunks = slice(base // SSM_CHUNK, (base + n_rows) // SSM_CHUNK)
        for o in range(SSM_N_OCT):
            for t in range(SSM_CHUNK):
                lo = o * SSM_OCT_IN + t * LANES
                u_ref[chunks, lo:lo + LANES] = (
                    su_ref[o, pl.ds(t, n_rows // SSM_CHUNK, stride=SSM_CHUNK), :].astype(BF16))

        for which, scr in enumerate((sq_ref, sk_ref, sv_ref)):
            lo = SSM_WIDTH + which * ATTN_WIDTH
            res = full[:, lo:lo + ATTN_WIDTH]
            if which == 0:
                res = res * (HEAD_DIM ** -0.5)
            for j in range(ATTN_WIDTH // LANES):
                scr[j] = res[:, j * LANES:(j + 1) * LANES]
            for g, (_, d) in enumerate(ATTN_PATTERNS):
                out = qkv_refs[which * N_PATTERNS + g]
                for jj in range(PATTERN_WIDTH // LANES):
                    j = g * (PATTERN_WIDTH // LANES) + jj
                    for r in range(d):
                        out[0, r, base // d:(base + n_rows) // d, jj * LANES:(jj + 1) * LANES] = (
                            scr[j, pl.ds(r, n_rows // d, stride=d), :].astype(BF16))


def _proj(x, norm_g, w_pre, bsz, seqlen):
    n = x.shape[0]
    tiles_per_seq = seqlen // TM_PROJ
    rows = lambda w: pl.BlockSpec((TM_PROJ, w), lambda i: (i, 0))
    qkv_specs, qkv_shapes = [], []
    for _ in range(3):
        for _, d in ATTN_PATTERNS:
            qkv_specs.append(pl.BlockSpec((1, d, TM_PROJ // d, PATTERN_WIDTH),
                                          lambda i: (i // tiles_per_seq, 0, i % tiles_per_seq, 0)))
            qkv_shapes.append(jax.ShapeDtypeStruct((bsz, d, seqlen // d, PATTERN_WIDTH), BF16))
    slabs = lambda k: pltpu.VMEM((k, TM_PROJ // PROJ_HALVES, LANES), F32)
    outs = pl.pallas_call(
        _proj_kernel,
        grid=(n // TM_PROJ,),
        in_specs=[rows(D_MODEL), _resident((1, D_MODEL)), _resident((D_MODEL, PRE_WIDTH))],
        out_specs=[pl.BlockSpec((TM_PROJ // SSM_CHUNK, SSM_ROW), lambda i: (i, 0))] + qkv_specs,
        out_shape=[jax.ShapeDtypeStruct((n // SSM_CHUNK, SSM_ROW), BF16)] + qkv_shapes,
        scratch_shapes=([slabs(SSM_N_OCT)] + [slabs(ATTN_WIDTH // LANES)] * 3) * PROJ_HALVES,
        compiler_params=_params(1),
        name="proj",
    )(x, norm_g.reshape(1, D_MODEL), w_pre.astype(BF16))
    u = outs[0]
    q, k, v = (outs[1 + i * N_PATTERNS:1 + (i + 1) * N_PATTERNS] for i in range(3))
    return u, q, k, v


def _ssm_compact(a_re, a_im, log_dt, b_re, b_im, c_re, c_im, d_skip):
    t_len = SSM_CHUNK
    dt = jnp.exp(log_dt)[:, None]
    mag = jnp.exp(a_re * dt)
    ang = a_im * dt
    abar_re = mag * jnp.cos(ang)
    abar_im = mag * jnp.sin(ang)
    nr = abar_re - 1.0
    ni = abar_im
    den = a_re * a_re + a_im * a_im
    coef_re = (nr * a_re + ni * a_im) / den
    coef_im = (ni * a_re - nr * a_im) / den
    bbar_re = coef_re[..., None] * b_re - coef_im[..., None] * b_im
    bbar_im = coef_re[..., None] * b_im + coef_im[..., None] * b_re
    bt_re = jnp.swapaxes(bbar_re, -1, -2)
    bt_im = jnp.swapaxes(bbar_im, -1, -2)
    pw_re = [jnp.ones_like(abar_re)]
    pw_im = [jnp.zeros_like(abar_re)]
    for _ in range(t_len):
        pr, pi = pw_re[-1], pw_im[-1]
        pw_re.append(pr * abar_re - pi * abar_im)
        pw_im.append(pr * abar_im + pi * abar_re)
    rev_re = jnp.stack(pw_re[t_len - 1::-1])[:, :, None, :]
    rev_im = jnp.stack(pw_im[t_len - 1::-1])[:, :, None, :]
    pw_re = jnp.stack(pw_re)
    pw_im = jnp.stack(pw_im)
    n_oct, oct_, grp, st = SSM_N_OCT, SSM_OCT, SSM_GROUP, SSM_STATE

    def per_octet(re, im):
        both = jnp.concatenate([re, im], axis=-1)
        n = both.shape[0]
        return both.reshape(n, n_oct, oct_ * grp, 2 * st).transpose(1, 0, 2, 3).reshape(n_oct, n * LANES, 2 * st)

    into_c = per_octet(rev_re * bt_re[None] - rev_im * bt_im[None], rev_re * bt_im[None] + rev_im * bt_re[None])
    cp_re = c_re[None] * pw_re[:, :, None, :] - c_im[None] * pw_im[:, :, None, :]
    cp_im = c_re[None] * pw_im[:, :, None, :] + c_im[None] * pw_re[:, :, None, :]
    outof_c = per_octet(cp_re, -cp_im)
    decay = jnp.stack([pw_re[t_len].reshape(n_oct, SSM_OCT_STATE), pw_im[t_len].reshape(n_oct, SSM_OCT_STATE)]
                      + [jnp.zeros((n_oct, SSM_OCT_STATE), F32)] * 6, axis=1)
    skip = d_skip.reshape(n_oct, 1, LANES)
    return into_c.astype(BF16), outof_c.astype(BF16), decay, skip


def _ssm_expanders():
    row = np.arange(LANES)
    col = np.arange(2 * SSM_OCT_STATE)
    spread = (row[:, None] // SSM_STATE == col[None, :] // SSM_OCT_STATE) & (
        row[:, None] % SSM_STATE == col[None, :] % SSM_STATE)
    keep = row[:, None] // SSM_GROUP == (col[None, :] % SSM_OCT_STATE) // SSM_STATE
    return jnp.asarray(spread, BF16), jnp.asarray(keep, F32)


def _ssm_kernel(x_ref, intoc_ref, outofc_ref, decay_ref, skip_ref, spread_ref, keep_ref, y_ref,
                toep_ref, into_ref, outof_ref, loc_ref, prev_ref, *, n_chunks, n_seq):
    blk = lambda i: slice(i * LANES, (i + 1) * LANES)

    @pl.when(pl.program_id(1) == 0)
    def _build_octet_matrices():
        spread, keep = spread_ref[...], keep_ref[...]
        for s in range(SSM_CHUNK):
            into_ref[blk(s), :] = (_dot(intoc_ref[0, blk(s), :], spread) * keep).astype(BF16)
        for j in range(SSM_CHUNK + 1):
            outof_ref[blk(j), :] = (_dot(outofc_ref[0, blk(j), :], spread) * keep).astype(BF16)
        bbar = into_ref[blk(SSM_CHUNK - 1), :]
        diag = (lax.broadcasted_iota(jnp.int32, (LANES, LANES), 0)
                == lax.broadcasted_iota(jnp.int32, (LANES, LANES), 1))
        zeros = jnp.zeros((LANES, LANES), BF16)
        for j in range(SSM_CHUNK):
            lag = _dot_nt(bbar, outof_ref[blk(j), :])
            if j == 0:
                lag = lag + jnp.where(diag, skip_ref[0], 0.0)
            lag = lag.astype(BF16)
            for s in range(SSM_CHUNK - j):
                toep_ref[blk(s), blk(s + j)] = lag
            if j > 0:
                for s in range(j, SSM_CHUNK):
                    toep_ref[blk(s), blk(s - j)] = zeros

    loc_ref[...] = _dot(x_ref[...], into_ref[...])
    ar = decay_ref[0, 0:1, :]
    ai = decay_ref[0, 1:2, :]

    def step(c, carry):
        new = []
        for q in range(n_seq):
            s_re, s_im = carry[2 * q], carry[2 * q + 1]
            row = pl.ds(q * n_chunks + c, 1)
            prev_ref[row, :SSM_OCT_STATE] = s_re
            prev_ref[row, SSM_OCT_STATE:] = s_im
            new.append(ar * s_re - ai * s_im + loc_ref[row, :SSM_OCT_STATE])
            new.append(ar * s_im + ai * s_re + loc_ref[row, SSM_OCT_STATE:])
        return tuple(new)

    zero = jnp.zeros((1, SSM_OCT_STATE), F32)
    lax.fori_loop(0, n_chunks, step, (zero,) * (2 * n_seq), unroll=True)
    prev = prev_ref[...].astype(BF16)
    for j in range(SSM_OCT_IN // MXU_TILE):
        lo, hi = j * MXU_TILE, (j + 1) * MXU_TILE
        y = _dot(x_ref[:, :hi], toep_ref[:hi, lo:hi]) + _dot_nt(prev, outof_ref[lo + LANES:hi + LANES, :])
        y_ref[:, lo:hi] = jax.nn.gelu(y).astype(BF16)


def _ssm(u_rows, compact, bsz):
    into_c, outof_c, decay, skip = compact
    spread, keep = _ssm_expanders()
    n_chunks = u_rows.shape[0] // bsz
    n_seq = SSM_SEQ_PER_STEP
    per_oct = lambda *shape: pl.BlockSpec((1,) + shape, lambda o, b: (o,) + (0,) * len(shape))
    rows = pl.BlockSpec((n_seq * n_chunks, SSM_OCT_IN), lambda o, b: (b, o))
    wide = 2 * SSM_OCT_STATE
    return pl.pallas_call(
        functools.partial(_ssm_kernel, n_chunks=n_chunks, n_seq=n_seq),
        grid=(SSM_N_OCT, bsz // n_seq),
        in_specs=[rows, per_oct(SSM_OCT_IN, LANES), per_oct(SSM_OCT_IN + LANES, LANES),
                  per_oct(8, SSM_OCT_STATE), per_oct(1, LANES), _resident((LANES, wide)), _resident((LANES, wide))],
        out_specs=rows,
        out_shape=jax.ShapeDtypeStruct(u_rows.shape, BF16),
        scratch_shapes=[pltpu.VMEM((SSM_OCT_IN, SSM_OCT_IN), BF16),
                        pltpu.VMEM((SSM_OCT_IN, wide), BF16),
                        pltpu.VMEM((SSM_OCT_IN + LANES, wide), BF16),
                        pltpu.VMEM((n_seq * n_chunks, wide), F32),
                        pltpu.VMEM((n_seq * n_chunks, wide), F32)],
        compiler_params=_params(2),
        name="ssm",
    )(u_rows, into_c, outof_c, decay, skip, spread, keep)


def _t5_buckets(dist):
    dist = np.asarray(dist, np.int32)
    max_exact = REL_BUCKETS // 2
    safe = np.maximum(dist, 1).astype(np.float32)
    large = max_exact + (np.log(safe / max_exact) / np.log(REL_MAX_DIST / max_exact)
                         * (REL_BUCKETS - max_exact)).astype(np.int32)
    large = np.minimum(large, REL_BUCKETS - 1)
    return np.where(dist < max_exact, dist, large).astype(np.int32)


def _band_bias(rel_table, pattern):
    dilation = ATTN_PATTERNS[pattern][1]
    buckets = _t5_buckets(np.arange(WINDOW_KEYS + 1) * dilation)
    table = rel_table[:, pattern * HEADS_PER_PATTERN:(pattern + 1) * HEADS_PER_PATTERN].T
    runs = []
    for b in np.unique(buckets):
        runs.append(jnp.broadcast_to(table[:, b:b + 1], (HEADS_PER_PATTERN, int(np.sum(buckets == b)))))
    per_step = jnp.concatenate(runs, axis=1)
    period = 3 * SUB_Q
    masked = lambda n: jnp.full((HEADS_PER_PATTERN, n), NEG_BIG, F32)
    w = jnp.concatenate([masked(SUB_Q - 1), per_step[:, ::-1], masked(period - SUB_Q - WINDOW_KEYS)], axis=1)
    skew = jnp.tile(w, (1, SUB_Q))[:, :SUB_Q * (period - 1)].reshape(HEADS_PER_PATTERN, SUB_Q, period - 1)
    bias = skew[:, :, SUB_Q - 1:3 * SUB_Q - 1]
    first = jnp.where((np.arange(2 * SUB_Q) >= SUB_Q)[None, None, :], bias, NEG_BIG)
    pairs = lambda b: b.reshape(HEADS_PER_PATTERN // 2, 2 * SUB_Q, 2 * SUB_Q)
    return jnp.stack([pairs(bias), pairs(first)])


def _attn_kernel(q_ref, kc_ref, kh_ref, vc_ref, vh_ref, bias_ref, o_ref):
    n_res, tq = q_ref.shape[1], q_ref.shape[2]
    first = jnp.where(pl.program_id(2) == 0, 1, 0)
    lane = lax.broadcasted_iota(jnp.int32, (SUB_Q, 2 * HEAD_DIM), 1)
    low_half = lane < HEAD_DIM
    for r in range(n_res):
        for j in range(tq // SUB_Q):
            rows = slice(j * SUB_Q, (j + 1) * SUB_Q)
            q = q_ref[0, r, rows, :]
            if j == 0:
                k = jnp.concatenate([kh_ref[0, r], kc_ref[0, r, rows, :]], axis=0)
                v = jnp.concatenate([vh_ref[0, r], vc_ref[0, r, rows, :]], axis=0)
            else:
                k = kc_ref[0, r, (j - 1) * SUB_Q:(j + 1) * SUB_Q, :]
                v = vc_ref[0, r, (j - 1) * SUB_Q:(j + 1) * SUB_Q, :]
            for pair in range(HEADS_PER_PATTERN // 2):
                cols = slice(pair * 2 * HEAD_DIM, (pair + 1) * 2 * HEAD_DIM)
                q2, k2, v2 = q[:, cols], k[:, cols], v[:, cols]
                zero = jnp.zeros_like(q2)
                qq = jnp.concatenate([jnp.where(low_half, q2, zero), jnp.where(low_half, zero, q2)], axis=0)
                s = _dot_nt(qq, k2) + (bias_ref[first, pair] if j == 0 else bias_ref[0, pair])
                m = jnp.max(s, axis=-1, keepdims=True)
                p = jnp.exp(s - m)
                l = jnp.sum(p, axis=-1, keepdims=True)
                o = _dot(p.astype(BF16), v2) / l
                lse = jnp.broadcast_to(m + jnp.log(l), (2 * SUB_Q, 2 * HEAD_DIM))
                o_ref[0, r, rows, cols] = jnp.where(low_half, o[:SUB_Q], o[SUB_Q:])
                o_ref[0, r, rows, PATTERN_WIDTH + pair * 2 * HEAD_DIM:PATTERN_WIDTH + (pair + 1) * 2 * HEAD_DIM] = (
                    jnp.where(low_half, lse[:SUB_Q], lse[SUB_Q:]))


def _dilated_attention(q, k, v, rel_table, pattern):
    bsz, d, res_len, _ = q.shape
    tq = min(res_len, ATTN_STEP_ROWS)
    n_res = ATTN_STEP_ROWS // tq
    cur = pl.BlockSpec((1, n_res, tq, PATTERN_WIDTH), lambda b, r, i: (b, r, i, 0))
    halo = pl.BlockSpec((1, n_res, SUB_Q, PATTERN_WIDTH),
                        lambda b, r, i: (b, r, jnp.maximum(i * (tq // SUB_Q) - 1, 0), 0))
    return pl.pallas_call(
        _attn_kernel,
        grid=(bsz, d // n_res, res_len // tq),
        in_specs=[cur, cur, halo, cur, halo, _resident((2, HEADS_PER_PATTERN // 2, 2 * SUB_Q, 2 * SUB_Q))],
        out_specs=pl.BlockSpec((1, n_res, tq, 2 * PATTERN_WIDTH), lambda b, r, i: (b, r, i, 0)),
        out_shape=jax.ShapeDtypeStruct((bsz, d, res_len, 2 * PATTERN_WIDTH), F32),
        compiler_params=_params(3),
        name=f"attn_d{d}",
    )(q, k, k, v, v, _band_bias(rel_table, pattern))


def _memkv_kernel(mem_ref, g_ref, w_ref, k_ref, v_ref):
    mn = _rmsnorm(mem_ref[...], g_ref[...]).astype(BF16)
    k_ref[...] = _dot(mn, w_ref[:, :XATTN_WIDTH]).astype(BF16)
    v_ref[...] = _dot(mn, w_ref[:, XATTN_WIDTH:]).astype(BF16)


def _memkv(mem2d, norm_g, w_kv):
    rows = mem2d.shape[0]
    full = lambda *shape: pl.BlockSpec(shape, lambda i: (0,) * len(shape))
    return pl.pallas_call(
        _memkv_kernel,
        grid=(1,),
        in_specs=[full(rows, D_MODEL), full(1, D_MODEL), full(D_MODEL, 2 * XATTN_WIDTH)],
        out_specs=[full(rows, XATTN_WIDTH), full(rows, XATTN_WIDTH)],
        out_shape=[jax.ShapeDtypeStruct((rows, XATTN_WIDTH), BF16)] * 2,
        compiler_params=_params(1),
        name="memkv",
    )(mem2d, norm_g.reshape(1, D_MODEL), w_kv.astype(BF16))


def _merge_kernel(x_ref, y_ref, o0_ref, o1_ref, o2_ref, mk_ref, mv_ref, g_ref, wpost_ref, wglu_ref,
                  wau_ref, wxu_ref, wout_ref, out_ref, ys_ref, s1_ref, s2_ref):
    x = x_ref[...]
    un = _rmsnorm(x, g_ref[...]).astype(BF16)

    def gate(index):
        lo = XATTN_WIDTH + index * D_MODEL
        return jax.nn.sigmoid(_dot(un, wpost_ref[:, lo:lo + D_MODEL]))

    chunk_rows = TM_MERGE // SSM_CHUNK
    for o in range(SSM_N_OCT):
        for t in range(SSM_CHUNK):
            lo = o * SSM_OCT_IN + t * LANES
            ys_ref[o, pl.ds(t, chunk_rows, stride=SSM_CHUNK), :] = y_ref[:, lo:lo + LANES].astype(F32)
    ys = jnp.concatenate([ys_ref[o] for o in range(SSM_N_OCT)], axis=-1).astype(BF16)
    glu = _dot(ys, wglu_ref[:, :D_MODEL]) * jax.nn.sigmoid(_dot(ys, wglu_ref[:, D_MODEL:]))
    merged = gate(0) * glu

    slabs = 2 * PATTERN_WIDTH // LANES
    for o_ref, s_ref in ((o1_ref, s1_ref), (o2_ref, s2_ref)):
        d = o_ref.shape[1]
        for r in range(d):
            for j in range(slabs):
                s_ref[j, pl.ds(r, TM_MERGE // d, stride=d), :] = o_ref[0, r, :, j * LANES:(j + 1) * LANES]
    half = slabs // 2
    nat = lambda s_ref, lo: jnp.concatenate([s_ref[j] for j in range(lo, lo + half)], axis=-1)
    outs = [o0_ref[0, 0, :, :PATTERN_WIDTH], nat(s1_ref, 0), nat(s2_ref, 0)]
    lses = [o0_ref[0, 0, :, PATTERN_WIDTH:], nat(s1_ref, half), nat(s2_ref, half)]
    top = jnp.maximum(jnp.maximum(lses[0], lses[1]), lses[2])
    ws = [jnp.exp(l - top) for l in lses]
    att = (ws[0] * outs[0] + ws[1] * outs[1] + ws[2] * outs[2]) / (ws[0] + ws[1] + ws[2])
    merged = merged + gate(1) * _dot(att.astype(BF16), wau_ref[...])

    xq = _dot(un, wpost_ref[:, :XATTN_WIDTH]).astype(BF16)
    heads = []
    for h in range(XATTN_HEADS):
        cols = slice(h * XATTN_HEAD_DIM, (h + 1) * XATTN_HEAD_DIM)
        s = _dot_nt(xq[:, cols], mk_ref[0, :, cols]) * (XATTN_HEAD_DIM ** -0.5)
        p = jnp.exp(s - jnp.max(s, axis=-1, keepdims=True))
        heads.append(_dot(p.astype(BF16), mv_ref[0, :, cols]) / jnp.sum(p, axis=-1, keepdims=True))
    xo = jnp.concatenate(heads, axis=-1).astype(BF16)
    merged = merged + gate(2) * _dot(xo, wxu_ref[...])

    out_ref[...] = x + _dot(merged.astype(BF16), wout_ref[...])


def _merge(x, y_rows, attn_outs, mk, mv, norm_g, w_post, w_glu, w_au, w_xu, w_out, seqlen):
    n = x.shape[0]
    tiles_per_seq = seqlen // TM_MERGE
    rows = lambda w: pl.BlockSpec((TM_MERGE, w), lambda i: (i, 0))
    mem = pl.BlockSpec((1, MEM_LEN, XATTN_WIDTH), lambda i: (i // tiles_per_seq, 0, 0))
    attn_specs = [pl.BlockSpec((1, d, TM_MERGE // d, 2 * PATTERN_WIDTH),
                               lambda i: (i // tiles_per_seq, 0, i % tiles_per_seq, 0))
                  for _, d in ATTN_PATTERNS]
    slabs = lambda k: pltpu.VMEM((k, TM_MERGE, LANES), F32)
    return pl.pallas_call(
        _merge_kernel,
        grid=(n // TM_MERGE,),
        in_specs=[rows(D_MODEL), pl.BlockSpec((TM_MERGE // SSM_CHUNK, SSM_ROW), lambda i: (i, 0))] + attn_specs
                 + [mem, mem, _resident((1, D_MODEL)), _resident((D_MODEL, POST_WIDTH)),
                    _resident((SSM_WIDTH, 2 * D_MODEL)), _resident((PATTERN_WIDTH, D_MODEL)),
                    _resident((XATTN_WIDTH, D_MODEL)), _resident((D_MODEL, D_MODEL))],
        out_specs=rows(D_MODEL),
        out_shape=jax.ShapeDtypeStruct((n, D_MODEL), F32),
        scratch_shapes=[slabs(SSM_N_OCT), slabs(2 * PATTERN_WIDTH // LANES), slabs(2 * PATTERN_WIDTH // LANES)],
        compiler_params=_params(1),
        name="merge",
    )(x, y_rows, *attn_outs, mk, mv, norm_g.reshape(1, D_MODEL), w_post.astype(BF16), w_glu.astype(BF16),
      w_au.astype(BF16), w_xu.astype(BF16), w_out.astype(BF16))


def _layer(x, mem, rel_table, ffn1_norm, ffn1_w_in, ffn1_w_down, mix_norm, w_in,
           ssm_a_re, ssm_a_im, ssm_log_dt, ssm_b_re, ssm_b_im, ssm_c_re, ssm_c_im, ssm_d, ssm_w_glu,
           attn_w_up, mem_norm, xattn_w_kv, xattn_w_up, w_out, ffn2_norm, ffn2_w_in, ffn2_w_down,
           final_norm, bsz, seqlen):
    x = _ffn(x, ffn1_norm, ffn1_w_in, ffn1_w_down)
    u_rows, q, k, v = _proj(x, mix_norm, w_in[:, :PRE_WIDTH], bsz, seqlen)
    compact = _ssm_compact(ssm_a_re, ssm_a_im, ssm_log_dt, ssm_b_re, ssm_b_im, ssm_c_re, ssm_c_im, ssm_d)
    y_rows = _ssm(u_rows, compact, bsz)
    attn_outs = [_dilated_attention(q[g], k[g], v[g], rel_table, g) for g in range(N_PATTERNS)]
    mk, mv = _memkv(mem.reshape(bsz * MEM_LEN, D_MODEL), mem_norm, xattn_w_kv)
    mk = mk.reshape(bsz, MEM_LEN, XATTN_WIDTH)
    mv = mv.reshape(bsz, MEM_LEN, XATTN_WIDTH)
    x = _merge(x, y_rows, attn_outs, mk, mv, mix_norm, w_in[:, PRE_WIDTH:], ssm_w_glu, attn_w_up,
               xattn_w_up, w_out, seqlen)
    return _ffn(x, ffn2_norm, ffn2_w_in, ffn2_w_down, final_g=final_norm)


def kernel(x, mem, ffn1_norm, ffn1_w_in, ffn1_w_down, mix_norm, w_in, ssm_a_re, ssm_a_im, ssm_log_dt,
           ssm_b_re, ssm_b_im, ssm_c_re, ssm_c_im, ssm_d, ssm_w_glu, rel_table, attn_w_up, mem_norm,
           xattn_w_kv, xattn_w_up, w_out, ffn2_norm, ffn2_w_in, ffn2_w_down, final_norm):
    bsz, seqlen, _ = x.shape
    assert ffn1_norm.shape[0] == 1, "single-layer trunk"
    assert seqlen % ATTN_STEP_ROWS == 0 and (seqlen // ATTN_PATTERNS[-1][1]) % SUB_Q == 0
    assert seqlen % TM_PROJ == 0 and seqlen % TM_MERGE == 0
    h = _layer(x.reshape(bsz * seqlen, D_MODEL), mem, rel_table, ffn1_norm[0], ffn1_w_in[0], ffn1_w_down[0],
               mix_norm[0], w_in[0], ssm_a_re[0], ssm_a_im[0], ssm_log_dt[0], ssm_b_re[0], ssm_b_im[0],
               ssm_c_re[0], ssm_c_im[0], ssm_d[0], ssm_w_glu[0], attn_w_up[0], mem_norm[0], xattn_w_kv[0],
               xattn_w_up[0], w_out[0], ffn2_norm[0], ffn2_w_in[0], ffn2_w_down[0], final_norm,
               bsz, seqlen)
    return h.reshape(bsz, seqlen, D_MODEL)
```

```python
import functools

import numpy as np
import jax
import jax.numpy as jnp
from jax import lax
from jax.experimental import pallas as pl
from jax.experimental.pallas import tpu as pltpu

F32 = jnp.float32
BF16 = jnp.bfloat16

LANES = 128
BF16_SUBLANES = 16
MXU_TILE = 256

D_MODEL = 1024
D_FF = 2816
EPS = 1e-6

HEAD_DIM = 64
ATTN_PATTERNS = ((128, 1), (512, 4), (2048, 16))
N_PATTERNS = 3
HEADS_PER_PATTERN = 4
N_ATTN_HEADS = N_PATTERNS * HEADS_PER_PATTERN
ATTN_WIDTH = N_ATTN_HEADS * HEAD_DIM
PATTERN_WIDTH = HEADS_PER_PATTERN * HEAD_DIM
WINDOW_KEYS = 128
REL_BUCKETS = 32
REL_MAX_DIST = 2048

SSM_GROUP = 16
SSM_STATE = 64
SSM_WIDTH = 512
SSM_GROUPS = SSM_WIDTH // SSM_GROUP
SSM_CHUNK = 16
SSM_OCT = LANES // SSM_GROUP
SSM_N_OCT = SSM_WIDTH // LANES
SSM_OCT_IN = SSM_CHUNK * LANES
SSM_OCT_STATE = SSM_OCT * SSM_STATE
SSM_ROW = SSM_CHUNK * SSM_WIDTH

MEM_LEN = 256
XATTN_HEADS = 4
XATTN_HEAD_DIM = 128
XATTN_WIDTH = XATTN_HEADS * XATTN_HEAD_DIM

PRE_WIDTH = SSM_WIDTH + 3 * ATTN_WIDTH
POST_WIDTH = XATTN_WIDTH + 3 * D_MODEL

NEG_BIG = -1e30

VMEM_LIMIT_BYTES = 56 * 1024 * 1024

TM_FFN = 1024
FFN_HALVES = 2
FFN_CHUNKS = ((0, 1536), (1536, D_FF))
TM_PROJ = 1024
PROJ_HALVES = 2
TM_MERGE = 512
ATTN_STEP_ROWS = 1024
SSM_SEQ_PER_STEP = 2
SUB_Q = 128


def _rmsnorm(x, g):
    return x * lax.rsqrt(jnp.mean(x * x, axis=-1, keepdims=True) + EPS) * g


def _dot(a, b):
    return jnp.dot(a, b, preferred_element_type=F32)


def _dot_nt(a, b):
    return lax.dot_general(a, b, (((1,), (1,)), ((), ())), preferred_element_type=F32)


def _resident(shape):
    nd = len(shape)
    return pl.BlockSpec(shape, lambda *_: (0,) * nd, pipeline_mode=pl.Buffered(1))


def _params(n_axes):
    return pltpu.CompilerParams(dimension_semantics=("arbitrary",) * n_axes,
                                vmem_limit_bytes=VMEM_LIMIT_BYTES)


def _ffn_kernel(x_ref, g_ref, win_ref, wdn_ref, *rest, final_norm, n_cast):
    rest = list(rest)
    fg_ref = rest.pop(0) if final_norm else None
    cast_src, o_ref, cast_dst = rest[:n_cast], rest[n_cast], rest[n_cast + 1:]
    for half in range(FFN_HALVES):
        rows = slice(half * TM_FFN // FFN_HALVES, (half + 1) * TM_FFN // FFN_HALVES)
        x = x_ref[rows, :]
        un = _rmsnorm(x, g_ref[...]).astype(BF16)
        acc = None
        for lo, hi in FFN_CHUNKS:
            a = _dot(un, win_ref[:, lo:hi])
            b = _dot(un, win_ref[:, D_FF + lo:D_FF + hi])
            h = (a * jax.nn.sigmoid(a) * b).astype(BF16)
            part = _dot(h, wdn_ref[lo:hi, :])
            acc = part if acc is None else acc + part
        y = x + 0.5 * acc
        if final_norm:
            y = _rmsnorm(y, fg_ref[...])
        o_ref[rows, :] = y

    dst = iter(cast_dst)
    for src in cast_src:
        lo = 0
        while lo < src.shape[1]:
            out = next(dst)
            out[...] = src[:, lo:lo + out.shape[1]].astype(BF16)
            lo += out.shape[1]


def _cast_rows(n_rows, n_steps):
    rows = BF16_SUBLANES
    while n_rows % rows or n_rows // rows > n_steps:
        rows += BF16_SUBLANES
    return rows


def _ffn(x, norm_g, w_in, w_down, final_g=None, cast=()):
    n = x.shape[0]
    n_steps = n // TM_FFN
    row = pl.BlockSpec((TM_FFN, D_MODEL), lambda i: (i, 0))
    in_specs = [row, _resident((1, D_MODEL)), _resident((D_MODEL, 2 * D_FF)), _resident((D_FF, D_MODEL))]
    args = [x, norm_g.reshape(1, D_MODEL), w_in, w_down]
    if final_g is not None:
        in_specs.append(_resident((1, D_MODEL)))
        args.append(final_g.reshape(1, D_MODEL))
    out_specs, out_shapes = [row], [jax.ShapeDtypeStruct((n, D_MODEL), F32)]
    for w, widths in cast:
        assert sum(widths) == w.shape[1]
        rows = _cast_rows(w.shape[0], n_steps)
        last = w.shape[0] // rows - 1
        block = lambda width, last=last, rows=rows: pl.BlockSpec((rows, width), lambda i: (jnp.minimum(i, last), 0))
        in_specs.append(block(w.shape[1]))
        args.append(w)
        out_specs += [block(width) for width in widths]
        out_shapes += [jax.ShapeDtypeStruct((w.shape[0], width), BF16) for width in widths]
    outs = pl.pallas_call(
        functools.partial(_ffn_kernel, final_norm=final_g is not None, n_cast=len(cast)),
        grid=(n_steps,),
        in_specs=in_specs,
        out_specs=out_specs,
        out_shape=out_shapes,
        compiler_params=_params(1),
        name="ffn_final" if final_g is not None else "ffn",
    )(*args)
    return outs[0], outs[1:]


def _proj_kernel(x_ref, g_ref, w_ref, u_ref, *rest):
    qkv_refs, slabs = rest[:3 * N_PATTERNS], rest[3 * N_PATTERNS:]
    n_rows = TM_PROJ // PROJ_HALVES
    for half in range(PROJ_HALVES):
        base = half * n_rows
        su_ref, sq_ref, sk_ref, sv_ref = slabs[4 * half:4 * half + 4]
        un = _rmsnorm(x_ref[base:base + n_rows, :], g_ref[...]).astype(BF16)

        full = _dot(un, w_ref[...])
        u = full[:, :SSM_WIDTH]
        for o in range(SSM_N_OCT):
            su_ref[o] = u[:, o * LANES:(o + 1) * LANES]
        chunks = slice(base // SSM_CHUNK, (base + n_rows) // SSM_CHUNK)
        for o in range(SSM_N_OCT):
            for t in range(SSM_CHUNK):
                lo = o * SSM_OCT_IN + t * LANES
                u_ref[chunks, lo:lo + LANES] = (
                    su_ref[o, pl.ds(t, n_rows // SSM_CHUNK, stride=SSM_CHUNK), :].astype(BF16))

        for which, scr in enumerate((sq_ref, sk_ref, sv_ref)):
            lo = SSM_WIDTH + which * ATTN_WIDTH
            res = full[:, lo:lo + ATTN_WIDTH]
            if which == 0:
                res = res * (HEAD_DIM ** -0.5)
            for j in range(ATTN_WIDTH // LANES):
                scr[j] = res[:, j * LANES:(j + 1) * LANES]
            for g, (_, d) in enumerate(ATTN_PATTERNS):
                out = qkv_refs[which * N_PATTERNS + g]
                for jj in range(PATTERN_WIDTH // LANES):
                    j = g * (PATTERN_WIDTH // LANES) + jj
                    for r in range(d):
                        out[0, r, base // d:(base + n_rows) // d, jj * LANES:(jj + 1) * LANES] = (
                            scr[j, pl.ds(r, n_rows // d, stride=d), :].astype(BF16))


def _proj(x, norm_g, w_pre, bsz, seqlen):
    n = x.shape[0]
    tiles_per_seq = seqlen // TM_PROJ
    rows = lambda w: pl.BlockSpec((TM_PROJ, w), lambda i: (i, 0))
    qkv_specs, qkv_shapes = [], []
    for _ in range(3):
        for _, d in ATTN_PATTERNS:
            qkv_specs.append(pl.BlockSpec((1, d, TM_PROJ // d, PATTERN_WIDTH),
                                          lambda i: (i // tiles_per_seq, 0, i % tiles_per_seq, 0)))
            qkv_shapes.append(jax.ShapeDtypeStruct((bsz, d, seqlen // d, PATTERN_WIDTH), BF16))
    slabs = lambda k: pltpu.VMEM((k, TM_PROJ // PROJ_HALVES, LANES), F32)
    outs = pl.pallas_call(
        _proj_kernel,
        grid=(n // TM_PROJ,),
        in_specs=[rows(D_MODEL), _resident((1, D_MODEL)), _resident((D_MODEL, PRE_WIDTH))],
        out_specs=[pl.BlockSpec((TM_PROJ // SSM_CHUNK, SSM_ROW), lambda i: (i, 0))] + qkv_specs,
        out_shape=[jax.ShapeDtypeStruct((n // SSM_CHUNK, SSM_ROW), BF16)] + qkv_shapes,
        scratch_shapes=([slabs(SSM_N_OCT)] + [slabs(ATTN_WIDTH // LANES)] * 3) * PROJ_HALVES,
        compiler_params=_params(1),
        name="proj",
    )(x, norm_g.reshape(1, D_MODEL), w_pre)
    u = outs[0]
    q, k, v = (outs[1 + i * N_PATTERNS:1 + (i + 1) * N_PATTERNS] for i in range(3))
    return u, q, k, v


def _ssm_compact(a_re, a_im, log_dt, b_re, b_im, c_re, c_im, d_skip):
    t_len = SSM_CHUNK
    dt = jnp.exp(log_dt)[:, None]
    mag = jnp.exp(a_re * dt)
    ang = a_im * dt
    abar_re = mag * jnp.cos(ang)
    abar_im = mag * jnp.sin(ang)
    nr = abar_re - 1.0
    ni = abar_im
    den = a_re * a_re + a_im * a_im
    coef_re = (nr * a_re + ni * a_im) / den
    coef_im = (ni * a_re - nr * a_im) / den
    bbar_re = coef_re[..., None] * b_re - coef_im[..., None] * b_im
    bbar_im = coef_re[..., None] * b_im + coef_im[..., None] * b_re
    bt_re = jnp.swapaxes(bbar_re, -1, -2)
    bt_im = jnp.swapaxes(bbar_im, -1, -2)
    pw_re = [jnp.ones_like(abar_re)]
    pw_im = [jnp.zeros_like(abar_re)]
    for _ in range(t_len):
        pr, pi = pw_re[-1], pw_im[-1]
        pw_re.append(pr * abar_re - pi * abar_im)
        pw_im.append(pr * abar_im + pi * abar_re)
    rev_re = jnp.stack(pw_re[t_len - 1::-1])[:, :, None, :]
    rev_im = jnp.stack(pw_im[t_len - 1::-1])[:, :, None, :]
    pw_re = jnp.stack(pw_re)
    pw_im = jnp.stack(pw_im)
    n_oct, oct_, grp, st = SSM_N_OCT, SSM_OCT, SSM_GROUP, SSM_STATE

    def per_octet(re, im):
        both = jnp.concatenate([re, im], axis=-1)
        n = both.shape[0]
        return both.reshape(n, n_oct, oct_ * grp, 2 * st).transpose(1, 0, 2, 3).reshape(n_oct, n * LANES, 2 * st)

    into_c = per_octet(rev_re * bt_re[None] - rev_im * bt_im[None], rev_re * bt_im[None] + rev_im * bt_re[None])
    cp_re = c_re[None] * pw_re[:, :, None, :] - c_im[None] * pw_im[:, :, None, :]
    cp_im = c_re[None] * pw_im[:, :, None, :] + c_im[None] * pw_re[:, :, None, :]
    outof_c = per_octet(cp_re, -cp_im)
    decay = jnp.stack([pw_re[t_len].reshape(n_oct, SSM_OCT_STATE), pw_im[t_len].reshape(n_oct, SSM_OCT_STATE)]
                      + [jnp.zeros((n_oct, SSM_OCT_STATE), F32)] * 6, axis=1)
    skip = d_skip.reshape(n_oct, 1, LANES)
    return into_c.astype(BF16), outof_c.astype(BF16), decay, skip


def _ssm_expanders():
    row = np.arange(LANES)
    col = np.arange(2 * SSM_OCT_STATE)
    spread = (row[:, None] // SSM_STATE == col[None, :] // SSM_OCT_STATE) & (
        row[:, None] % SSM_STATE == col[None, :] % SSM_STATE)
    keep = row[:, None] // SSM_GROUP == (col[None, :] % SSM_OCT_STATE) // SSM_STATE
    return jnp.asarray(spread, BF16), jnp.asarray(keep, F32)


def _ssm_kernel(x_ref, intoc_ref, outofc_ref, decay_ref, skip_ref, spread_ref, keep_ref, y_ref,
                toep_ref, into_ref, outof_ref, loc_ref, prev_ref, *, n_chunks, n_seq):
    blk = lambda i: slice(i * LANES, (i + 1) * LANES)

    @pl.when(pl.program_id(1) == 0)
    def _build_octet_matrices():
        spread, keep = spread_ref[...], keep_ref[...]
        for s in range(SSM_CHUNK):
            into_ref[blk(s), :] = (_dot(intoc_ref[0, blk(s), :], spread) * keep).astype(BF16)
        for j in range(SSM_CHUNK + 1):
            outof_ref[blk(j), :] = (_dot(outofc_ref[0, blk(j), :], spread) * keep).astype(BF16)
        bbar = into_ref[blk(SSM_CHUNK - 1), :]
        diag = (lax.broadcasted_iota(jnp.int32, (LANES, LANES), 0)
                == lax.broadcasted_iota(jnp.int32, (LANES, LANES), 1))
        zeros = jnp.zeros((LANES, LANES), BF16)
        for j in range(SSM_CHUNK):
            lag = _dot_nt(bbar, outof_ref[blk(j), :])
            if j == 0:
                lag = lag + jnp.where(diag, skip_ref[0], 0.0)
            lag = lag.astype(BF16)
            for s in range(SSM_CHUNK - j):
                toep_ref[blk(s), blk(s + j)] = lag
            if j > 0:
                for s in range(j, SSM_CHUNK):
                    toep_ref[blk(s), blk(s - j)] = zeros

    loc_ref[...] = _dot(x_ref[...], into_ref[...])
    ar = decay_ref[0, 0:1, :]
    ai = decay_ref[0, 1:2, :]

    def step(c, carry):
        new = []
        for q in range(n_seq):
            s_re, s_im = carry[2 * q], carry[2 * q + 1]
            row = pl.ds(q * n_chunks + c, 1)
            prev_ref[row, :SSM_OCT_STATE] = s_re
            prev_ref[row, SSM_OCT_STATE:] = s_im
            new.append(ar * s_re - ai * s_im + loc_ref[row, :SSM_OCT_STATE])
            new.append(ar * s_im + ai * s_re + loc_ref[row, SSM_OCT_STATE:])
        return tuple(new)

    zero = jnp.zeros((1, SSM_OCT_STATE), F32)
    lax.fori_loop(0, n_chunks, step, (zero,) * (2 * n_seq), unroll=True)
    prev = prev_ref[...].astype(BF16)
    for j in range(SSM_OCT_IN // MXU_TILE):
        lo, hi = j * MXU_TILE, (j + 1) * MXU_TILE
        y = _dot(x_ref[:, :hi], toep_ref[:hi, lo:hi]) + _dot_nt(prev, outof_ref[lo + LANES:hi + LANES, :])
        y_ref[:, lo:hi] = jax.nn.gelu(y).astype(BF16)


def _ssm(u_rows, compact, bsz):
    into_c, outof_c, decay, skip = compact
    spread, keep = _ssm_expanders()
    n_chunks = u_rows.shape[0] // bsz
    n_seq = SSM_SEQ_PER_STEP
    per_oct = lambda *shape: pl.BlockSpec((1,) + shape, lambda o, b: (o,) + (0,) * len(shape))
    rows = pl.BlockSpec((n_seq * n_chunks, SSM_OCT_IN), lambda o, b: (b, o))
    wide = 2 * SSM_OCT_STATE
    return pl.pallas_call(
        functools.partial(_ssm_kernel, n_chunks=n_chunks, n_seq=n_seq),
        grid=(SSM_N_OCT, bsz // n_seq),
        in_specs=[rows, per_oct(SSM_OCT_IN, LANES), per_oct(SSM_OCT_IN + LANES, LANES),
                  per_oct(8, SSM_OCT_STATE), per_oct(1, LANES), _resident((LANES, wide)), _resident((LANES, wide))],
        out_specs=rows,
        out_shape=jax.ShapeDtypeStruct(u_rows.shape, BF16),
        scratch_shapes=[pltpu.VMEM((SSM_OCT_IN, SSM_OCT_IN), BF16),
                        pltpu.VMEM((SSM_OCT_IN, wide), BF16),
                        pltpu.VMEM((SSM_OCT_IN + LANES, wide), BF16),
                        pltpu.VMEM((n_seq * n_chunks, wide), F32),
                        pltpu.VMEM((n_seq * n_chunks, wide), F32)],
        compiler_params=_params(2),
        name="ssm",
    )(u_rows, into_c, outof_c, decay, skip, spread, keep)


def _t5_buckets(dist):
    dist = np.asarray(dist, np.int32)
    max_exact = REL_BUCKETS // 2
    safe = np.maximum(dist, 1).astype(np.float32)
    large = max_exact + (np.log(safe / max_exact) / np.log(REL_MAX_DIST / max_exact)
                         * (REL_BUCKETS - max_exact)).astype(np.int32)
    large = np.minimum(large, REL_BUCKETS - 1)
    return np.where(dist < max_exact, dist, large).astype(np.int32)


def _band_bias(rel_table, pattern):
    dilation = ATTN_PATTERNS[pattern][1]
    buckets = _t5_buckets(np.arange(WINDOW_KEYS + 1) * dilation)
    table = rel_table[:, pattern * HEADS_PER_PATTERN:(pattern + 1) * HEADS_PER_PATTERN].T
    runs = []
    for b in np.unique(buckets):
        runs.append(jnp.broadcast_to(table[:, b:b + 1], (HEADS_PER_PATTERN, int(np.sum(buckets == b)))))
    per_step = jnp.concatenate(runs, axis=1)
    period = 3 * SUB_Q
    masked = lambda n: jnp.full((HEADS_PER_PATTERN, n), NEG_BIG, F32)
    w = jnp.concatenate([masked(SUB_Q - 1), per_step[:, ::-1], masked(period - SUB_Q - WINDOW_KEYS)], axis=1)
    skew = jnp.tile(w, (1, SUB_Q))[:, :SUB_Q * (period - 1)].reshape(HEADS_PER_PATTERN, SUB_Q, period - 1)
    bias = skew[:, :, SUB_Q - 1:3 * SUB_Q - 1]
    first = jnp.where((np.arange(2 * SUB_Q) >= SUB_Q)[None, None, :], bias, NEG_BIG)
    pairs = lambda b: b.reshape(HEADS_PER_PATTERN // 2, 2 * SUB_Q, 2 * SUB_Q)
    return jnp.stack([pairs(bias), pairs(first)])


def _attn_kernel(q_ref, kc_ref, kh_ref, vc_ref, vh_ref, bias_ref, o_ref):
    n_res, tq = q_ref.shape[1], q_ref.shape[2]
    first = jnp.where(pl.program_id(2) == 0, 1, 0)
    lane = lax.broadcasted_iota(jnp.int32, (SUB_Q, 2 * HEAD_DIM), 1)
    low_half = lane < HEAD_DIM
    for r in range(n_res):
        for j in range(tq // SUB_Q):
            rows = slice(j * SUB_Q, (j + 1) * SUB_Q)
            q = q_ref[0, r, rows, :]
            if j == 0:
                k = jnp.concatenate([kh_ref[0, r], kc_ref[0, r, rows, :]], axis=0)
                v = jnp.concatenate([vh_ref[0, r], vc_ref[0, r, rows, :]], axis=0)
            else:
                k = kc_ref[0, r, (j - 1) * SUB_Q:(j + 1) * SUB_Q, :]
                v = vc_ref[0, r, (j - 1) * SUB_Q:(j + 1) * SUB_Q, :]
            for pair in range(HEADS_PER_PATTERN // 2):
                cols = slice(pair * 2 * HEAD_DIM, (pair + 1) * 2 * HEAD_DIM)
                q2, k2, v2 = q[:, cols], k[:, cols], v[:, cols]
                zero = jnp.zeros_like(q2)
                qq = jnp.concatenate([jnp.where(low_half, q2, zero), jnp.where(low_half, zero, q2)], axis=0)
                s = _dot_nt(qq, k2) + (bias_ref[first, pair] if j == 0 else bias_ref[0, pair])
                m = jnp.max(s, axis=-1, keepdims=True)
                p = jnp.exp(s - m)
                l = jnp.sum(p, axis=-1, keepdims=True)
                o = _dot(p.astype(BF16), v2) / l
                lse = jnp.broadcast_to(m + jnp.log(l), (2 * SUB_Q, 2 * HEAD_DIM))
                o_ref[0, r, rows, cols] = jnp.where(low_half, o[:SUB_Q], o[SUB_Q:])
                o_ref[0, r, rows, PATTERN_WIDTH + pair * 2 * HEAD_DIM:PATTERN_WIDTH + (pair + 1) * 2 * HEAD_DIM] = (
                    jnp.where(low_half, lse[:SUB_Q], lse[SUB_Q:]))


def _dilated_attention(q, k, v, rel_table, pattern):
    bsz, d, res_len, _ = q.shape
    tq = min(res_len, ATTN_STEP_ROWS)
    n_res = ATTN_STEP_ROWS // tq
    cur = pl.BlockSpec((1, n_res, tq, PATTERN_WIDTH), lambda b, r, i: (b, r, i, 0))
    halo = pl.BlockSpec((1, n_res, SUB_Q, PATTERN_WIDTH),
                        lambda b, r, i: (b, r, jnp.maximum(i * (tq // SUB_Q) - 1, 0), 0))
    return pl.pallas_call(
        _attn_kernel,
        grid=(bsz, d // n_res, res_len // tq),
        in_specs=[cur, cur, halo, cur, halo, _resident((2, HEADS_PER_PATTERN // 2, 2 * SUB_Q, 2 * SUB_Q))],
        out_specs=pl.BlockSpec((1, n_res, tq, 2 * PATTERN_WIDTH), lambda b, r, i: (b, r, i, 0)),
        out_shape=jax.ShapeDtypeStruct((bsz, d, res_len, 2 * PATTERN_WIDTH), F32),
        compiler_params=_params(3),
        name=f"attn_d{d}",
    )(q, k, k, v, v, _band_bias(rel_table, pattern))


def _memkv_kernel(mem_ref, g_ref, w_ref, k_ref, v_ref):
    mn = _rmsnorm(mem_ref[...], g_ref[...]).astype(BF16)
    k_ref[...] = _dot(mn, w_ref[:, :XATTN_WIDTH]).astype(BF16)
    v_ref[...] = _dot(mn, w_ref[:, XATTN_WIDTH:]).astype(BF16)


def _memkv(mem2d, norm_g, w_kv):
    rows = mem2d.shape[0]
    full = lambda *shape: pl.BlockSpec(shape, lambda i: (0,) * len(shape))
    return pl.pallas_call(
        _memkv_kernel,
        grid=(1,),
        in_specs=[full(rows, D_MODEL), full(1, D_MODEL), full(D_MODEL, 2 * XATTN_WIDTH)],
        out_specs=[full(rows, XATTN_WIDTH), full(rows, XATTN_WIDTH)],
        out_shape=[jax.ShapeDtypeStruct((rows, XATTN_WIDTH), BF16)] * 2,
        compiler_params=_params(1),
        name="memkv",
    )(mem2d, norm_g.reshape(1, D_MODEL), w_kv)


def _merge_kernel(x_ref, y_ref, o0_ref, o1_ref, o2_ref, mk_ref, mv_ref, g_ref, wpost_ref, wglu_ref,
                  wau_ref, wxu_ref, wout_ref, out_ref, ys_ref, s1_ref, s2_ref):
    x = x_ref[...]
    un = _rmsnorm(x, g_ref[...]).astype(BF16)

    def gate(index):
        lo = XATTN_WIDTH + index * D_MODEL
        return jax.nn.sigmoid(_dot(un, wpost_ref[:, lo:lo + D_MODEL]))

    slabs = 2 * PATTERN_WIDTH // LANES
    for o_ref, s_ref in ((o1_ref, s1_ref), (o2_ref, s2_ref)):
        d = o_ref.shape[1]
        for r in range(d):
            for j in range(slabs):
                s_ref[j, pl.ds(r, TM_MERGE // d, stride=d), :] = o_ref[0, r, :, j * LANES:(j + 1) * LANES]
    half = slabs // 2
    nat = lambda s_ref, lo: jnp.concatenate([s_ref[j] for j in range(lo, lo + half)], axis=-1)
    outs = [o0_ref[0, 0, :, :PATTERN_WIDTH], nat(s1_ref, 0), nat(s2_ref, 0)]
    lses = [o0_ref[0, 0, :, PATTERN_WIDTH:], nat(s1_ref, half), nat(s2_ref, half)]
    top = jnp.maximum(jnp.maximum(lses[0], lses[1]), lses[2])
    ws = [jnp.exp(l - top) for l in lses]
    att = (ws[0] * outs[0] + ws[1] * outs[1] + ws[2] * outs[2]) / (ws[0] + ws[1] + ws[2])
    att = att.astype(BF16)

    chunk_rows = TM_MERGE // SSM_CHUNK
    for o in range(SSM_N_OCT):
        for t in range(SSM_CHUNK):
            lo = o * SSM_OCT_IN + t * LANES
            ys_ref[o, pl.ds(t, chunk_rows, stride=SSM_CHUNK), :] = y_ref[:, lo:lo + LANES].astype(F32)
    ys = jnp.concatenate([ys_ref[o] for o in range(SSM_N_OCT)], axis=-1).astype(BF16)
    glu = _dot(ys, wglu_ref[:, :D_MODEL]) * jax.nn.sigmoid(_dot(ys, wglu_ref[:, D_MODEL:]))
    merged = gate(0) * glu

    xq = _dot(un, wpost_ref[:, :XATTN_WIDTH]).astype(BF16)
    heads = []
    for h in range(XATTN_HEADS):
        cols = slice(h * XATTN_HEAD_DIM, (h + 1) * XATTN_HEAD_DIM)
        s = _dot_nt(xq[:, cols], mk_ref[0, :, cols]) * (XATTN_HEAD_DIM ** -0.5)
        p = jnp.exp(s - jnp.max(s, axis=-1, keepdims=True))
        heads.append(_dot(p.astype(BF16), mv_ref[0, :, cols]) / jnp.sum(p, axis=-1, keepdims=True))
    xo = jnp.concatenate(heads, axis=-1).astype(BF16)
    merged = merged + gate(2) * _dot(xo, wxu_ref[...])

    merged = merged + gate(1) * _dot(att, wau_ref[...])

    out_ref[...] = x + _dot(merged.astype(BF16), wout_ref[...])


def _merge(x, y_rows, attn_outs, mk, mv, norm_g, w_post, w_glu, w_au, w_xu, w_out, seqlen):
    n = x.shape[0]
    tiles_per_seq = seqlen // TM_MERGE
    rows = lambda w: pl.BlockSpec((TM_MERGE, w), lambda i: (i, 0))
    mem = pl.BlockSpec((1, MEM_LEN, XATTN_WIDTH), lambda i: (i // tiles_per_seq, 0, 0))
    attn_specs = [pl.BlockSpec((1, d, TM_MERGE // d, 2 * PATTERN_WIDTH),
                               lambda i: (i // tiles_per_seq, 0, i % tiles_per_seq, 0))
                  for _, d in ATTN_PATTERNS]
    slabs = lambda k: pltpu.VMEM((k, TM_MERGE, LANES), F32)
    return pl.pallas_call(
        _merge_kernel,
        grid=(n // TM_MERGE,),
        in_specs=[rows(D_MODEL), pl.BlockSpec((TM_MERGE // SSM_CHUNK, SSM_ROW), lambda i: (i, 0))] + attn_specs
                 + [mem, mem, _resident((1, D_MODEL)), _resident((D_MODEL, POST_WIDTH)),
                    _resident((SSM_WIDTH, 2 * D_MODEL)), _resident((PATTERN_WIDTH, D_MODEL)),
                    _resident((XATTN_WIDTH, D_MODEL)), _resident((D_MODEL, D_MODEL))],
        out_specs=rows(D_MODEL),
        out_shape=jax.ShapeDtypeStruct((n, D_MODEL), F32),
        scratch_shapes=[slabs(SSM_N_OCT), slabs(2 * PATTERN_WIDTH // LANES), slabs(2 * PATTERN_WIDTH // LANES)],
        compiler_params=_params(1),
        name="merge",
    )(x, y_rows, *attn_outs, mk, mv, norm_g.reshape(1, D_MODEL), w_post, w_glu, w_au, w_xu, w_out)


def _layer(x, mem, rel_table, ffn1_norm, ffn1_w_in, ffn1_w_down, mix_norm, w_in,
           ssm_a_re, ssm_a_im, ssm_log_dt, ssm_b_re, ssm_b_im, ssm_c_re, ssm_c_im, ssm_d, ssm_w_glu,
           attn_w_up, mem_norm, xattn_w_kv, xattn_w_up, w_out, ffn2_norm, ffn2_w_in, ffn2_w_down,
           final_norm, bsz, seqlen):
    cast = [(w_in, (PRE_WIDTH, POST_WIDTH)), (ssm_w_glu, (2 * D_MODEL,)), (attn_w_up, (D_MODEL,)),
            (xattn_w_kv, (2 * XATTN_WIDTH,)), (xattn_w_up, (D_MODEL,)), (w_out, (D_MODEL,)),
            (ffn2_w_in, (2 * D_FF,)), (ffn2_w_down, (D_MODEL,))]
    x, (w_pre, w_post, w_glu, w_au, w_kv, w_xu, w_o, w2_in, w2_down) = _ffn(
        x, ffn1_norm, ffn1_w_in.astype(BF16), ffn1_w_down.astype(BF16), cast=cast)
    u_rows, q, k, v = _proj(x, mix_norm, w_pre, bsz, seqlen)
    compact = _ssm_compact(ssm_a_re, ssm_a_im, ssm_log_dt, ssm_b_re, ssm_b_im, ssm_c_re, ssm_c_im, ssm_d)
    y_rows = _ssm(u_rows, compact, bsz)
    attn_outs = [_dilated_attention(q[g], k[g], v[g], rel_table, g) for g in range(N_PATTERNS)]
    mk, mv = _memkv(mem.reshape(bsz * MEM_LEN, D_MODEL), mem_norm, w_kv)
    mk = mk.reshape(bsz, MEM_LEN, XATTN_WIDTH)
    mv = mv.reshape(bsz, MEM_LEN, XATTN_WIDTH)
    x = _merge(x, y_rows, attn_outs, mk, mv, mix_norm, w_post, w_glu, w_au, w_xu, w_o, seqlen)
    return _ffn(x, ffn2_norm, w2_in, w2_down, final_g=final_norm)[0]


def kernel(x, mem, ffn1_norm, ffn1_w_in, ffn1_w_down, mix_norm, w_in, ssm_a_re, ssm_a_im, ssm_log_dt,
           ssm_b_re, ssm_b_im, ssm_c_re, ssm_c_im, ssm_d, ssm_w_glu, rel_table, attn_w_up, mem_norm,
           xattn_w_kv, xattn_w_up, w_out, ffn2_norm, ffn2_w_in, ffn2_w_down, final_norm):
    bsz, seqlen, _ = x.shape
    assert ffn1_norm.shape[0] == 1, "single-layer trunk"
    assert seqlen % ATTN_STEP_ROWS == 0 and (seqlen // ATTN_PATTERNS[-1][1]) % SUB_Q == 0
    assert seqlen % TM_PROJ == 0 and seqlen % TM_MERGE == 0
    h = _layer(x.reshape(bsz * seqlen, D_MODEL), mem, rel_table, ffn1_norm[0], ffn1_w_in[0], ffn1_w_down[0],
               mix_norm[0], w_in[0], ssm_a_re[0], ssm_a_im[0], ssm_log_dt[0], ssm_b_re[0], ssm_b_im[0],
               ssm_c_re[0], ssm_c_im[0], ssm_d[0], ssm_w_glu[0], attn_w_up[0], mem_norm[0], xattn_w_kv[0],
               xattn_w_up[0], w_out[0], ffn2_norm[0], ffn2_w_in[0], ffn2_w_down[0], final_norm,
               bsz, seqlen)
    return h.reshape(bsz, seqlen, D_MODEL)
```

```python
import functools

import numpy as np
import jax
import jax.numpy as jnp
from jax import lax
from jax.experimental import pallas as pl
from jax.experimental.pallas import tpu as pltpu

F32 = jnp.float32
BF16 = jnp.bfloat16

LANES = 128
BF16_SUBLANES = 16
MXU_TILE = 256

D_MODEL = 1024
D_FF = 2816
EPS = 1e-6

HEAD_DIM = 64
ATTN_PATTERNS = ((128, 1), (512, 4), (2048, 16))
N_PATTERNS = 3
HEADS_PER_PATTERN = 4
N_ATTN_HEADS = N_PATTERNS * HEADS_PER_PATTERN
ATTN_WIDTH = N_ATTN_HEADS * HEAD_DIM
PATTERN_WIDTH = HEADS_PER_PATTERN * HEAD_DIM
WINDOW_KEYS = 128
REL_BUCKETS = 32
REL_MAX_DIST = 2048

SSM_GROUP = 16
SSM_STATE = 64
SSM_WIDTH = 512
SSM_GROUPS = SSM_WIDTH // SSM_GROUP
SSM_CHUNK = 16
SSM_OCT = LANES // SSM_GROUP
SSM_N_OCT = SSM_WIDTH // LANES
SSM_OCT_IN = SSM_CHUNK * LANES
SSM_OCT_STATE = SSM_OCT * SSM_STATE
SSM_ROW = SSM_CHUNK * SSM_WIDTH

MEM_LEN = 256
XATTN_HEADS = 4
XATTN_HEAD_DIM = 128
XATTN_WIDTH = XATTN_HEADS * XATTN_HEAD_DIM

PRE_WIDTH = SSM_WIDTH + 3 * ATTN_WIDTH
POST_WIDTH = XATTN_WIDTH + 3 * D_MODEL

NEG_BIG = -1e30

VMEM_LIMIT_BYTES = 56 * 1024 * 1024

TM_FFN = 1024
FFN_HALVES = 2
FFN_CHUNKS = ((0, 1536), (1536, D_FF))
TM_PROJ = 1024
PROJ_HALVES = 2
SPLIT_STRIDE = 4
TM_MERGE = 512
ATTN_STEP_ROWS = 4096
SSM_SEQ_PER_STEP = 2
SUB_Q = 128


def _rmsnorm(x, g):
    return x * lax.rsqrt(jnp.mean(x * x, axis=-1, keepdims=True) + EPS) * g


def _dot(a, b):
    return jnp.dot(a, b, preferred_element_type=F32)


def _dot_nt(a, b):
    return lax.dot_general(a, b, (((1,), (1,)), ((), ())), preferred_element_type=F32)


def _resident(shape):
    nd = len(shape)
    return pl.BlockSpec(shape, lambda *_: (0,) * nd, pipeline_mode=pl.Buffered(1))


def _params(n_axes):
    return pltpu.CompilerParams(dimension_semantics=("arbitrary",) * n_axes,
                                vmem_limit_bytes=VMEM_LIMIT_BYTES)


def _ffn_kernel(x_ref, g_ref, win_ref, wdn_ref, *rest, final_norm, n_cast):
    rest = list(rest)
    fg_ref = rest.pop(0) if final_norm else None
    cast_src, o_ref, cast_dst = rest[:n_cast], rest[n_cast], rest[n_cast + 1:]
    for half in range(FFN_HALVES):
        rows = slice(half * TM_FFN // FFN_HALVES, (half + 1) * TM_FFN // FFN_HALVES)
        x = x_ref[rows, :]
        un = _rmsnorm(x, g_ref[...]).astype(BF16)
        acc = None
        for lo, hi in FFN_CHUNKS:
            a = _dot(un, win_ref[:, lo:hi])
            b = _dot(un, win_ref[:, D_FF + lo:D_FF + hi])
            h = (a * jax.nn.sigmoid(a) * b).astype(BF16)
            part = _dot(h, wdn_ref[lo:hi, :])
            acc = part if acc is None else acc + part
        y = x + 0.5 * acc
        if final_norm:
            y = _rmsnorm(y, fg_ref[...])
        o_ref[rows, :] = y

    dst = iter(cast_dst)
    for src in cast_src:
        lo = 0
        while lo < src.shape[1]:
            out = next(dst)
            out[...] = src[:, lo:lo + out.shape[1]].astype(BF16)
            lo += out.shape[1]


def _cast_rows(n_rows, n_steps):
    rows = BF16_SUBLANES
    while n_rows % rows or n_rows // rows > n_steps:
        rows += BF16_SUBLANES
    return rows


def _ffn(x, norm_g, w_in, w_down, final_g=None, cast=()):
    n = x.shape[0]
    n_steps = n // TM_FFN
    row = pl.BlockSpec((TM_FFN, D_MODEL), lambda i: (i, 0))
    in_specs = [row, _resident((1, D_MODEL)), _resident((D_MODEL, 2 * D_FF)), _resident((D_FF, D_MODEL))]
    args = [x, norm_g.reshape(1, D_MODEL), w_in, w_down]
    if final_g is not None:
        in_specs.append(_resident((1, D_MODEL)))
        args.append(final_g.reshape(1, D_MODEL))
    out_specs, out_shapes = [row], [jax.ShapeDtypeStruct((n, D_MODEL), F32)]
    for w, widths in cast:
        assert sum(widths) == w.shape[1]
        rows = _cast_rows(w.shape[0], n_steps)
        last = w.shape[0] // rows - 1
        block = lambda width, last=last, rows=rows: pl.BlockSpec((rows, width), lambda i: (jnp.minimum(i, last), 0))
        in_specs.append(block(w.shape[1]))
        args.append(w)
        out_specs += [block(width) for width in widths]
        out_shapes += [jax.ShapeDtypeStruct((w.shape[0], width), BF16) for width in widths]
    outs = pl.pallas_call(
        functools.partial(_ffn_kernel, final_norm=final_g is not None, n_cast=len(cast)),
        grid=(n_steps,),
        in_specs=in_specs,
        out_specs=out_specs,
        out_shape=out_shapes,
        compiler_params=_params(1),
        name="ffn_final" if final_g is not None else "ffn",
    )(*args)
    return outs[0], outs[1:]


def _rows_by_residue(slab_ref, j, stage_ref, slot, n_rows, d):
    if d == 1:
        return [slab_ref[j]]
    if d == SPLIT_STRIDE:
        return [slab_ref[j, pl.ds(r, n_rows // d, stride=d), :] for r in range(d)]
    assert d == SPLIT_STRIDE * SPLIT_STRIDE
    for low in range(SPLIT_STRIDE):
        stage_ref[slot, low] = slab_ref[j, pl.ds(low, n_rows // SPLIT_STRIDE, stride=SPLIT_STRIDE), :]
    return [stage_ref[slot, r % SPLIT_STRIDE, pl.ds(r // SPLIT_STRIDE, n_rows // d, stride=SPLIT_STRIDE), :]
            for r in range(d)]


def _proj_kernel(x_ref, g_ref, w_ref, u_ref, *rest):
    qkv_refs, slabs = rest[:3 * N_PATTERNS], rest[3 * N_PATTERNS:]
    n_rows = TM_PROJ // PROJ_HALVES
    per_half = len(slabs) // PROJ_HALVES
    for half in range(PROJ_HALVES):
        base = half * n_rows
        su_ref, sq_ref, sk_ref, sv_ref, stage_ref = slabs[per_half * half:per_half * (half + 1)]
        un = _rmsnorm(x_ref[base:base + n_rows, :], g_ref[...]).astype(BF16)
        full = _dot(un, w_ref[...])
        slot = 0

        u = full[:, :SSM_WIDTH]
        chunks = slice(base // SSM_CHUNK, (base + n_rows) // SSM_CHUNK)
        for o in range(SSM_N_OCT):
            su_ref[o] = u[:, o * LANES:(o + 1) * LANES]
            for t, rows in enumerate(_rows_by_residue(su_ref, o, stage_ref, slot, n_rows, SSM_CHUNK)):
                lo = o * SSM_OCT_IN + t * LANES
                u_ref[chunks, lo:lo + LANES] = rows.astype(BF16)
            slot += 1

        for which, scr in enumerate((sq_ref, sk_ref, sv_ref)):
            lo = SSM_WIDTH + which * ATTN_WIDTH
            res = full[:, lo:lo + ATTN_WIDTH]
            if which == 0:
                res = res * (HEAD_DIM ** -0.5)
            for j in range(ATTN_WIDTH // LANES):
                scr[j] = res[:, j * LANES:(j + 1) * LANES]
            for g, (_, d) in enumerate(ATTN_PATTERNS):
                out = qkv_refs[which * N_PATTERNS + g]
                for jj in range(PATTERN_WIDTH // LANES):
                    j = g * (PATTERN_WIDTH // LANES) + jj
                    for r, rows in enumerate(_rows_by_residue(scr, j, stage_ref, slot, n_rows, d)):
                        out[0, r, base // d:(base + n_rows) // d, jj * LANES:(jj + 1) * LANES] = rows.astype(BF16)
                    slot += d == SPLIT_STRIDE * SPLIT_STRIDE


def _proj(x, norm_g, w_pre, bsz, seqlen):
    n = x.shape[0]
    tiles_per_seq = seqlen // TM_PROJ
    rows = lambda w: pl.BlockSpec((TM_PROJ, w), lambda i: (i, 0))
    qkv_specs, qkv_shapes = [], []
    for _ in range(3):
        for _, d in ATTN_PATTERNS:
            qkv_specs.append(pl.BlockSpec((1, d, TM_PROJ // d, PATTERN_WIDTH),
                                          lambda i: (i // tiles_per_seq, 0, i % tiles_per_seq, 0)))
            qkv_shapes.append(jax.ShapeDtypeStruct((bsz, d, seqlen // d, PATTERN_WIDTH), BF16))
    half_rows = TM_PROJ // PROJ_HALVES
    slabs = lambda k: pltpu.VMEM((k, half_rows, LANES), F32)
    n_staged = SSM_N_OCT + 3 * PATTERN_WIDTH // LANES
    stage = pltpu.VMEM((n_staged, SPLIT_STRIDE, half_rows // SPLIT_STRIDE, LANES), F32)
    outs = pl.pallas_call(
        _proj_kernel,
        grid=(n // TM_PROJ,),
        in_specs=[rows(D_MODEL), _resident((1, D_MODEL)), _resident((D_MODEL, PRE_WIDTH))],
        out_specs=[pl.BlockSpec((TM_PROJ // SSM_CHUNK, SSM_ROW), lambda i: (i, 0))] + qkv_specs,
        out_shape=[jax.ShapeDtypeStruct((n // SSM_CHUNK, SSM_ROW), BF16)] + qkv_shapes,
        scratch_shapes=([slabs(SSM_N_OCT)] + [slabs(ATTN_WIDTH // LANES)] * 3 + [stage]) * PROJ_HALVES,
        compiler_params=_params(1),
        name="proj",
    )(x, norm_g.reshape(1, D_MODEL), w_pre)
    u = outs[0]
    q, k, v = (outs[1 + i * N_PATTERNS:1 + (i + 1) * N_PATTERNS] for i in range(3))
    return u, q, k, v


def _ssm_compact(a_re, a_im, log_dt, b_re, b_im, c_re, c_im, d_skip):
    t_len = SSM_CHUNK
    dt = jnp.exp(log_dt)[:, None]
    mag = jnp.exp(a_re * dt)
    ang = a_im * dt
    abar_re = mag * jnp.cos(ang)
    abar_im = mag * jnp.sin(ang)
    nr = abar_re - 1.0
    ni = abar_im
    den = a_re * a_re + a_im * a_im
    coef_re = (nr * a_re + ni * a_im) / den
    coef_im = (ni * a_re - nr * a_im) / den
    bbar_re = coef_re[..., None] * b_re - coef_im[..., None] * b_im
    bbar_im = coef_re[..., None] * b_im + coef_im[..., None] * b_re
    bt_re = jnp.swapaxes(bbar_re, -1, -2)
    bt_im = jnp.swapaxes(bbar_im, -1, -2)
    pw_re = [jnp.ones_like(abar_re)]
    pw_im = [jnp.zeros_like(abar_re)]
    for _ in range(t_len):
        pr, pi = pw_re[-1], pw_im[-1]
        pw_re.append(pr * abar_re - pi * abar_im)
        pw_im.append(pr * abar_im + pi * abar_re)
    rev_re = jnp.stack(pw_re[t_len - 1::-1])[:, :, None, :]
    rev_im = jnp.stack(pw_im[t_len - 1::-1])[:, :, None, :]
    pw_re = jnp.stack(pw_re)
    pw_im = jnp.stack(pw_im)
    n_oct, oct_, grp, st = SSM_N_OCT, SSM_OCT, SSM_GROUP, SSM_STATE

    def per_octet(re, im):
        both = jnp.concatenate([re, im], axis=-1)
        n = both.shape[0]
        return both.reshape(n, n_oct, oct_ * grp, 2 * st).transpose(1, 0, 2, 3).reshape(n_oct, n * LANES, 2 * st)

    into_c = per_octet(rev_re * bt_re[None] - rev_im * bt_im[None], rev_re * bt_im[None] + rev_im * bt_re[None])
    cp_re = c_re[None] * pw_re[:, :, None, :] - c_im[None] * pw_im[:, :, None, :]
    cp_im = c_re[None] * pw_im[:, :, None, :] + c_im[None] * pw_re[:, :, None, :]
    outof_c = per_octet(cp_re, -cp_im)
    decay = jnp.stack([pw_re[t_len].reshape(n_oct, SSM_OCT_STATE), pw_im[t_len].reshape(n_oct, SSM_OCT_STATE)]
                      + [jnp.zeros((n_oct, SSM_OCT_STATE), F32)] * 6, axis=1)
    skip = d_skip.reshape(n_oct, 1, LANES)
    return into_c.astype(BF16), outof_c.astype(BF16), decay, skip


def _ssm_expanders():
    row = np.arange(LANES)
    col = np.arange(2 * SSM_OCT_STATE)
    spread = (row[:, None] // SSM_STATE == col[None, :] // SSM_OCT_STATE) & (
        row[:, None] % SSM_STATE == col[None, :] % SSM_STATE)
    keep = row[:, None] // SSM_GROUP == (col[None, :] % SSM_OCT_STATE) // SSM_STATE
    return jnp.asarray(spread, BF16), jnp.asarray(keep, F32)


def _ssm_kernel(x_ref, intoc_ref, outofc_ref, decay_ref, skip_ref, spread_ref, keep_ref, y_ref,
                toep_ref, into_ref, outof_ref, loc_ref, prev_ref, *, n_chunks, n_seq):
    blk = lambda i: slice(i * LANES, (i + 1) * LANES)

    @pl.when(pl.program_id(1) == 0)
    def _build_octet_matrices():
        spread, keep = spread_ref[...], keep_ref[...]
        for s in range(SSM_CHUNK):
            into_ref[blk(s), :] = (_dot(intoc_ref[0, blk(s), :], spread) * keep).astype(BF16)
        for j in range(SSM_CHUNK + 1):
            outof_ref[blk(j), :] = (_dot(outofc_ref[0, blk(j), :], spread) * keep).astype(BF16)
        bbar = into_ref[blk(SSM_CHUNK - 1), :]
        diag = (lax.broadcasted_iota(jnp.int32, (LANES, LANES), 0)
                == lax.broadcasted_iota(jnp.int32, (LANES, LANES), 1))
        zeros = jnp.zeros((LANES, LANES), BF16)
        for j in range(SSM_CHUNK):
            lag = _dot_nt(bbar, outof_ref[blk(j), :])
            if j == 0:
                lag = lag + jnp.where(diag, skip_ref[0], 0.0)
            lag = lag.astype(BF16)
            for s in range(SSM_CHUNK - j):
                toep_ref[blk(s), blk(s + j)] = lag
            if j > 0:
                for s in range(j, SSM_CHUNK):
                    toep_ref[blk(s), blk(s - j)] = zeros

    loc_ref[...] = _dot(x_ref[...], into_ref[...])
    ar = decay_ref[0, 0:1, :]
    ai = decay_ref[0, 1:2, :]

    def step(c, carry):
        new = []
        for q in range(n_seq):
            s_re, s_im = carry[2 * q], carry[2 * q + 1]
            row = pl.ds(q * n_chunks + c, 1)
            prev_ref[row, :SSM_OCT_STATE] = s_re
            prev_ref[row, SSM_OCT_STATE:] = s_im
            new.append(ar * s_re - ai * s_im + loc_ref[row, :SSM_OCT_STATE])
            new.append(ar * s_im + ai * s_re + loc_ref[row, SSM_OCT_STATE:])
        return tuple(new)

    zero = jnp.zeros((1, SSM_OCT_STATE), F32)
    lax.fori_loop(0, n_chunks, step, (zero,) * (2 * n_seq), unroll=True)
    prev = prev_ref[...].astype(BF16)
    for j in range(SSM_OCT_IN // MXU_TILE):
        lo, hi = j * MXU_TILE, (j + 1) * MXU_TILE
        y = _dot(x_ref[:, :hi], toep_ref[:hi, lo:hi]) + _dot_nt(prev, outof_ref[lo + LANES:hi + LANES, :])
        y_ref[:, lo:hi] = jax.nn.gelu(y).astype(BF16)


def _ssm(u_rows, compact, bsz):
    into_c, outof_c, decay, skip = compact
    spread, keep = _ssm_expanders()
    n_chunks = u_rows.shape[0] // bsz
    n_seq = SSM_SEQ_PER_STEP
    per_oct = lambda *shape: pl.BlockSpec((1,) + shape, lambda o, b: (o,) + (0,) * len(shape))
    rows = pl.BlockSpec((n_seq * n_chunks, SSM_OCT_IN), lambda o, b: (b, o))
    wide = 2 * SSM_OCT_STATE
    return pl.pallas_call(
        functools.partial(_ssm_kernel, n_chunks=n_chunks, n_seq=n_seq),
        grid=(SSM_N_OCT, bsz // n_seq),
        in_specs=[rows, per_oct(SSM_OCT_IN, LANES), per_oct(SSM_OCT_IN + LANES, LANES),
                  per_oct(8, SSM_OCT_STATE), per_oct(1, LANES), _resident((LANES, wide)), _resident((LANES, wide))],
        out_specs=rows,
        out_shape=jax.ShapeDtypeStruct(u_rows.shape, BF16),
        scratch_shapes=[pltpu.VMEM((SSM_OCT_IN, SSM_OCT_IN), BF16),
                        pltpu.VMEM((SSM_OCT_IN, wide), BF16),
                        pltpu.VMEM((SSM_OCT_IN + LANES, wide), BF16),
                        pltpu.VMEM((n_seq * n_chunks, wide), F32),
                        pltpu.VMEM((n_seq * n_chunks, wide), F32)],
        compiler_params=_params(2),
        name="ssm",
    )(u_rows, into_c, outof_c, decay, skip, spread, keep)


def _t5_buckets(dist):
    dist = np.asarray(dist, np.int32)
    max_exact = REL_BUCKETS // 2
    safe = np.maximum(dist, 1).astype(np.float32)
    large = max_exact + (np.log(safe / max_exact) / np.log(REL_MAX_DIST / max_exact)
                         * (REL_BUCKETS - max_exact)).astype(np.int32)
    large = np.minimum(large, REL_BUCKETS - 1)
    return np.where(dist < max_exact, dist, large).astype(np.int32)


def _band_bias(rel_table, pattern):
    dilation = ATTN_PATTERNS[pattern][1]
    buckets = _t5_buckets(np.arange(WINDOW_KEYS + 1) * dilation)
    table = rel_table[:, pattern * HEADS_PER_PATTERN:(pattern + 1) * HEADS_PER_PATTERN].T
    runs = []
    for b in np.unique(buckets):
        runs.append(jnp.broadcast_to(table[:, b:b + 1], (HEADS_PER_PATTERN, int(np.sum(buckets == b)))))
    per_step = jnp.concatenate(runs, axis=1)
    period = 3 * SUB_Q
    masked = lambda n: jnp.full((HEADS_PER_PATTERN, n), NEG_BIG, F32)
    w = jnp.concatenate([masked(SUB_Q - 1), per_step[:, ::-1], masked(period - SUB_Q - WINDOW_KEYS)], axis=1)
    skew = jnp.tile(w, (1, SUB_Q))[:, :SUB_Q * (period - 1)].reshape(HEADS_PER_PATTERN, SUB_Q, period - 1)
    bias = skew[:, :, SUB_Q - 1:3 * SUB_Q - 1]
    first = jnp.where((np.arange(2 * SUB_Q) >= SUB_Q)[None, None, :], bias, NEG_BIG)
    pairs = lambda b: b.reshape(HEADS_PER_PATTERN // 2, 2 * SUB_Q, 2 * SUB_Q)
    return jnp.stack([pairs(bias), pairs(first)])


def _attn_kernel(q_ref, kc_ref, kh_ref, vc_ref, vh_ref, bias_ref, o_ref):
    n_res, tq = q_ref.shape[1], q_ref.shape[2]
    first = jnp.where(pl.program_id(2) == 0, 1, 0)
    lane = lax.broadcasted_iota(jnp.int32, (SUB_Q, 2 * HEAD_DIM), 1)
    low_half = lane < HEAD_DIM
    for r in range(n_res):
        for j in range(tq // SUB_Q):
            rows = slice(j * SUB_Q, (j + 1) * SUB_Q)
            q = q_ref[0, r, rows, :]
            if j == 0:
                k = jnp.concatenate([kh_ref[0, r], kc_ref[0, r, rows, :]], axis=0)
                v = jnp.concatenate([vh_ref[0, r], vc_ref[0, r, rows, :]], axis=0)
            else:
                k = kc_ref[0, r, (j - 1) * SUB_Q:(j + 1) * SUB_Q, :]
                v = vc_ref[0, r, (j - 1) * SUB_Q:(j + 1) * SUB_Q, :]
            for pair in range(HEADS_PER_PATTERN // 2):
                cols = slice(pair * 2 * HEAD_DIM, (pair + 1) * 2 * HEAD_DIM)
                q2, k2, v2 = q[:, cols], k[:, cols], v[:, cols]
                zero = jnp.zeros_like(q2)
                qq = jnp.concatenate([jnp.where(low_half, q2, zero), jnp.where(low_half, zero, q2)], axis=0)
                s = _dot_nt(qq, k2) + (bias_ref[first, pair] if j == 0 else bias_ref[0, pair])
                m = jnp.max(s, axis=-1, keepdims=True)
                p = jnp.exp(s - m)
                l = jnp.sum(p, axis=-1, keepdims=True)
                o = _dot(p.astype(BF16), v2) / l
                lse = jnp.broadcast_to(m + jnp.log(l), (2 * SUB_Q, 2 * HEAD_DIM))
                o_ref[0, r, rows, cols] = jnp.where(low_half, o[:SUB_Q], o[SUB_Q:])
                o_ref[0, r, rows, PATTERN_WIDTH + pair * 2 * HEAD_DIM:PATTERN_WIDTH + (pair + 1) * 2 * HEAD_DIM] = (
                    jnp.where(low_half, lse[:SUB_Q], lse[SUB_Q:]))


def _dilated_attention(q, k, v, rel_table, pattern):
    bsz, d, res_len, _ = q.shape
    tq = min(res_len, ATTN_STEP_ROWS)
    n_res = ATTN_STEP_ROWS // tq
    cur = pl.BlockSpec((1, n_res, tq, PATTERN_WIDTH), lambda b, r, i: (b, r, i, 0))
    halo = pl.BlockSpec((1, n_res, SUB_Q, PATTERN_WIDTH),
                        lambda b, r, i: (b, r, jnp.maximum(i * (tq // SUB_Q) - 1, 0), 0))
    return pl.pallas_call(
        _attn_kernel,
        grid=(bsz, d // n_res, res_len // tq),
        in_specs=[cur, cur, halo, cur, halo, _resident((2, HEADS_PER_PATTERN // 2, 2 * SUB_Q, 2 * SUB_Q))],
        out_specs=pl.BlockSpec((1, n_res, tq, 2 * PATTERN_WIDTH), lambda b, r, i: (b, r, i, 0)),
        out_shape=jax.ShapeDtypeStruct((bsz, d, res_len, 2 * PATTERN_WIDTH), F32),
        compiler_params=_params(3),
        name=f"attn_d{d}",
    )(q, k, k, v, v, _band_bias(rel_table, pattern))


def _memkv_kernel(mem_ref, g_ref, w_ref, k_ref, v_ref):
    mn = _rmsnorm(mem_ref[...], g_ref[...]).astype(BF16)
    k_ref[...] = _dot(mn, w_ref[:, :XATTN_WIDTH]).astype(BF16)
    v_ref[...] = _dot(mn, w_ref[:, XATTN_WIDTH:]).astype(BF16)


def _memkv(mem2d, norm_g, w_kv):
    rows = mem2d.shape[0]
    full = lambda *shape: pl.BlockSpec(shape, lambda i: (0,) * len(shape))
    return pl.pallas_call(
        _memkv_kernel,
        grid=(1,),
        in_specs=[full(rows, D_MODEL), full(1, D_MODEL), full(D_MODEL, 2 * XATTN_WIDTH)],
        out_specs=[full(rows, XATTN_WIDTH), full(rows, XATTN_WIDTH)],
        out_shape=[jax.ShapeDtypeStruct((rows, XATTN_WIDTH), BF16)] * 2,
        compiler_params=_params(1),
        name="memkv",
    )(mem2d, norm_g.reshape(1, D_MODEL), w_kv)


def _merge_kernel(x_ref, y_ref, o0_ref, o1_ref, o2_ref, mk_ref, mv_ref, g_ref, wpost_ref, wglu_ref,
                  wau_ref, wxu_ref, wout_ref, out_ref, ys_ref, s1_ref, s2_ref):
    x = x_ref[...]
    un = _rmsnorm(x, g_ref[...]).astype(BF16)

    def gate(index):
        lo = XATTN_WIDTH + index * D_MODEL
        return jax.nn.sigmoid(_dot(un, wpost_ref[:, lo:lo + D_MODEL]))

    slabs = 2 * PATTERN_WIDTH // LANES
    for o_ref, s_ref in ((o1_ref, s1_ref), (o2_ref, s2_ref)):
        d = o_ref.shape[1]
        for r in range(d):
            for j in range(slabs):
                s_ref[j, pl.ds(r, TM_MERGE // d, stride=d), :] = o_ref[0, r, :, j * LANES:(j + 1) * LANES]
    half = slabs // 2
    nat = lambda s_ref, lo: jnp.concatenate([s_ref[j] for j in range(lo, lo + half)], axis=-1)
    outs = [o0_ref[0, 0, :, :PATTERN_WIDTH], nat(s1_ref, 0), nat(s2_ref, 0)]
    lses = [o0_ref[0, 0, :, PATTERN_WIDTH:], nat(s1_ref, half), nat(s2_ref, half)]
    top = jnp.maximum(jnp.maximum(lses[0], lses[1]), lses[2])
    ws = [jnp.exp(l - top) for l in lses]
    att = (ws[0] * outs[0] + ws[1] * outs[1] + ws[2] * outs[2]) / (ws[0] + ws[1] + ws[2])
    att = att.astype(BF16)

    chunk_rows = TM_MERGE // SSM_CHUNK
    for o in range(SSM_N_OCT):
        for t in range(SSM_CHUNK):
            lo = o * SSM_OCT_IN + t * LANES
            ys_ref[o, pl.ds(t, chunk_rows, stride=SSM_CHUNK), :] = y_ref[:, lo:lo + LANES].astype(F32)
    ys = jnp.concatenate([ys_ref[o] for o in range(SSM_N_OCT)], axis=-1).astype(BF16)
    glu = _dot(ys, wglu_ref[:, :D_MODEL]) * jax.nn.sigmoid(_dot(ys, wglu_ref[:, D_MODEL:]))
    merged = gate(0) * glu

    xq = _dot(un, wpost_ref[:, :XATTN_WIDTH]).astype(BF16)
    heads = []
    for h in range(XATTN_HEADS):
        cols = slice(h * XATTN_HEAD_DIM, (h + 1) * XATTN_HEAD_DIM)
        s = _dot_nt(xq[:, cols], mk_ref[0, :, cols]) * (XATTN_HEAD_DIM ** -0.5)
        p = jnp.exp(s - jnp.max(s, axis=-1, keepdims=True))
        heads.append(_dot(p.astype(BF16), mv_ref[0, :, cols]) / jnp.sum(p, axis=-1, keepdims=True))
    xo = jnp.concatenate(heads, axis=-1).astype(BF16)
    merged = merged + gate(2) * _dot(xo, wxu_ref[...])

    merged = merged + gate(1) * _dot(att, wau_ref[...])

    out_ref[...] = x + _dot(merged.astype(BF16), wout_ref[...])


def _merge(x, y_rows, attn_outs, mk, mv, norm_g, w_post, w_glu, w_au, w_xu, w_out, seqlen):
    n = x.shape[0]
    tiles_per_seq = seqlen // TM_MERGE
    rows = lambda w: pl.BlockSpec((TM_MERGE, w), lambda i: (i, 0))
    mem = pl.BlockSpec((1, MEM_LEN, XATTN_WIDTH), lambda i: (i // tiles_per_seq, 0, 0))
    attn_specs = [pl.BlockSpec((1, d, TM_MERGE // d, 2 * PATTERN_WIDTH),
                               lambda i: (i // tiles_per_seq, 0, i % tiles_per_seq, 0))
                  for _, d in ATTN_PATTERNS]
    slabs = lambda k: pltpu.VMEM((k, TM_MERGE, LANES), F32)
    return pl.pallas_call(
        _merge_kernel,
        grid=(n // TM_MERGE,),
        in_specs=[rows(D_MODEL), pl.BlockSpec((TM_MERGE // SSM_CHUNK, SSM_ROW), lambda i: (i, 0))] + attn_specs
                 + [mem, mem, _resident((1, D_MODEL)), _resident((D_MODEL, POST_WIDTH)),
                    _resident((SSM_WIDTH, 2 * D_MODEL)), _resident((PATTERN_WIDTH, D_MODEL)),
                    _resident((XATTN_WIDTH, D_MODEL)), _resident((D_MODEL, D_MODEL))],
        out_specs=rows(D_MODEL),
        out_shape=jax.ShapeDtypeStruct((n, D_MODEL), F32),
        scratch_shapes=[slabs(SSM_N_OCT), slabs(2 * PATTERN_WIDTH // LANES), slabs(2 * PATTERN_WIDTH // LANES)],
        compiler_params=_params(1),
        name="merge",
    )(x, y_rows, *attn_outs, mk, mv, norm_g.reshape(1, D_MODEL), w_post, w_glu, w_au, w_xu, w_out)


def _layer(x, mem, rel_table, ffn1_norm, ffn1_w_in, ffn1_w_down, mix_norm, w_in,
           ssm_a_re, ssm_a_im, ssm_log_dt, ssm_b_re, ssm_b_im, ssm_c_re, ssm_c_im, ssm_d, ssm_w_glu,
           attn_w_up, mem_norm, xattn_w_kv, xattn_w_up, w_out, ffn2_norm, ffn2_w_in, ffn2_w_down,
           final_norm, bsz, seqlen):
    cast = [(w_in, (PRE_WIDTH, POST_WIDTH)), (ssm_w_glu, (2 * D_MODEL,)), (attn_w_up, (D_MODEL,)),
            (xattn_w_kv, (2 * XATTN_WIDTH,)), (xattn_w_up, (D_MODEL,)), (w_out, (D_MODEL,)),
            (ffn2_w_in, (2 * D_FF,)), (ffn2_w_down, (D_MODEL,))]
    x, (w_pre, w_post, w_glu, w_au, w_kv, w_xu, w_o, w2_in, w2_down) = _ffn(
        x, ffn1_norm, ffn1_w_in.astype(BF16), ffn1_w_down.astype(BF16), cast=cast)
    u_rows, q, k, v = _proj(x, mix_norm, w_pre, bsz, seqlen)
    compact = _ssm_compact(ssm_a_re, ssm_a_im, ssm_log_dt, ssm_b_re, ssm_b_im, ssm_c_re, ssm_c_im, ssm_d)
    y_rows = _ssm(u_rows, compact, bsz)
    attn_outs = [_dilated_attention(q[g], k[g], v[g], rel_table, g) for g in range(N_PATTERNS)]
    mk, mv = _memkv(mem.reshape(bsz * MEM_LEN, D_MODEL), mem_norm, w_kv)
    mk = mk.reshape(bsz, MEM_LEN, XATTN_WIDTH)
    mv = mv.reshape(bsz, MEM_LEN, XATTN_WIDTH)
    x = _merge(x, y_rows, attn_outs, mk, mv, mix_norm, w_post, w_glu, w_au, w_xu, w_o, seqlen)
    return _ffn(x, ffn2_norm, w2_in, w2_down, final_g=final_norm)[0]


def kernel(x, mem, ffn1_norm, ffn1_w_in, ffn1_w_down, mix_norm, w_in, ssm_a_re, ssm_a_im, ssm_log_dt,
           ssm_b_re, ssm_b_im, ssm_c_re, ssm_c_im, ssm_d, ssm_w_glu, rel_table, attn_w_up, mem_norm,
           xattn_w_kv, xattn_w_up, w_out, ffn2_norm, ffn2_w_in, ffn2_w_down, final_norm):
    bsz, seqlen, _ = x.shape
    assert ffn1_norm.shape[0] == 1, "single-layer trunk"
    assert seqlen % ATTN_STEP_ROWS == 0 and (seqlen // ATTN_PATTERNS[-1][1]) % SUB_Q == 0
    assert seqlen % TM_PROJ == 0 and seqlen % TM_MERGE == 0
    h = _layer(x.reshape(bsz * seqlen, D_MODEL), mem, rel_table, ffn1_norm[0], ffn1_w_in[0], ffn1_w_down[0],
               mix_norm[0], w_in[0], ssm_a_re[0], ssm_a_im[0], ssm_log_dt[0], ssm_b_re[0], ssm_b_im[0],
               ssm_c_re[0], ssm_c_im[0], ssm_d[0], ssm_w_glu[0], attn_w_up[0], mem_norm[0], xattn_w_kv[0],
               xattn_w_up[0], w_out[0], ffn2_norm[0], ffn2_w_in[0], ffn2_w_down[0], final_norm,
               bsz, seqlen)
    return h.reshape(bsz, seqlen, D_MODEL)
```

```python
import functools

import numpy as np
import jax
import jax.numpy as jnp
from jax import lax
from jax.experimental import pallas as pl
from jax.experimental.pallas import tpu as pltpu

F32 = jnp.float32
BF16 = jnp.bfloat16

LANES = 128
BF16_SUBLANES = 16
MXU_TILE = 256

D_MODEL = 1024
D_FF = 2816
EPS = 1e-6

HEAD_DIM = 64
ATTN_PATTERNS = ((128, 1), (512, 4), (2048, 16))
N_PATTERNS = 3
HEADS_PER_PATTERN = 4
N_ATTN_HEADS = N_PATTERNS * HEADS_PER_PATTERN
ATTN_WIDTH = N_ATTN_HEADS * HEAD_DIM
PATTERN_WIDTH = HEADS_PER_PATTERN * HEAD_DIM
WINDOW_KEYS = 128
REL_BUCKETS = 32
REL_MAX_DIST = 2048

SSM_GROUP = 16
SSM_STATE = 64
SSM_WIDTH = 512
SSM_GROUPS = SSM_WIDTH // SSM_GROUP
SSM_CHUNK = 16
SSM_OCT = LANES // SSM_GROUP
SSM_N_OCT = SSM_WIDTH // LANES
SSM_OCT_IN = SSM_CHUNK * LANES
SSM_OCT_STATE = SSM_OCT * SSM_STATE
SSM_ROW = SSM_CHUNK * SSM_WIDTH

MEM_LEN = 256
XATTN_HEADS = 4
XATTN_HEAD_DIM = 128
XATTN_WIDTH = XATTN_HEADS * XATTN_HEAD_DIM

PRE_WIDTH = SSM_WIDTH + 3 * ATTN_WIDTH
POST_WIDTH = XATTN_WIDTH + 3 * D_MODEL

NEG_BIG = -1e30

VMEM_LIMIT_BYTES = 56 * 1024 * 1024

TM_FFN = 1024
FFN_HALVES = 2
FFN_CHUNKS = ((0, 1536), (1536, D_FF))
TM_PROJ = 1024
PROJ_HALVES = 2
SPLIT_STRIDE = 4
TM_MERGE = 512
ATTN_STEP_ROWS = 4096
SSM_SEQ_PER_STEP = 2
SUB_Q = 128


def _rmsnorm(x, g):
    return x * lax.rsqrt(jnp.mean(x * x, axis=-1, keepdims=True) + EPS) * g


def _sigmoid(x):
    return 0.5 * jnp.tanh(0.5 * x) + 0.5


def _dot(a, b):
    return jnp.dot(a, b, preferred_element_type=F32)


def _dot_nt(a, b):
    return lax.dot_general(a, b, (((1,), (1,)), ((), ())), preferred_element_type=F32)


def _resident(shape):
    nd = len(shape)
    return pl.BlockSpec(shape, lambda *_: (0,) * nd, pipeline_mode=pl.Buffered(1))


def _params(n_axes):
    return pltpu.CompilerParams(dimension_semantics=("arbitrary",) * n_axes,
                                vmem_limit_bytes=VMEM_LIMIT_BYTES)


def _ffn_kernel(x_ref, g_ref, win_ref, wdn_ref, *rest, final_norm, n_cast):
    rest = list(rest)
    fg_ref = rest.pop(0) if final_norm else None
    cast_src, o_ref, cast_dst = rest[:n_cast], rest[n_cast], rest[n_cast + 1:]
    for half in range(FFN_HALVES):
        rows = slice(half * TM_FFN // FFN_HALVES, (half + 1) * TM_FFN // FFN_HALVES)
        x = x_ref[rows, :]
        un = _rmsnorm(x, g_ref[...]).astype(BF16)
        acc = None
        for lo, hi in FFN_CHUNKS:
            a = _dot(un, win_ref[:, lo:hi])
            b = _dot(un, win_ref[:, D_FF + lo:D_FF + hi])
            h = (a * jax.nn.sigmoid(a) * b).astype(BF16)
            part = _dot(h, wdn_ref[lo:hi, :])
            acc = part if acc is None else acc + part
        y = x + 0.5 * acc
        if final_norm:
            y = _rmsnorm(y, fg_ref[...])
        o_ref[rows, :] = y

    dst = iter(cast_dst)
    for src in cast_src:
        lo = 0
        while lo < src.shape[1]:
            out = next(dst)
            out[...] = src[:, lo:lo + out.shape[1]].astype(BF16)
            lo += out.shape[1]


def _cast_rows(n_rows, n_steps):
    rows = BF16_SUBLANES
    while n_rows % rows or n_rows // rows > n_steps:
        rows += BF16_SUBLANES
    return rows


def _ffn(x, norm_g, w_in, w_down, final_g=None, cast=()):
    n = x.shape[0]
    n_steps = n // TM_FFN
    row = pl.BlockSpec((TM_FFN, D_MODEL), lambda i: (i, 0))
    in_specs = [row, _resident((1, D_MODEL)), _resident((D_MODEL, 2 * D_FF)), _resident((D_FF, D_MODEL))]
    args = [x, norm_g.reshape(1, D_MODEL), w_in, w_down]
    if final_g is not None:
        in_specs.append(_resident((1, D_MODEL)))
        args.append(final_g.reshape(1, D_MODEL))
    out_specs, out_shapes = [row], [jax.ShapeDtypeStruct((n, D_MODEL), F32)]
    for w, widths in cast:
        assert sum(widths) == w.shape[1]
        rows = _cast_rows(w.shape[0], n_steps)
        last = w.shape[0] // rows - 1
        block = lambda width, last=last, rows=rows: pl.BlockSpec((rows, width), lambda i: (jnp.minimum(i, last), 0))
        in_specs.append(block(w.shape[1]))
        args.append(w)
        out_specs += [block(width) for width in widths]
        out_shapes += [jax.ShapeDtypeStruct((w.shape[0], width), BF16) for width in widths]
    outs = pl.pallas_call(
        functools.partial(_ffn_kernel, final_norm=final_g is not None, n_cast=len(cast)),
        grid=(n_steps,),
        in_specs=in_specs,
        out_specs=out_specs,
        out_shape=out_shapes,
        compiler_params=_params(1),
        name="ffn_final" if final_g is not None else "ffn",
    )(*args)
    return outs[0], outs[1:]


def _rows_by_residue(slab_ref, j, stage_ref, slot, n_rows, d):
    if d == 1:
        return [slab_ref[j]]
    if d == SPLIT_STRIDE:
        return [slab_ref[j, pl.ds(r, n_rows // d, stride=d), :] for r in range(d)]
    assert d == SPLIT_STRIDE * SPLIT_STRIDE
    for low in range(SPLIT_STRIDE):
        stage_ref[slot, low] = slab_ref[j, pl.ds(low, n_rows // SPLIT_STRIDE, stride=SPLIT_STRIDE), :]
    return [stage_ref[slot, r % SPLIT_STRIDE, pl.ds(r // SPLIT_STRIDE, n_rows // d, stride=SPLIT_STRIDE), :]
            for r in range(d)]


def _proj_kernel(x_ref, g_ref, w_ref, u_ref, *rest):
    qkv_refs, slabs = rest[:3 * N_PATTERNS], rest[3 * N_PATTERNS:]
    n_rows = TM_PROJ // PROJ_HALVES
    per_half = len(slabs) // PROJ_HALVES
    for half in range(PROJ_HALVES):
        base = half * n_rows
        su_ref, sq_ref, sk_ref, sv_ref, stage_ref = slabs[per_half * half:per_half * (half + 1)]
        un = _rmsnorm(x_ref[base:base + n_rows, :], g_ref[...]).astype(BF16)
        full = _dot(un, w_ref[...])
        slot = 0

        u = full[:, :SSM_WIDTH]
        chunks = slice(base // SSM_CHUNK, (base + n_rows) // SSM_CHUNK)
        for o in range(SSM_N_OCT):
            su_ref[o] = u[:, o * LANES:(o + 1) * LANES]
            for t, rows in enumerate(_rows_by_residue(su_ref, o, stage_ref, slot, n_rows, SSM_CHUNK)):
                lo = o * SSM_OCT_IN + t * LANES
                u_ref[chunks, lo:lo + LANES] = rows.astype(BF16)
            slot += 1

        for which, scr in enumerate((sq_ref, sk_ref, sv_ref)):
            lo = SSM_WIDTH + which * ATTN_WIDTH
            res = full[:, lo:lo + ATTN_WIDTH]
            if which == 0:
                res = res * (HEAD_DIM ** -0.5)
            for j in range(ATTN_WIDTH // LANES):
                scr[j] = res[:, j * LANES:(j + 1) * LANES]
            for g, (_, d) in enumerate(ATTN_PATTERNS):
                out = qkv_refs[which * N_PATTERNS + g]
                for jj in range(PATTERN_WIDTH // LANES):
                    j = g * (PATTERN_WIDTH // LANES) + jj
                    for r, rows in enumerate(_rows_by_residue(scr, j, stage_ref, slot, n_rows, d)):
                        out[0, r, base // d:(base + n_rows) // d, jj * LANES:(jj + 1) * LANES] = rows.astype(BF16)
                    slot += d == SPLIT_STRIDE * SPLIT_STRIDE


def _proj(x, norm_g, w_pre, bsz, seqlen):
    n = x.shape[0]
    tiles_per_seq = seqlen // TM_PROJ
    rows = lambda w: pl.BlockSpec((TM_PROJ, w), lambda i: (i, 0))
    qkv_specs, qkv_shapes = [], []
    for _ in range(3):
        for _, d in ATTN_PATTERNS:
            qkv_specs.append(pl.BlockSpec((1, d, TM_PROJ // d, PATTERN_WIDTH),
                                          lambda i: (i // tiles_per_seq, 0, i % tiles_per_seq, 0)))
            qkv_shapes.append(jax.ShapeDtypeStruct((bsz, d, seqlen // d, PATTERN_WIDTH), BF16))
    half_rows = TM_PROJ // PROJ_HALVES
    slabs = lambda k: pltpu.VMEM((k, half_rows, LANES), F32)
    n_staged = SSM_N_OCT + 3 * PATTERN_WIDTH // LANES
    stage = pltpu.VMEM((n_staged, SPLIT_STRIDE, half_rows // SPLIT_STRIDE, LANES), F32)
    outs = pl.pallas_call(
        _proj_kernel,
        grid=(n // TM_PROJ,),
        in_specs=[rows(D_MODEL), _resident((1, D_MODEL)), _resident((D_MODEL, PRE_WIDTH))],
        out_specs=[pl.BlockSpec((TM_PROJ // SSM_CHUNK, SSM_ROW), lambda i: (i, 0))] + qkv_specs,
        out_shape=[jax.ShapeDtypeStruct((n // SSM_CHUNK, SSM_ROW), BF16)] + qkv_shapes,
        scratch_shapes=([slabs(SSM_N_OCT)] + [slabs(ATTN_WIDTH // LANES)] * 3 + [stage]) * PROJ_HALVES,
        compiler_params=_params(1),
        name="proj",
    )(x, norm_g.reshape(1, D_MODEL), w_pre)
    u = outs[0]
    q, k, v = (outs[1 + i * N_PATTERNS:1 + (i + 1) * N_PATTERNS] for i in range(3))
    return u, q, k, v


def _ssm_compact(a_re, a_im, log_dt, b_re, b_im, c_re, c_im, d_skip):
    t_len = SSM_CHUNK
    dt = jnp.exp(log_dt)[:, None]
    mag = jnp.exp(a_re * dt)
    ang = a_im * dt
    abar_re = mag * jnp.cos(ang)
    abar_im = mag * jnp.sin(ang)
    nr = abar_re - 1.0
    ni = abar_im
    den = a_re * a_re + a_im * a_im
    coef_re = (nr * a_re + ni * a_im) / den
    coef_im = (ni * a_re - nr * a_im) / den
    bbar_re = coef_re[..., None] * b_re - coef_im[..., None] * b_im
    bbar_im = coef_re[..., None] * b_im + coef_im[..., None] * b_re
    bt_re = jnp.swapaxes(bbar_re, -1, -2)
    bt_im = jnp.swapaxes(bbar_im, -1, -2)
    pw_re = [jnp.ones_like(abar_re)]
    pw_im = [jnp.zeros_like(abar_re)]
    for _ in range(t_len):
        pr, pi = pw_re[-1], pw_im[-1]
        pw_re.append(pr * abar_re - pi * abar_im)
        pw_im.append(pr * abar_im + pi * abar_re)
    rev_re = jnp.stack(pw_re[t_len - 1::-1])[:, :, None, :]
    rev_im = jnp.stack(pw_im[t_len - 1::-1])[:, :, None, :]
    pw_re = jnp.stack(pw_re)
    pw_im = jnp.stack(pw_im)
    n_oct, oct_, grp, st = SSM_N_OCT, SSM_OCT, SSM_GROUP, SSM_STATE

    def per_octet(re, im):
        both = jnp.concatenate([re, im], axis=-1)
        n = both.shape[0]
        return both.reshape(n, n_oct, oct_ * grp, 2 * st).transpose(1, 0, 2, 3).reshape(n_oct, n * LANES, 2 * st)

    into_c = per_octet(rev_re * bt_re[None] - rev_im * bt_im[None], rev_re * bt_im[None] + rev_im * bt_re[None])
    cp_re = c_re[None] * pw_re[:, :, None, :] - c_im[None] * pw_im[:, :, None, :]
    cp_im = c_re[None] * pw_im[:, :, None, :] + c_im[None] * pw_re[:, :, None, :]
    outof_c = per_octet(cp_re, -cp_im)
    decay = jnp.stack([pw_re[t_len].reshape(n_oct, SSM_OCT_STATE), pw_im[t_len].reshape(n_oct, SSM_OCT_STATE)]
                      + [jnp.zeros((n_oct, SSM_OCT_STATE), F32)] * 6, axis=1)
    skip = d_skip.reshape(n_oct, 1, LANES)
    return into_c.astype(BF16), outof_c.astype(BF16), decay, skip


def _ssm_expanders():
    row = np.arange(LANES)
    col = np.arange(2 * SSM_OCT_STATE)
    spread = (row[:, None] // SSM_STATE == col[None, :] // SSM_OCT_STATE) & (
        row[:, None] % SSM_STATE == col[None, :] % SSM_STATE)
    keep = row[:, None] // SSM_GROUP == (col[None, :] % SSM_OCT_STATE) // SSM_STATE
    return jnp.asarray(spread, BF16), jnp.asarray(keep, F32)


def _ssm_kernel(x_ref, intoc_ref, outofc_ref, decay_ref, skip_ref, spread_ref, keep_ref, y_ref,
                toep_ref, into_ref, outof_ref, loc_ref, prev_ref, *, n_chunks, n_seq):
    blk = lambda i: slice(i * LANES, (i + 1) * LANES)

    @pl.when(pl.program_id(1) == 0)
    def _build_octet_matrices():
        spread, keep = spread_ref[...], keep_ref[...]
        for s in range(SSM_CHUNK):
            into_ref[blk(s), :] = (_dot(intoc_ref[0, blk(s), :], spread) * keep).astype(BF16)
        for j in range(SSM_CHUNK + 1):
            outof_ref[blk(j), :] = (_dot(outofc_ref[0, blk(j), :], spread) * keep).astype(BF16)
        bbar = into_ref[blk(SSM_CHUNK - 1), :]
        diag = (lax.broadcasted_iota(jnp.int32, (LANES, LANES), 0)
                == lax.broadcasted_iota(jnp.int32, (LANES, LANES), 1))
        zeros = jnp.zeros((LANES, LANES), BF16)
        for j in range(SSM_CHUNK):
            lag = _dot_nt(bbar, outof_ref[blk(j), :])
            if j == 0:
                lag = lag + jnp.where(diag, skip_ref[0], 0.0)
            lag = lag.astype(BF16)
            for s in range(SSM_CHUNK - j):
                toep_ref[blk(s), blk(s + j)] = lag
            if j > 0:
                for s in range(j, SSM_CHUNK):
                    toep_ref[blk(s), blk(s - j)] = zeros

    loc_ref[...] = _dot(x_ref[...], into_ref[...])
    ar = decay_ref[0, 0:1, :]
    ai = decay_ref[0, 1:2, :]

    def step(c, carry):
        new = []
        for q in range(n_seq):
            s_re, s_im = carry[2 * q], carry[2 * q + 1]
            row = pl.ds(q * n_chunks + c, 1)
            prev_ref[row, :SSM_OCT_STATE] = s_re
            prev_ref[row, SSM_OCT_STATE:] = s_im
            new.append(ar * s_re - ai * s_im + loc_ref[row, :SSM_OCT_STATE])
            new.append(ar * s_im + ai * s_re + loc_ref[row, SSM_OCT_STATE:])
        return tuple(new)

    zero = jnp.zeros((1, SSM_OCT_STATE), F32)
    lax.fori_loop(0, n_chunks, step, (zero,) * (2 * n_seq), unroll=True)
    prev = prev_ref[...].astype(BF16)
    for j in range(SSM_OCT_IN // MXU_TILE):
        lo, hi = j * MXU_TILE, (j + 1) * MXU_TILE
        y = _dot(x_ref[:, :hi], toep_ref[:hi, lo:hi]) + _dot_nt(prev, outof_ref[lo + LANES:hi + LANES, :])
        y_ref[:, lo:hi] = jax.nn.gelu(y).astype(BF16)


def _ssm(u_rows, compact, bsz):
    into_c, outof_c, decay, skip = compact
    spread, keep = _ssm_expanders()
    n_chunks = u_rows.shape[0] // bsz
    n_seq = SSM_SEQ_PER_STEP
    per_oct = lambda *shape: pl.BlockSpec((1,) + shape, lambda o, b: (o,) + (0,) * len(shape))
    rows = pl.BlockSpec((n_seq * n_chunks, SSM_OCT_IN), lambda o, b: (b, o))
    wide = 2 * SSM_OCT_STATE
    return pl.pallas_call(
        functools.partial(_ssm_kernel, n_chunks=n_chunks, n_seq=n_seq),
        grid=(SSM_N_OCT, bsz // n_seq),
        in_specs=[rows, per_oct(SSM_OCT_IN, LANES), per_oct(SSM_OCT_IN + LANES, LANES),
                  per_oct(8, SSM_OCT_STATE), per_oct(1, LANES), _resident((LANES, wide)), _resident((LANES, wide))],
        out_specs=rows,
        out_shape=jax.ShapeDtypeStruct(u_rows.shape, BF16),
        scratch_shapes=[pltpu.VMEM((SSM_OCT_IN, SSM_OCT_IN), BF16),
                        pltpu.VMEM((SSM_OCT_IN, wide), BF16),
                        pltpu.VMEM((SSM_OCT_IN + LANES, wide), BF16),
                        pltpu.VMEM((n_seq * n_chunks, wide), F32),
                        pltpu.VMEM((n_seq * n_chunks, wide), F32)],
        compiler_params=_params(2),
        name="ssm",
    )(u_rows, into_c, outof_c, decay, skip, spread, keep)


def _t5_buckets(dist):
    dist = np.asarray(dist, np.int32)
    max_exact = REL_BUCKETS // 2
    safe = np.maximum(dist, 1).astype(np.float32)
    large = max_exact + (np.log(safe / max_exact) / np.log(REL_MAX_DIST / max_exact)
                         * (REL_BUCKETS - max_exact)).astype(np.int32)
    large = np.minimum(large, REL_BUCKETS - 1)
    return np.where(dist < max_exact, dist, large).astype(np.int32)


def _band_bias(rel_table, pattern):
    dilation = ATTN_PATTERNS[pattern][1]
    buckets = _t5_buckets(np.arange(WINDOW_KEYS + 1) * dilation)
    table = rel_table[:, pattern * HEADS_PER_PATTERN:(pattern + 1) * HEADS_PER_PATTERN].T
    runs = []
    for b in np.unique(buckets):
        runs.append(jnp.broadcast_to(table[:, b:b + 1], (HEADS_PER_PATTERN, int(np.sum(buckets == b)))))
    per_step = jnp.concatenate(runs, axis=1)
    period = 3 * SUB_Q
    masked = lambda n: jnp.full((HEADS_PER_PATTERN, n), NEG_BIG, F32)
    w = jnp.concatenate([masked(SUB_Q - 1), per_step[:, ::-1], masked(period - SUB_Q - WINDOW_KEYS)], axis=1)
    skew = jnp.tile(w, (1, SUB_Q))[:, :SUB_Q * (period - 1)].reshape(HEADS_PER_PATTERN, SUB_Q, period - 1)
    bias = skew[:, :, SUB_Q - 1:3 * SUB_Q - 1]
    first = jnp.where((np.arange(2 * SUB_Q) >= SUB_Q)[None, None, :], bias, NEG_BIG)
    pairs = lambda b: b.reshape(HEADS_PER_PATTERN // 2, 2 * SUB_Q, 2 * SUB_Q)
    return jnp.stack([pairs(bias), pairs(first)])


def _attn_kernel(q_ref, kc_ref, kh_ref, vc_ref, vh_ref, bias_ref, o_ref):
    n_res, tq = q_ref.shape[1], q_ref.shape[2]
    first = jnp.where(pl.program_id(2) == 0, 1, 0)
    lane = lax.broadcasted_iota(jnp.int32, (SUB_Q, 2 * HEAD_DIM), 1)
    low_half = lane < HEAD_DIM
    for r in range(n_res):
        for j in range(tq // SUB_Q):
            rows = slice(j * SUB_Q, (j + 1) * SUB_Q)
            q = q_ref[0, r, rows, :]
            if j == 0:
                k = jnp.concatenate([kh_ref[0, r], kc_ref[0, r, rows, :]], axis=0)
                v = jnp.concatenate([vh_ref[0, r], vc_ref[0, r, rows, :]], axis=0)
            else:
                k = kc_ref[0, r, (j - 1) * SUB_Q:(j + 1) * SUB_Q, :]
                v = vc_ref[0, r, (j - 1) * SUB_Q:(j + 1) * SUB_Q, :]
            for pair in range(HEADS_PER_PATTERN // 2):
                cols = slice(pair * 2 * HEAD_DIM, (pair + 1) * 2 * HEAD_DIM)
                q2, k2, v2 = q[:, cols], k[:, cols], v[:, cols]
                zero = jnp.zeros_like(q2)
                qq = jnp.concatenate([jnp.where(low_half, q2, zero), jnp.where(low_half, zero, q2)], axis=0)
                s = _dot_nt(qq, k2) + (bias_ref[first, pair] if j == 0 else bias_ref[0, pair])
                m = jnp.max(s, axis=-1, keepdims=True)
                p = jnp.exp(s - m)
                l = jnp.sum(p, axis=-1, keepdims=True)
                o = _dot(p.astype(BF16), v2) / l
                lse = jnp.broadcast_to(m + jnp.log(l), (2 * SUB_Q, 2 * HEAD_DIM))
                o_ref[0, r, rows, cols] = jnp.where(low_half, o[:SUB_Q], o[SUB_Q:])
                o_ref[0, r, rows, PATTERN_WIDTH + pair * 2 * HEAD_DIM:PATTERN_WIDTH + (pair + 1) * 2 * HEAD_DIM] = (
                    jnp.where(low_half, lse[:SUB_Q], lse[SUB_Q:]))


def _dilated_attention(q, k, v, rel_table, pattern):
    bsz, d, res_len, _ = q.shape
    tq = min(res_len, ATTN_STEP_ROWS)
    n_res = ATTN_STEP_ROWS // tq
    cur = pl.BlockSpec((1, n_res, tq, PATTERN_WIDTH), lambda b, r, i: (b, r, i, 0))
    halo = pl.BlockSpec((1, n_res, SUB_Q, PATTERN_WIDTH),
                        lambda b, r, i: (b, r, jnp.maximum(i * (tq // SUB_Q) - 1, 0), 0))
    return pl.pallas_call(
        _attn_kernel,
        grid=(bsz, d // n_res, res_len // tq),
        in_specs=[cur, cur, halo, cur, halo, _resident((2, HEADS_PER_PATTERN // 2, 2 * SUB_Q, 2 * SUB_Q))],
        out_specs=pl.BlockSpec((1, n_res, tq, 2 * PATTERN_WIDTH), lambda b, r, i: (b, r, i, 0)),
        out_shape=jax.ShapeDtypeStruct((bsz, d, res_len, 2 * PATTERN_WIDTH), F32),
        compiler_params=_params(3),
        name=f"attn_d{d}",
    )(q, k, k, v, v, _band_bias(rel_table, pattern))


def _memkv_kernel(mem_ref, g_ref, w_ref, k_ref, v_ref):
    mn = _rmsnorm(mem_ref[...], g_ref[...]).astype(BF16)
    k_ref[...] = _dot(mn, w_ref[:, :XATTN_WIDTH]).astype(BF16)
    v_ref[...] = _dot(mn, w_ref[:, XATTN_WIDTH:]).astype(BF16)


def _memkv(mem2d, norm_g, w_kv):
    rows = mem2d.shape[0]
    full = lambda *shape: pl.BlockSpec(shape, lambda i: (0,) * len(shape))
    return pl.pallas_call(
        _memkv_kernel,
        grid=(1,),
        in_specs=[full(rows, D_MODEL), full(1, D_MODEL), full(D_MODEL, 2 * XATTN_WIDTH)],
        out_specs=[full(rows, XATTN_WIDTH), full(rows, XATTN_WIDTH)],
        out_shape=[jax.ShapeDtypeStruct((rows, XATTN_WIDTH), BF16)] * 2,
        compiler_params=_params(1),
        name="memkv",
    )(mem2d, norm_g.reshape(1, D_MODEL), w_kv)


def _merge_kernel(x_ref, y_ref, o0_ref, o1_ref, o2_ref, mk_ref, mv_ref, g_ref, wpost_ref, wglu_ref,
                  wau_ref, wxu_ref, wout_ref, out_ref, ys_ref, s1_ref, s2_ref):
    x = x_ref[...]
    un = _rmsnorm(x, g_ref[...]).astype(BF16)

    def gate(index):
        lo = XATTN_WIDTH + index * D_MODEL
        return _sigmoid(_dot(un, wpost_ref[:, lo:lo + D_MODEL]))

    slabs = 2 * PATTERN_WIDTH // LANES
    for o_ref, s_ref in ((o1_ref, s1_ref), (o2_ref, s2_ref)):
        d = o_ref.shape[1]
        for r in range(d):
            for j in range(slabs):
                s_ref[j, pl.ds(r, TM_MERGE // d, stride=d), :] = o_ref[0, r, :, j * LANES:(j + 1) * LANES]
    half = slabs // 2
    nat = lambda s_ref, lo: jnp.concatenate([s_ref[j] for j in range(lo, lo + half)], axis=-1)
    outs = [o0_ref[0, 0, :, :PATTERN_WIDTH], nat(s1_ref, 0), nat(s2_ref, 0)]
    lses = [o0_ref[0, 0, :, PATTERN_WIDTH:], nat(s1_ref, half), nat(s2_ref, half)]
    top = jnp.maximum(jnp.maximum(lses[0], lses[1]), lses[2])
    ws = [jnp.exp(l - top) for l in lses]
    att = (ws[0] * outs[0] + ws[1] * outs[1] + ws[2] * outs[2]) / (ws[0] + ws[1] + ws[2])
    att = att.astype(BF16)

    chunk_rows = TM_MERGE // SSM_CHUNK
    for o in range(SSM_N_OCT):
        for t in range(SSM_CHUNK):
            lo = o * SSM_OCT_IN + t * LANES
            ys_ref[o, pl.ds(t, chunk_rows, stride=SSM_CHUNK), :] = y_ref[:, lo:lo + LANES].astype(F32)
    ys = jnp.concatenate([ys_ref[o] for o in range(SSM_N_OCT)], axis=-1).astype(BF16)
    glu = _dot(ys, wglu_ref[:, :D_MODEL]) * _sigmoid(_dot(ys, wglu_ref[:, D_MODEL:]))
    merged = gate(0) * glu

    xq = _dot(un, wpost_ref[:, :XATTN_WIDTH]).astype(BF16)
    heads = []
    for h in range(XATTN_HEADS):
        cols = slice(h * XATTN_HEAD_DIM, (h + 1) * XATTN_HEAD_DIM)
        s = _dot_nt(xq[:, cols], mk_ref[0, :, cols]) * (XATTN_HEAD_DIM ** -0.5)
        p = jnp.exp(s - jnp.max(s, axis=-1, keepdims=True))
        heads.append(_dot(p.astype(BF16), mv_ref[0, :, cols]) / jnp.sum(p, axis=-1, keepdims=True))
    xo = jnp.concatenate(heads, axis=-1).astype(BF16)
    merged = merged + gate(2) * _dot(xo, wxu_ref[...])

    merged = merged + gate(1) * _dot(att, wau_ref[...])

    out_ref[...] = x + _dot(merged.astype(BF16), wout_ref[...])


def _merge(x, y_rows, attn_outs, mk, mv, norm_g, w_post, w_glu, w_au, w_xu, w_out, seqlen):
    n = x.shape[0]
    tiles_per_seq = seqlen // TM_MERGE
    rows = lambda w: pl.BlockSpec((TM_MERGE, w), lambda i: (i, 0))
    mem = pl.BlockSpec((1, MEM_LEN, XATTN_WIDTH), lambda i: (i // tiles_per_seq, 0, 0))
    attn_specs = [pl.BlockSpec((1, d, TM_MERGE // d, 2 * PATTERN_WIDTH),
                               lambda i: (i // tiles_per_seq, 0, i % tiles_per_seq, 0))
                  for _, d in ATTN_PATTERNS]
    slabs = lambda k: pltpu.VMEM((k, TM_MERGE, LANES), F32)
    return pl.pallas_call(
        _merge_kernel,
        grid=(n // TM_MERGE,),
        in_specs=[rows(D_MODEL), pl.BlockSpec((TM_MERGE // SSM_CHUNK, SSM_ROW), lambda i: (i, 0))] + attn_specs
                 + [mem, mem, _resident((1, D_MODEL)), _resident((D_MODEL, POST_WIDTH)),
                    _resident((SSM_WIDTH, 2 * D_MODEL)), _resident((PATTERN_WIDTH, D_MODEL)),
                    _resident((XATTN_WIDTH, D_MODEL)), _resident((D_MODEL, D_MODEL))],
        out_specs=rows(D_MODEL),
        out_shape=jax.ShapeDtypeStruct((n, D_MODEL), F32),
        scratch_shapes=[slabs(SSM_N_OCT), slabs(2 * PATTERN_WIDTH // LANES), slabs(2 * PATTERN_WIDTH // LANES)],
        compiler_params=_params(1),
        name="merge",
    )(x, y_rows, *attn_outs, mk, mv, norm_g.reshape(1, D_MODEL), w_post, w_glu, w_au, w_xu, w_out)


def _layer(x, mem, rel_table, ffn1_norm, ffn1_w_in, ffn1_w_down, mix_norm, w_in,
           ssm_a_re, ssm_a_im, ssm_log_dt, ssm_b_re, ssm_b_im, ssm_c_re, ssm_c_im, ssm_d, ssm_w_glu,
           attn_w_up, mem_norm, xattn_w_kv, xattn_w_up, w_out, ffn2_norm, ffn2_w_in, ffn2_w_down,
           final_norm, bsz, seqlen):
    cast = [(w_in, (PRE_WIDTH, POST_WIDTH)), (ssm_w_glu, (2 * D_MODEL,)), (attn_w_up, (D_MODEL,)),
            (xattn_w_kv, (2 * XATTN_WIDTH,)), (xattn_w_up, (D_MODEL,)), (w_out, (D_MODEL,)),
            (ffn2_w_in, (2 * D_FF,)), (ffn2_w_down, (D_MODEL,))]
    x, (w_pre, w_post, w_glu, w_au, w_kv, w_xu, w_o, w2_in, w2_down) = _ffn(
        x, ffn1_norm, ffn1_w_in.astype(BF16), ffn1_w_down.astype(BF16), cast=cast)
    u_rows, q, k, v = _proj(x, mix_norm, w_pre, bsz, seqlen)
    compact = _ssm_compact(ssm_a_re, ssm_a_im, ssm_log_dt, ssm_b_re, ssm_b_im, ssm_c_re, ssm_c_im, ssm_d)
    y_rows = _ssm(u_rows, compact, bsz)
    attn_outs = [_dilated_attention(q[g], k[g], v[g], rel_table, g) for g in range(N_PATTERNS)]
    mk, mv = _memkv(mem.reshape(bsz * MEM_LEN, D_MODEL), mem_norm, w_kv)
    mk = mk.reshape(bsz, MEM_LEN, XATTN_WIDTH)
    mv = mv.reshape(bsz, MEM_LEN, XATTN_WIDTH)
    x = _merge(x, y_rows, attn_outs, mk, mv, mix_norm, w_post, w_glu, w_au, w_xu, w_o, seqlen)
    return _ffn(x, ffn2_norm, w2_in, w2_down, final_g=final_norm)[0]


def kernel(x, mem, ffn1_norm, ffn1_w_in, ffn1_w_down, mix_norm, w_in, ssm_a_re, ssm_a_im, ssm_log_dt,
           ssm_b_re, ssm_b_im, ssm_c_re, ssm_c_im, ssm_d, ssm_w_glu, rel_table, attn_w_up, mem_norm,
           xattn_w_kv, xattn_w_up, w_out, ffn2_norm, ffn2_w_in, ffn2_w_down, final_norm):
    bsz, seqlen, _ = x.shape
    assert ffn1_norm.shape[0] == 1, "single-layer trunk"
    assert seqlen % ATTN_STEP_ROWS == 0 and (seqlen // ATTN_PATTERNS[-1][1]) % SUB_Q == 0
    assert seqlen % TM_PROJ == 0 and seqlen % TM_MERGE == 0
    h = _layer(x.reshape(bsz * seqlen, D_MODEL), mem, rel_table, ffn1_norm[0], ffn1_w_in[0], ffn1_w_down[0],
               mix_norm[0], w_in[0], ssm_a_re[0], ssm_a_im[0], ssm_log_dt[0], ssm_b_re[0], ssm_b_im[0],
               ssm_c_re[0], ssm_c_im[0], ssm_d[0], ssm_w_glu[0], attn_w_up[0], mem_norm[0], xattn_w_kv[0],
               xattn_w_up[0], w_out[0], ffn2_norm[0], ffn2_w_in[0], ffn2_w_down[0], final_norm,
               bsz, seqlen)
    return h.reshape(bsz, seqlen, D_MODEL)
```

```python
import functools

import numpy as np
import jax
import jax.numpy as jnp
from jax import lax
from jax.experimental import pallas as pl
from jax.experimental.pallas import tpu as pltpu

F32 = jnp.float32
BF16 = jnp.bfloat16

LANES = 128
BF16_SUBLANES = 16
MXU_TILE = 256

D_MODEL = 1024
D_FF = 2816
EPS = 1e-6

HEAD_DIM = 64
ATTN_PATTERNS = ((128, 1), (512, 4), (2048, 16))
N_PATTERNS = 3
HEADS_PER_PATTERN = 4
N_ATTN_HEADS = N_PATTERNS * HEADS_PER_PATTERN
ATTN_WIDTH = N_ATTN_HEADS * HEAD_DIM
PATTERN_WIDTH = HEADS_PER_PATTERN * HEAD_DIM
WINDOW_KEYS = 128
REL_BUCKETS = 32
REL_MAX_DIST = 2048

SSM_GROUP = 16
SSM_STATE = 64
SSM_WIDTH = 512
SSM_GROUPS = SSM_WIDTH // SSM_GROUP
SSM_CHUNK = 16
SSM_OCT = LANES // SSM_GROUP
SSM_N_OCT = SSM_WIDTH // LANES
SSM_OCT_IN = SSM_CHUNK * LANES
SSM_OCT_STATE = SSM_OCT * SSM_STATE
SSM_ROW = SSM_CHUNK * SSM_WIDTH

MEM_LEN = 256
XATTN_HEADS = 4
XATTN_HEAD_DIM = 128
XATTN_WIDTH = XATTN_HEADS * XATTN_HEAD_DIM

PRE_WIDTH = SSM_WIDTH + 3 * ATTN_WIDTH
POST_WIDTH = XATTN_WIDTH + 3 * D_MODEL

NEG_BIG = -1e30

VMEM_LIMIT_BYTES = 56 * 1024 * 1024

TM_FFN = 1024
FFN_HALVES = 2
FFN_CHUNKS = ((0, 1536), (1536, D_FF))
TM_PROJ = 1024
PROJ_HALVES = 2
SPLIT_STRIDE = 4
TM_MERGE = 512
ATTN_STEP_ROWS = 4096
SSM_SEQ_PER_STEP = 2
SUB_Q = 128


def _rmsnorm(x, g):
    return x * lax.rsqrt(jnp.mean(x * x, axis=-1, keepdims=True) + EPS) * g


def _sigmoid(x):
    return 0.5 * jnp.tanh(0.5 * x) + 0.5


def _dot(a, b):
    return jnp.dot(a, b, preferred_element_type=F32)


def _dot_nt(a, b):
    return lax.dot_general(a, b, (((1,), (1,)), ((), ())), preferred_element_type=F32)


def _resident(shape):
    nd = len(shape)
    return pl.BlockSpec(shape, lambda *_: (0,) * nd, pipeline_mode=pl.Buffered(1))


def _params(n_axes):
    return pltpu.CompilerParams(dimension_semantics=("arbitrary",) * n_axes,
                                vmem_limit_bytes=VMEM_LIMIT_BYTES)


def _ffn_kernel(x_ref, g_ref, win_ref, wdn_ref, *rest, final_norm, n_cast):
    rest = list(rest)
    fg_ref = rest.pop(0) if final_norm else None
    cast_src, o_ref, cast_dst = rest[:n_cast], rest[n_cast], rest[n_cast + 1:]
    for half in range(FFN_HALVES):
        rows = slice(half * TM_FFN // FFN_HALVES, (half + 1) * TM_FFN // FFN_HALVES)
        x = x_ref[rows, :]
        un = _rmsnorm(x, g_ref[...]).astype(BF16)
        acc = None
        for lo, hi in FFN_CHUNKS:
            a = _dot(un, win_ref[:, lo:hi])
            b = _dot(un, win_ref[:, D_FF + lo:D_FF + hi])
            h = (a * _sigmoid(a) * b).astype(BF16)
            part = _dot(h, wdn_ref[lo:hi, :])
            acc = part if acc is None else acc + part
        y = x + 0.5 * acc
        if final_norm:
            y = _rmsnorm(y, fg_ref[...])
        o_ref[rows, :] = y

    dst = iter(cast_dst)
    for src in cast_src:
        lo = 0
        while lo < src.shape[1]:
            out = next(dst)
            out[...] = src[:, lo:lo + out.shape[1]].astype(BF16)
            lo += out.shape[1]


def _cast_rows(n_rows, n_steps):
    rows = BF16_SUBLANES
    while n_rows % rows or n_rows // rows > n_steps:
        rows += BF16_SUBLANES
    return rows


def _ffn(x, norm_g, w_in, w_down, final_g=None, cast=()):
    n = x.shape[0]
    n_steps = n // TM_FFN
    row = pl.BlockSpec((TM_FFN, D_MODEL), lambda i: (i, 0))
    in_specs = [row, _resident((1, D_MODEL)), _resident((D_MODEL, 2 * D_FF)), _resident((D_FF, D_MODEL))]
    args = [x, norm_g.reshape(1, D_MODEL), w_in, w_down]
    if final_g is not None:
        in_specs.append(_resident((1, D_MODEL)))
        args.append(final_g.reshape(1, D_MODEL))
    out_specs, out_shapes = [row], [jax.ShapeDtypeStruct((n, D_MODEL), F32)]
    for w, widths in cast:
        assert sum(widths) == w.shape[1]
        rows = _cast_rows(w.shape[0], n_steps)
        last = w.shape[0] // rows - 1
        block = lambda width, last=last, rows=rows: pl.BlockSpec((rows, width), lambda i: (jnp.minimum(i, last), 0))
        in_specs.append(block(w.shape[1]))
        args.append(w)
        out_specs += [block(width) for width in widths]
        out_shapes += [jax.ShapeDtypeStruct((w.shape[0], width), BF16) for width in widths]
    outs = pl.pallas_call(
        functools.partial(_ffn_kernel, final_norm=final_g is not None, n_cast=len(cast)),
        grid=(n_steps,),
        in_specs=in_specs,
        out_specs=out_specs,
        out_shape=out_shapes,
        compiler_params=_params(1),
        name="ffn_final" if final_g is not None else "ffn",
    )(*args)
    return outs[0], outs[1:]


def _rows_by_residue(slab_ref, j, stage_ref, slot, n_rows, d):
    if d == 1:
        return [slab_ref[j]]
    if d == SPLIT_STRIDE:
        return [slab_ref[j, pl.ds(r, n_rows // d, stride=d), :] for r in range(d)]
    assert d == SPLIT_STRIDE * SPLIT_STRIDE
    for low in range(SPLIT_STRIDE):
        stage_ref[slot, low] = slab_ref[j, pl.ds(low, n_rows // SPLIT_STRIDE, stride=SPLIT_STRIDE), :]
    return [stage_ref[slot, r % SPLIT_STRIDE, pl.ds(r // SPLIT_STRIDE, n_rows // d, stride=SPLIT_STRIDE), :]
            for r in range(d)]


def _proj_kernel(x_ref, g_ref, w_ref, u_ref, *rest):
    qkv_refs, slabs = rest[:3 * N_PATTERNS], rest[3 * N_PATTERNS:]
    n_rows = TM_PROJ // PROJ_HALVES
    per_half = len(slabs) // PROJ_HALVES
    for half in range(PROJ_HALVES):
        base = half * n_rows
        su_ref, sq_ref, sk_ref, sv_ref, stage_ref = slabs[per_half * half:per_half * (half + 1)]
        un = _rmsnorm(x_ref[base:base + n_rows, :], g_ref[...]).astype(BF16)
        full = _dot(un, w_ref[...])
        slot = 0

        u = full[:, :SSM_WIDTH]
        chunks = slice(base // SSM_CHUNK, (base + n_rows) // SSM_CHUNK)
        for o in range(SSM_N_OCT):
            su_ref[o] = u[:, o * LANES:(o + 1) * LANES]
            for t, rows in enumerate(_rows_by_residue(su_ref, o, stage_ref, slot, n_rows, SSM_CHUNK)):
                lo = o * SSM_OCT_IN + t * LANES
                u_ref[chunks, lo:lo + LANES] = rows.astype(BF16)
            slot += 1

        for which, scr in enumerate((sq_ref, sk_ref, sv_ref)):
            lo = SSM_WIDTH + which * ATTN_WIDTH
            res = full[:, lo:lo + ATTN_WIDTH]
            if which == 0:
                res = res * (HEAD_DIM ** -0.5)
            for j in range(ATTN_WIDTH // LANES):
                scr[j] = res[:, j * LANES:(j + 1) * LANES]
            for g, (_, d) in enumerate(ATTN_PATTERNS):
                out = qkv_refs[which * N_PATTERNS + g]
                for jj in range(PATTERN_WIDTH // LANES):
                    j = g * (PATTERN_WIDTH // LANES) + jj
                    for r, rows in enumerate(_rows_by_residue(scr, j, stage_ref, slot, n_rows, d)):
                        out[0, r, base // d:(base + n_rows) // d, jj * LANES:(jj + 1) * LANES] = rows.astype(BF16)
                    slot += d == SPLIT_STRIDE * SPLIT_STRIDE


def _proj(x, norm_g, w_pre, bsz, seqlen):
    n = x.shape[0]
    tiles_per_seq = seqlen // TM_PROJ
    rows = lambda w: pl.BlockSpec((TM_PROJ, w), lambda i: (i, 0))
    qkv_specs, qkv_shapes = [], []
    for _ in range(3):
        for _, d in ATTN_PATTERNS:
            qkv_specs.append(pl.BlockSpec((1, d, TM_PROJ // d, PATTERN_WIDTH),
                                          lambda i: (i // tiles_per_seq, 0, i % tiles_per_seq, 0)))
            qkv_shapes.append(jax.ShapeDtypeStruct((bsz, d, seqlen // d, PATTERN_WIDTH), BF16))
    half_rows = TM_PROJ // PROJ_HALVES
    slabs = lambda k: pltpu.VMEM((k, half_rows, LANES), F32)
    n_staged = SSM_N_OCT + 3 * PATTERN_WIDTH // LANES
    stage = pltpu.VMEM((n_staged, SPLIT_STRIDE, half_rows // SPLIT_STRIDE, LANES), F32)
    outs = pl.pallas_call(
        _proj_kernel,
        grid=(n // TM_PROJ,),
        in_specs=[rows(D_MODEL), _resident((1, D_MODEL)), _resident((D_MODEL, PRE_WIDTH))],
        out_specs=[pl.BlockSpec((TM_PROJ // SSM_CHUNK, SSM_ROW), lambda i: (i, 0))] + qkv_specs,
        out_shape=[jax.ShapeDtypeStruct((n // SSM_CHUNK, SSM_ROW), BF16)] + qkv_shapes,
        scratch_shapes=([slabs(SSM_N_OCT)] + [slabs(ATTN_WIDTH // LANES)] * 3 + [stage]) * PROJ_HALVES,
        compiler_params=_params(1),
        name="proj",
    )(x, norm_g.reshape(1, D_MODEL), w_pre)
    u = outs[0]
    q, k, v = (outs[1 + i * N_PATTERNS:1 + (i + 1) * N_PATTERNS] for i in range(3))
    return u, q, k, v


def _ssm_compact(a_re, a_im, log_dt, b_re, b_im, c_re, c_im, d_skip):
    t_len = SSM_CHUNK
    dt = jnp.exp(log_dt)[:, None]
    mag = jnp.exp(a_re * dt)
    ang = a_im * dt
    abar_re = mag * jnp.cos(ang)
    abar_im = mag * jnp.sin(ang)
    nr = abar_re - 1.0
    ni = abar_im
    den = a_re * a_re + a_im * a_im
    coef_re = (nr * a_re + ni * a_im) / den
    coef_im = (ni * a_re - nr * a_im) / den
    bbar_re = coef_re[..., None] * b_re - coef_im[..., None] * b_im
    bbar_im = coef_re[..., None] * b_im + coef_im[..., None] * b_re
    bt_re = jnp.swapaxes(bbar_re, -1, -2)
    bt_im = jnp.swapaxes(bbar_im, -1, -2)
    pw_re = [jnp.ones_like(abar_re)]
    pw_im = [jnp.zeros_like(abar_re)]
    for _ in range(t_len):
        pr, pi = pw_re[-1], pw_im[-1]
        pw_re.append(pr * abar_re - pi * abar_im)
        pw_im.append(pr * abar_im + pi * abar_re)
    rev_re = jnp.stack(pw_re[t_len - 1::-1])[:, :, None, :]
    rev_im = jnp.stack(pw_im[t_len - 1::-1])[:, :, None, :]
    pw_re = jnp.stack(pw_re)
    pw_im = jnp.stack(pw_im)
    n_oct, oct_, grp, st = SSM_N_OCT, SSM_OCT, SSM_GROUP, SSM_STATE

    def per_octet(re, im):
        both = jnp.concatenate([re, im], axis=-1)
        n = both.shape[0]
        return both.reshape(n, n_oct, oct_ * grp, 2 * st).transpose(1, 0, 2, 3).reshape(n_oct, n * LANES, 2 * st)

    into_c = per_octet(rev_re * bt_re[None] - rev_im * bt_im[None], rev_re * bt_im[None] + rev_im * bt_re[None])
    cp_re = c_re[None] * pw_re[:, :, None, :] - c_im[None] * pw_im[:, :, None, :]
    cp_im = c_re[None] * pw_im[:, :, None, :] + c_im[None] * pw_re[:, :, None, :]
    outof_c = per_octet(cp_re, -cp_im)
    decay = jnp.stack([pw_re[t_len].reshape(n_oct, SSM_OCT_STATE), pw_im[t_len].reshape(n_oct, SSM_OCT_STATE)]
                      + [jnp.zeros((n_oct, SSM_OCT_STATE), F32)] * 6, axis=1)
    skip = d_skip.reshape(n_oct, 1, LANES)
    return into_c.astype(BF16), outof_c.astype(BF16), decay, skip


def _ssm_expanders():
    row = np.arange(LANES)
    col = np.arange(2 * SSM_OCT_STATE)
    spread = (row[:, None] // SSM_STATE == col[None, :] // SSM_OCT_STATE) & (
        row[:, None] % SSM_STATE == col[None, :] % SSM_STATE)
    keep = row[:, None] // SSM_GROUP == (col[None, :] % SSM_OCT_STATE) // SSM_STATE
    return jnp.asarray(spread, BF16), jnp.asarray(keep, F32)


def _ssm_kernel(x_ref, intoc_ref, outofc_ref, decay_ref, skip_ref, spread_ref, keep_ref, y_ref,
                toep_ref, into_ref, outof_ref, loc_ref, prev_ref, *, n_chunks, n_seq):
    blk = lambda i: slice(i * LANES, (i + 1) * LANES)

    @pl.when(pl.program_id(1) == 0)
    def _build_octet_matrices():
        spread, keep = spread_ref[...], keep_ref[...]
        for s in range(SSM_CHUNK):
            into_ref[blk(s), :] = (_dot(intoc_ref[0, blk(s), :], spread) * keep).astype(BF16)
        for j in range(SSM_CHUNK + 1):
            outof_ref[blk(j), :] = (_dot(outofc_ref[0, blk(j), :], spread) * keep).astype(BF16)
        bbar = into_ref[blk(SSM_CHUNK - 1), :]
        diag = (lax.broadcasted_iota(jnp.int32, (LANES, LANES), 0)
                == lax.broadcasted_iota(jnp.int32, (LANES, LANES), 1))
        zeros = jnp.zeros((LANES, LANES), BF16)
        for j in range(SSM_CHUNK):
            lag = _dot_nt(bbar, outof_ref[blk(j), :])
            if j == 0:
                lag = lag + jnp.where(diag, skip_ref[0], 0.0)
            lag = lag.astype(BF16)
            for s in range(SSM_CHUNK - j):
                toep_ref[blk(s), blk(s + j)] = lag
            if j > 0:
                for s in range(j, SSM_CHUNK):
                    toep_ref[blk(s), blk(s - j)] = zeros

    loc_ref[...] = _dot(x_ref[...], into_ref[...])
    ar = decay_ref[0, 0:1, :]
    ai = decay_ref[0, 1:2, :]

    def step(c, carry):
        new = []
        for q in range(n_seq):
            s_re, s_im = carry[2 * q], carry[2 * q + 1]
            row = pl.ds(q * n_chunks + c, 1)
            prev_ref[row, :SSM_OCT_STATE] = s_re
            prev_ref[row, SSM_OCT_STATE:] = s_im
            new.append(ar * s_re - ai * s_im + loc_ref[row, :SSM_OCT_STATE])
            new.append(ar * s_im + ai * s_re + loc_ref[row, SSM_OCT_STATE:])
        return tuple(new)

    zero = jnp.zeros((1, SSM_OCT_STATE), F32)
    lax.fori_loop(0, n_chunks, step, (zero,) * (2 * n_seq), unroll=True)
    prev = prev_ref[...].astype(BF16)
    for j in range(SSM_OCT_IN // MXU_TILE):
        lo, hi = j * MXU_TILE, (j + 1) * MXU_TILE
        y = _dot(x_ref[:, :hi], toep_ref[:hi, lo:hi]) + _dot_nt(prev, outof_ref[lo + LANES:hi + LANES, :])
        y_ref[:, lo:hi] = jax.nn.gelu(y).astype(BF16)


def _ssm(u_rows, compact, bsz):
    into_c, outof_c, decay, skip = compact
    spread, keep = _ssm_expanders()
    n_chunks = u_rows.shape[0] // bsz
    n_seq = SSM_SEQ_PER_STEP
    per_oct = lambda *shape: pl.BlockSpec((1,) + shape, lambda o, b: (o,) + (0,) * len(shape))
    rows = pl.BlockSpec((n_seq * n_chunks, SSM_OCT_IN), lambda o, b: (b, o))
    wide = 2 * SSM_OCT_STATE
    return pl.pallas_call(
        functools.partial(_ssm_kernel, n_chunks=n_chunks, n_seq=n_seq),
        grid=(SSM_N_OCT, bsz // n_seq),
        in_specs=[rows, per_oct(SSM_OCT_IN, LANES), per_oct(SSM_OCT_IN + LANES, LANES),
                  per_oct(8, SSM_OCT_STATE), per_oct(1, LANES), _resident((LANES, wide)), _resident((LANES, wide))],
        out_specs=rows,
        out_shape=jax.ShapeDtypeStruct(u_rows.shape, BF16),
        scratch_shapes=[pltpu.VMEM((SSM_OCT_IN, SSM_OCT_IN), BF16),
                        pltpu.VMEM((SSM_OCT_IN, wide), BF16),
                        pltpu.VMEM((SSM_OCT_IN + LANES, wide), BF16),
                        pltpu.VMEM((n_seq * n_chunks, wide), F32),
                        pltpu.VMEM((n_seq * n_chunks, wide), F32)],
        compiler_params=_params(2),
        name="ssm",
    )(u_rows, into_c, outof_c, decay, skip, spread, keep)


def _t5_buckets(dist):
    dist = np.asarray(dist, np.int32)
    max_exact = REL_BUCKETS // 2
    safe = np.maximum(dist, 1).astype(np.float32)
    large = max_exact + (np.log(safe / max_exact) / np.log(REL_MAX_DIST / max_exact)
                         * (REL_BUCKETS - max_exact)).astype(np.int32)
    large = np.minimum(large, REL_BUCKETS - 1)
    return np.where(dist < max_exact, dist, large).astype(np.int32)


def _band_bias(rel_table, pattern):
    dilation = ATTN_PATTERNS[pattern][1]
    buckets = _t5_buckets(np.arange(WINDOW_KEYS + 1) * dilation)
    table = rel_table[:, pattern * HEADS_PER_PATTERN:(pattern + 1) * HEADS_PER_PATTERN].T
    runs = []
    for b in np.unique(buckets):
        runs.append(jnp.broadcast_to(table[:, b:b + 1], (HEADS_PER_PATTERN, int(np.sum(buckets == b)))))
    per_step = jnp.concatenate(runs, axis=1)
    period = 3 * SUB_Q
    masked = lambda n: jnp.full((HEADS_PER_PATTERN, n), NEG_BIG, F32)
    w = jnp.concatenate([masked(SUB_Q - 1), per_step[:, ::-1], masked(period - SUB_Q - WINDOW_KEYS)], axis=1)
    skew = jnp.tile(w, (1, SUB_Q))[:, :SUB_Q * (period - 1)].reshape(HEADS_PER_PATTERN, SUB_Q, period - 1)
    bias = skew[:, :, SUB_Q - 1:3 * SUB_Q - 1]
    first = jnp.where((np.arange(2 * SUB_Q) >= SUB_Q)[None, None, :], bias, NEG_BIG)
    pairs = lambda b: b.reshape(HEADS_PER_PATTERN // 2, 2 * SUB_Q, 2 * SUB_Q)
    return jnp.stack([pairs(bias), pairs(first)])


def _attn_kernel(q_ref, kc_ref, kh_ref, vc_ref, vh_ref, bias_ref, o_ref):
    n_res, tq = q_ref.shape[1], q_ref.shape[2]
    first = jnp.where(pl.program_id(2) == 0, 1, 0)
    lane = lax.broadcasted_iota(jnp.int32, (SUB_Q, 2 * HEAD_DIM), 1)
    low_half = lane < HEAD_DIM
    for r in range(n_res):
        for j in range(tq // SUB_Q):
            rows = slice(j * SUB_Q, (j + 1) * SUB_Q)
            q = q_ref[0, r, rows, :]
            if j == 0:
                k = jnp.concatenate([kh_ref[0, r], kc_ref[0, r, rows, :]], axis=0)
                v = jnp.concatenate([vh_ref[0, r], vc_ref[0, r, rows, :]], axis=0)
            else:
                k = kc_ref[0, r, (j - 1) * SUB_Q:(j + 1) * SUB_Q, :]
                v = vc_ref[0, r, (j - 1) * SUB_Q:(j + 1) * SUB_Q, :]
            for pair in range(HEADS_PER_PATTERN // 2):
                cols = slice(pair * 2 * HEAD_DIM, (pair + 1) * 2 * HEAD_DIM)
                q2, k2, v2 = q[:, cols], k[:, cols], v[:, cols]
                zero = jnp.zeros_like(q2)
                qq = jnp.concatenate([jnp.where(low_half, q2, zero), jnp.where(low_half, zero, q2)], axis=0)
                s = _dot_nt(qq, k2) + (bias_ref[first, pair] if j == 0 else bias_ref[0, pair])
                m = jnp.max(s, axis=-1, keepdims=True)
                p = jnp.exp(s - m)
                l = jnp.sum(p, axis=-1, keepdims=True)
                o = _dot(p.astype(BF16), v2) / l
                lse = jnp.broadcast_to(m + jnp.log(l), (2 * SUB_Q, 2 * HEAD_DIM))
                o_ref[0, r, rows, cols] = jnp.where(low_half, o[:SUB_Q], o[SUB_Q:])
                o_ref[0, r, rows, PATTERN_WIDTH + pair * 2 * HEAD_DIM:PATTERN_WIDTH + (pair + 1) * 2 * HEAD_DIM] = (
                    jnp.where(low_half, lse[:SUB_Q], lse[SUB_Q:]))


def _dilated_attention(q, k, v, rel_table, pattern):
    bsz, d, res_len, _ = q.shape
    tq = min(res_len, ATTN_STEP_ROWS)
    n_res = ATTN_STEP_ROWS // tq
    cur = pl.BlockSpec((1, n_res, tq, PATTERN_WIDTH), lambda b, r, i: (b, r, i, 0))
    halo = pl.BlockSpec((1, n_res, SUB_Q, PATTERN_WIDTH),
                        lambda b, r, i: (b, r, jnp.maximum(i * (tq // SUB_Q) - 1, 0), 0))
    return pl.pallas_call(
        _attn_kernel,
        grid=(bsz, d // n_res, res_len // tq),
        in_specs=[cur, cur, halo, cur, halo, _resident((2, HEADS_PER_PATTERN // 2, 2 * SUB_Q, 2 * SUB_Q))],
        out_specs=pl.BlockSpec((1, n_res, tq, 2 * PATTERN_WIDTH), lambda b, r, i: (b, r, i, 0)),
        out_shape=jax.ShapeDtypeStruct((bsz, d, res_len, 2 * PATTERN_WIDTH), F32),
        compiler_params=_params(3),
        name=f"attn_d{d}",
    )(q, k, k, v, v, _band_bias(rel_table, pattern))


def _memkv_kernel(mem_ref, g_ref, w_ref, k_ref, v_ref):
    mn = _rmsnorm(mem_ref[...], g_ref[...]).astype(BF16)
    k_ref[...] = _dot(mn, w_ref[:, :XATTN_WIDTH]).astype(BF16)
    v_ref[...] = _dot(mn, w_ref[:, XATTN_WIDTH:]).astype(BF16)


def _memkv(mem2d, norm_g, w_kv):
    rows = mem2d.shape[0]
    full = lambda *shape: pl.BlockSpec(shape, lambda i: (0,) * len(shape))
    return pl.pallas_call(
        _memkv_kernel,
        grid=(1,),
        in_specs=[full(rows, D_MODEL), full(1, D_MODEL), full(D_MODEL, 2 * XATTN_WIDTH)],
        out_specs=[full(rows, XATTN_WIDTH), full(rows, XATTN_WIDTH)],
        out_shape=[jax.ShapeDtypeStruct((rows, XATTN_WIDTH), BF16)] * 2,
        compiler_params=_params(1),
        name="memkv",
    )(mem2d, norm_g.reshape(1, D_MODEL), w_kv)


def _merge_kernel(x_ref, y_ref, o0_ref, o1_ref, o2_ref, mk_ref, mv_ref, g_ref, wpost_ref, wglu_ref,
                  wau_ref, wxu_ref, wout_ref, out_ref, ys_ref, s1_ref, s2_ref):
    x = x_ref[...]
    un = _rmsnorm(x, g_ref[...]).astype(BF16)

    def gate(index):
        lo = XATTN_WIDTH + index * D_MODEL
        return _sigmoid(_dot(un, wpost_ref[:, lo:lo + D_MODEL]))

    slabs = 2 * PATTERN_WIDTH // LANES
    for o_ref, s_ref in ((o1_ref, s1_ref), (o2_ref, s2_ref)):
        d = o_ref.shape[1]
        for r in range(d):
            for j in range(slabs):
                s_ref[j, pl.ds(r, TM_MERGE // d, stride=d), :] = o_ref[0, r, :, j * LANES:(j + 1) * LANES]
    half = slabs // 2
    nat = lambda s_ref, lo: jnp.concatenate([s_ref[j] for j in range(lo, lo + half)], axis=-1)
    outs = [o0_ref[0, 0, :, :PATTERN_WIDTH], nat(s1_ref, 0), nat(s2_ref, 0)]
    lses = [o0_ref[0, 0, :, PATTERN_WIDTH:], nat(s1_ref, half), nat(s2_ref, half)]
    top = jnp.maximum(jnp.maximum(lses[0], lses[1]), lses[2])
    ws = [jnp.exp(l - top) for l in lses]
    att = (ws[0] * outs[0] + ws[1] * outs[1] + ws[2] * outs[2]) / (ws[0] + ws[1] + ws[2])
    att = att.astype(BF16)

    chunk_rows = TM_MERGE // SSM_CHUNK
    for o in range(SSM_N_OCT):
        for t in range(SSM_CHUNK):
            lo = o * SSM_OCT_IN + t * LANES
            ys_ref[o, pl.ds(t, chunk_rows, stride=SSM_CHUNK), :] = y_ref[:, lo:lo + LANES].astype(F32)
    ys = jnp.concatenate([ys_ref[o] for o in range(SSM_N_OCT)], axis=-1).astype(BF16)
    glu = _dot(ys, wglu_ref[:, :D_MODEL]) * _sigmoid(_dot(ys, wglu_ref[:, D_MODEL:]))
    merged = gate(0) * glu

    xq = _dot(un, wpost_ref[:, :XATTN_WIDTH]).astype(BF16)
    heads = []
    for h in range(XATTN_HEADS):
        cols = slice(h * XATTN_HEAD_DIM, (h + 1) * XATTN_HEAD_DIM)
        s = _dot_nt(xq[:, cols], mk_ref[0, :, cols]) * (XATTN_HEAD_DIM ** -0.5)
        p = jnp.exp(s - jnp.max(s, axis=-1, keepdims=True))
        heads.append(_dot(p.astype(BF16), mv_ref[0, :, cols]) / jnp.sum(p, axis=-1, keepdims=True))
    xo = jnp.concatenate(heads, axis=-1).astype(BF16)
    merged = merged + gate(2) * _dot(xo, wxu_ref[...])

    merged = merged + gate(1) * _dot(att, wau_ref[...])

    out_ref[...] = x + _dot(merged.astype(BF16), wout_ref[...])


def _merge(x, y_rows, attn_outs, mk, mv, norm_g, w_post, w_glu, w_au, w_xu, w_out, seqlen):
    n = x.shape[0]
    tiles_per_seq = seqlen // TM_MERGE
    rows = lambda w: pl.BlockSpec((TM_MERGE, w), lambda i: (i, 0))
    mem = pl.BlockSpec((1, MEM_LEN, XATTN_WIDTH), lambda i: (i // tiles_per_seq, 0, 0))
    attn_specs = [pl.BlockSpec((1, d, TM_MERGE // d, 2 * PATTERN_WIDTH),
                               lambda i: (i // tiles_per_seq, 0, i % tiles_per_seq, 0))
                  for _, d in ATTN_PATTERNS]
    slabs = lambda k: pltpu.VMEM((k, TM_MERGE, LANES), F32)
    return pl.pallas_call(
        _merge_kernel,
        grid=(n // TM_MERGE,),
        in_specs=[rows(D_MODEL), pl.BlockSpec((TM_MERGE // SSM_CHUNK, SSM_ROW), lambda i: (i, 0))] + attn_specs
                 + [mem, mem, _resident((1, D_MODEL)), _resident((D_MODEL, POST_WIDTH)),
                    _resident((SSM_WIDTH, 2 * D_MODEL)), _resident((PATTERN_WIDTH, D_MODEL)),
                    _resident((XATTN_WIDTH, D_MODEL)), _resident((D_MODEL, D_MODEL))],
        out_specs=rows(D_MODEL),
        out_shape=jax.ShapeDtypeStruct((n, D_MODEL), F32),
        scratch_shapes=[slabs(SSM_N_OCT), slabs(2 * PATTERN_WIDTH // LANES), slabs(2 * PATTERN_WIDTH // LANES)],
        compiler_params=_params(1),
        name="merge",
    )(x, y_rows, *attn_outs, mk, mv, norm_g.reshape(1, D_MODEL), w_post, w_glu, w_au, w_xu, w_out)


def _layer(x, mem, rel_table, ffn1_norm, ffn1_w_in, ffn1_w_down, mix_norm, w_in,
           ssm_a_re, ssm_a_im, ssm_log_dt, ssm_b_re, ssm_b_im, ssm_c_re, ssm_c_im, ssm_d, ssm_w_glu,
           attn_w_up, mem_norm, xattn_w_kv, xattn_w_up, w_out, ffn2_norm, ffn2_w_in, ffn2_w_down,
           final_norm, bsz, seqlen):
    cast = [(w_in, (PRE_WIDTH, POST_WIDTH)), (ssm_w_glu, (2 * D_MODEL,)), (attn_w_up, (D_MODEL,)),
            (xattn_w_kv, (2 * XATTN_WIDTH,)), (xattn_w_up, (D_MODEL,)), (w_out, (D_MODEL,)),
            (ffn2_w_in, (2 * D_FF,)), (ffn2_w_down, (D_MODEL,))]
    x, (w_pre, w_post, w_glu, w_au, w_kv, w_xu, w_o, w2_in, w2_down) = _ffn(
        x, ffn1_norm, ffn1_w_in.astype(BF16), ffn1_w_down.astype(BF16), cast=cast)
    u_rows, q, k, v = _proj(x, mix_norm, w_pre, bsz, seqlen)
    compact = _ssm_compact(ssm_a_re, ssm_a_im, ssm_log_dt, ssm_b_re, ssm_b_im, ssm_c_re, ssm_c_im, ssm_d)
    y_rows = _ssm(u_rows, compact, bsz)
    attn_outs = [_dilated_attention(q[g], k[g], v[g], rel_table, g) for g in range(N_PATTERNS)]
    mk, mv = _memkv(mem.reshape(bsz * MEM_LEN, D_MODEL), mem_norm, w_kv)
    mk = mk.reshape(bsz, MEM_LEN, XATTN_WIDTH)
    mv = mv.reshape(bsz, MEM_LEN, XATTN_WIDTH)
    x = _merge(x, y_rows, attn_outs, mk, mv, mix_norm, w_post, w_glu, w_au, w_xu, w_o, seqlen)
    return _ffn(x, ffn2_norm, w2_in, w2_down, final_g=final_norm)[0]


def kernel(x, mem, ffn1_norm, ffn1_w_in, ffn1_w_down, mix_norm, w_in, ssm_a_re, ssm_a_im, ssm_log_dt,
           ssm_b_re, ssm_b_im, ssm_c_re, ssm_c_im, ssm_d, ssm_w_glu, rel_table, attn_w_up, mem_norm,
           xattn_w_kv, xattn_w_up, w_out, ffn2_norm, ffn2_w_in, ffn2_w_down, final_norm):
    bsz, seqlen, _ = x.shape
    assert ffn1_norm.shape[0] == 1, "single-layer trunk"
    assert seqlen % ATTN_STEP_ROWS == 0 and (seqlen // ATTN_PATTERNS[-1][1]) % SUB_Q == 0
    assert seqlen % TM_PROJ == 0 and seqlen % TM_MERGE == 0
    h = _layer(x.reshape(bsz * seqlen, D_MODEL), mem, rel_table, ffn1_norm[0], ffn1_w_in[0], ffn1_w_down[0],
               mix_norm[0], w_in[0], ssm_a_re[0], ssm_a_im[0], ssm_log_dt[0], ssm_b_re[0], ssm_b_im[0],
               ssm_c_re[0], ssm_c_im[0], ssm_d[0], ssm_w_glu[0], attn_w_up[0], mem_norm[0], xattn_w_kv[0],
               xattn_w_up[0], w_out[0], ffn2_norm[0], ffn2_w_in[0], ffn2_w_down[0], final_norm,
               bsz, seqlen)
    return h.reshape(bsz, seqlen, D_MODEL)
```

```python
import functools

import numpy as np
import jax
import jax.numpy as jnp
from jax import lax
from jax.experimental import pallas as pl
from jax.experimental.pallas import tpu as pltpu

F32 = jnp.float32
BF16 = jnp.bfloat16

LANES = 128
BF16_SUBLANES = 16
MXU_TILE = 256

D_MODEL = 1024
D_FF = 2816
EPS = 1e-6

HEAD_DIM = 64
ATTN_PATTERNS = ((128, 1), (512, 4), (2048, 16))
N_PATTERNS = 3
HEADS_PER_PATTERN = 4
ATTN_WIDTH = N_PATTERNS * HEADS_PER_PATTERN * HEAD_DIM
PATTERN_WIDTH = HEADS_PER_PATTERN * HEAD_DIM
WINDOW_KEYS = 128
REL_BUCKETS = 32
REL_MAX_DIST = 2048

SSM_GROUP = 16
SSM_STATE = 64
SSM_WIDTH = 512
SSM_CHUNK = 16
SSM_OCT = LANES // SSM_GROUP
SSM_N_OCT = SSM_WIDTH // LANES
SSM_OCT_IN = SSM_CHUNK * LANES
SSM_OCT_STATE = SSM_OCT * SSM_STATE
SSM_ROW = SSM_CHUNK * SSM_WIDTH

MEM_LEN = 256
XATTN_HEADS = 4
XATTN_HEAD_DIM = 128
XATTN_WIDTH = XATTN_HEADS * XATTN_HEAD_DIM

PRE_WIDTH = SSM_WIDTH + 3 * ATTN_WIDTH
POST_WIDTH = XATTN_WIDTH + 3 * D_MODEL

NEG_BIG = -1e30

VMEM_LIMIT_BYTES = 56 * 1024 * 1024

TM_FFN = 1024
FFN_HALVES = 2
FFN_CHUNKS = ((0, 1536), (1536, D_FF))
TM_PROJ = 1024
PROJ_HALVES = 2
SPLIT_STRIDE = 4
TM_MERGE = 512
ATTN_STEP_ROWS = 4096
SSM_SEQ_PER_STEP = 2
SUB_Q = 128


def _rmsnorm(x, g):
    return x * lax.rsqrt(jnp.mean(x * x, axis=-1, keepdims=True) + EPS) * g


def _sigmoid(x):
    return 0.5 * jnp.tanh(0.5 * x) + 0.5


def _dot(a, b):
    return jnp.dot(a, b, preferred_element_type=F32)


def _dot_nt(a, b):
    return lax.dot_general(a, b, (((1,), (1,)), ((), ())), preferred_element_type=F32)


def _resident(shape):
    nd = len(shape)
    return pl.BlockSpec(shape, lambda *_: (0,) * nd, pipeline_mode=pl.Buffered(1))


def _params(n_axes):
    return pltpu.CompilerParams(dimension_semantics=("arbitrary",) * n_axes,
                                vmem_limit_bytes=VMEM_LIMIT_BYTES)


def _ffn_kernel(x_ref, g_ref, win_ref, wdn_ref, *rest, final_norm, n_cast):
    rest = list(rest)
    fg_ref = rest.pop(0) if final_norm else None
    cast_src, o_ref, cast_dst = rest[:n_cast], rest[n_cast], rest[n_cast + 1:]
    for half in range(FFN_HALVES):
        rows = slice(half * TM_FFN // FFN_HALVES, (half + 1) * TM_FFN // FFN_HALVES)
        x = x_ref[rows, :]
        un = _rmsnorm(x, g_ref[...]).astype(BF16)
        acc = None
        for lo, hi in FFN_CHUNKS:
            a = _dot(un, win_ref[:, lo:hi])
            b = _dot(un, win_ref[:, D_FF + lo:D_FF + hi])
            h = (a * _sigmoid(a) * b).astype(BF16)
            part = _dot(h, wdn_ref[lo:hi, :])
            acc = part if acc is None else acc + part
        y = x + 0.5 * acc
        if final_norm:
            y = _rmsnorm(y, fg_ref[...])
        o_ref[rows, :] = y

    dst = iter(cast_dst)
    for src in cast_src:
        lo = 0
        while lo < src.shape[1]:
            out = next(dst)
            out[...] = src[:, lo:lo + out.shape[1]].astype(BF16)
            lo += out.shape[1]


def _cast_rows(n_rows, n_steps):
    rows = BF16_SUBLANES
    while n_rows % rows or n_rows // rows > n_steps:
        rows += BF16_SUBLANES
    return rows


def _ffn(x, norm_g, w_in, w_down, final_g=None, cast=()):
    n = x.shape[0]
    n_steps = n // TM_FFN
    row = pl.BlockSpec((TM_FFN, D_MODEL), lambda i: (i, 0))
    in_specs = [row, _resident((1, D_MODEL)), _resident((D_MODEL, 2 * D_FF)), _resident((D_FF, D_MODEL))]
    args = [x, norm_g.reshape(1, D_MODEL), w_in, w_down]
    if final_g is not None:
        in_specs.append(_resident((1, D_MODEL)))
        args.append(final_g.reshape(1, D_MODEL))
    out_specs, out_shapes = [row], [jax.ShapeDtypeStruct((n, D_MODEL), F32)]
    for w, widths in cast:
        assert sum(widths) == w.shape[1]
        rows = _cast_rows(w.shape[0], n_steps)
        last = w.shape[0] // rows - 1
        block = lambda width, last=last, rows=rows: pl.BlockSpec((rows, width), lambda i: (jnp.minimum(i, last), 0))
        in_specs.append(block(w.shape[1]))
        args.append(w)
        out_specs += [block(width) for width in widths]
        out_shapes += [jax.ShapeDtypeStruct((w.shape[0], width), BF16) for width in widths]
    outs = pl.pallas_call(
        functools.partial(_ffn_kernel, final_norm=final_g is not None, n_cast=len(cast)),
        grid=(n_steps,),
        in_specs=in_specs,
        out_specs=out_specs,
        out_shape=out_shapes,
        compiler_params=_params(1),
        name="ffn_final" if final_g is not None else "ffn",
    )(*args)
    return outs[0], outs[1:]


def _rows_by_residue(slab_ref, j, stage_ref, slot, n_rows, d):
    if d == 1:
        return [slab_ref[j]]
    if d == SPLIT_STRIDE:
        return [slab_ref[j, pl.ds(r, n_rows // d, stride=d), :] for r in range(d)]
    assert d == SPLIT_STRIDE * SPLIT_STRIDE
    for low in range(SPLIT_STRIDE):
        stage_ref[slot, low] = slab_ref[j, pl.ds(low, n_rows // SPLIT_STRIDE, stride=SPLIT_STRIDE), :]
    return [stage_ref[slot, r % SPLIT_STRIDE, pl.ds(r // SPLIT_STRIDE, n_rows // d, stride=SPLIT_STRIDE), :]
            for r in range(d)]


def _proj_kernel(x_ref, g_ref, w_ref, u_ref, *rest):
    qkv_refs, slabs = rest[:3 * N_PATTERNS], rest[3 * N_PATTERNS:]
    n_rows = TM_PROJ // PROJ_HALVES
    per_half = len(slabs) // PROJ_HALVES
    for half in range(PROJ_HALVES):
        base = half * n_rows
        su_ref, sq_ref, sk_ref, sv_ref, stage_ref = slabs[per_half * half:per_half * (half + 1)]
        un = _rmsnorm(x_ref[base:base + n_rows, :], g_ref[...]).astype(BF16)
        full = _dot(un, w_ref[...])
        slot = 0

        u = full[:, :SSM_WIDTH]
        chunks = slice(base // SSM_CHUNK, (base + n_rows) // SSM_CHUNK)
        for o in range(SSM_N_OCT):
            su_ref[o] = u[:, o * LANES:(o + 1) * LANES]
            for t, rows in enumerate(_rows_by_residue(su_ref, o, stage_ref, slot, n_rows, SSM_CHUNK)):
                lo = o * SSM_OCT_IN + t * LANES
                u_ref[chunks, lo:lo + LANES] = rows.astype(BF16)
            slot += 1

        for which, scr in enumerate((sq_ref, sk_ref, sv_ref)):
            lo = SSM_WIDTH + which * ATTN_WIDTH
            res = full[:, lo:lo + ATTN_WIDTH]
            if which == 0:
                res = res * (HEAD_DIM ** -0.5)
            for j in range(ATTN_WIDTH // LANES):
                scr[j] = res[:, j * LANES:(j + 1) * LANES]
            for g, (_, d) in enumerate(ATTN_PATTERNS):
                out = qkv_refs[which * N_PATTERNS + g]
                for jj in range(PATTERN_WIDTH // LANES):
                    j = g * (PATTERN_WIDTH // LANES) + jj
                    for r, rows in enumerate(_rows_by_residue(scr, j, stage_ref, slot, n_rows, d)):
                        out[0, r, base // d:(base + n_rows) // d, jj * LANES:(jj + 1) * LANES] = rows.astype(BF16)
                    slot += d == SPLIT_STRIDE * SPLIT_STRIDE


def _proj(x, norm_g, w_pre, bsz, seqlen):
    n = x.shape[0]
    tiles_per_seq = seqlen // TM_PROJ
    rows = lambda w: pl.BlockSpec((TM_PROJ, w), lambda i: (i, 0))
    qkv_specs, qkv_shapes = [], []
    for _ in range(3):
        for _, d in ATTN_PATTERNS:
            qkv_specs.append(pl.BlockSpec((1, d, TM_PROJ // d, PATTERN_WIDTH),
                                          lambda i: (i // tiles_per_seq, 0, i % tiles_per_seq, 0)))
            qkv_shapes.append(jax.ShapeDtypeStruct((bsz, d, seqlen // d, PATTERN_WIDTH), BF16))
    half_rows = TM_PROJ // PROJ_HALVES
    slabs = lambda k: pltpu.VMEM((k, half_rows, LANES), F32)
    n_staged = SSM_N_OCT + 3 * PATTERN_WIDTH // LANES
    stage = pltpu.VMEM((n_staged, SPLIT_STRIDE, half_rows // SPLIT_STRIDE, LANES), F32)
    outs = pl.pallas_call(
        _proj_kernel,
        grid=(n // TM_PROJ,),
        in_specs=[rows(D_MODEL), _resident((1, D_MODEL)), _resident((D_MODEL, PRE_WIDTH))],
        out_specs=[pl.BlockSpec((TM_PROJ // SSM_CHUNK, SSM_ROW), lambda i: (i, 0))] + qkv_specs,
        out_shape=[jax.ShapeDtypeStruct((n // SSM_CHUNK, SSM_ROW), BF16)] + qkv_shapes,
        scratch_shapes=([slabs(SSM_N_OCT)] + [slabs(ATTN_WIDTH // LANES)] * 3 + [stage]) * PROJ_HALVES,
        compiler_params=_params(1),
        name="proj",
    )(x, norm_g.reshape(1, D_MODEL), w_pre)
    u = outs[0]
    q, k, v = (outs[1 + i * N_PATTERNS:1 + (i + 1) * N_PATTERNS] for i in range(3))
    return u, q, k, v


def _ssm_compact(a_re, a_im, log_dt, b_re, b_im, c_re, c_im, d_skip):
    t_len = SSM_CHUNK
    dt = jnp.exp(log_dt)[:, None]
    mag = jnp.exp(a_re * dt)
    ang = a_im * dt
    abar_re = mag * jnp.cos(ang)
    abar_im = mag * jnp.sin(ang)
    nr = abar_re - 1.0
    ni = abar_im
    den = a_re * a_re + a_im * a_im
    coef_re = (nr * a_re + ni * a_im) / den
    coef_im = (ni * a_re - nr * a_im) / den
    bbar_re = coef_re[..., None] * b_re - coef_im[..., None] * b_im
    bbar_im = coef_re[..., None] * b_im + coef_im[..., None] * b_re
    bt_re = jnp.swapaxes(bbar_re, -1, -2)
    bt_im = jnp.swapaxes(bbar_im, -1, -2)
    pw_re = [jnp.ones_like(abar_re)]
    pw_im = [jnp.zeros_like(abar_re)]
    for _ in range(t_len):
        pr, pi = pw_re[-1], pw_im[-1]
        pw_re.append(pr * abar_re - pi * abar_im)
        pw_im.append(pr * abar_im + pi * abar_re)
    rev_re = jnp.stack(pw_re[t_len - 1::-1])[:, :, None, :]
    rev_im = jnp.stack(pw_im[t_len - 1::-1])[:, :, None, :]
    pw_re = jnp.stack(pw_re)
    pw_im = jnp.stack(pw_im)
    n_oct, oct_, grp, st = SSM_N_OCT, SSM_OCT, SSM_GROUP, SSM_STATE

    def per_octet(re, im):
        both = jnp.concatenate([re, im], axis=-1)
        n = both.shape[0]
        return both.reshape(n, n_oct, oct_ * grp, 2 * st).transpose(1, 0, 2, 3).reshape(n_oct, n * LANES, 2 * st)

    into_c = per_octet(rev_re * bt_re[None] - rev_im * bt_im[None], rev_re * bt_im[None] + rev_im * bt_re[None])
    cp_re = c_re[None] * pw_re[:, :, None, :] - c_im[None] * pw_im[:, :, None, :]
    cp_im = c_re[None] * pw_im[:, :, None, :] + c_im[None] * pw_re[:, :, None, :]
    outof_c = per_octet(cp_re, -cp_im)
    decay = jnp.stack([pw_re[t_len].reshape(n_oct, SSM_OCT_STATE), pw_im[t_len].reshape(n_oct, SSM_OCT_STATE)]
                      + [jnp.zeros((n_oct, SSM_OCT_STATE), F32)] * 6, axis=1)
    skip = d_skip.reshape(n_oct, 1, LANES)
    return into_c.astype(BF16), outof_c.astype(BF16), decay, skip


def _ssm_expanders():
    row = np.arange(LANES)
    col = np.arange(2 * SSM_OCT_STATE)
    spread = (row[:, None] // SSM_STATE == col[None, :] // SSM_OCT_STATE) & (
        row[:, None] % SSM_STATE == col[None, :] % SSM_STATE)
    keep = row[:, None] // SSM_GROUP == (col[None, :] % SSM_OCT_STATE) // SSM_STATE
    return jnp.asarray(spread, BF16), jnp.asarray(keep, F32)


def _ssm_kernel(x_ref, intoc_ref, outofc_ref, decay_ref, skip_ref, spread_ref, keep_ref, y_ref,
                toep_ref, into_ref, outof_ref, loc_ref, prev_ref, *, n_chunks, n_seq):
    blk = lambda i: slice(i * LANES, (i + 1) * LANES)

    @pl.when(pl.program_id(1) == 0)
    def _build_octet_matrices():
        spread, keep = spread_ref[...], keep_ref[...]
        for s in range(SSM_CHUNK):
            into_ref[blk(s), :] = (_dot(intoc_ref[0, blk(s), :], spread) * keep).astype(BF16)
        for j in range(SSM_CHUNK + 1):
            outof_ref[blk(j), :] = (_dot(outofc_ref[0, blk(j), :], spread) * keep).astype(BF16)
        bbar = into_ref[blk(SSM_CHUNK - 1), :]
        diag = (lax.broadcasted_iota(jnp.int32, (LANES, LANES), 0)
                == lax.broadcasted_iota(jnp.int32, (LANES, LANES), 1))
        zeros = jnp.zeros((LANES, LANES), BF16)
        for j in range(SSM_CHUNK):
            lag = _dot_nt(bbar, outof_ref[blk(j), :])
            if j == 0:
                lag = lag + jnp.where(diag, skip_ref[0], 0.0)
            lag = lag.astype(BF16)
            for s in range(SSM_CHUNK - j):
                toep_ref[blk(s), blk(s + j)] = lag
            if j > 0:
                for s in range(j, SSM_CHUNK):
                    toep_ref[blk(s), blk(s - j)] = zeros

    loc_ref[...] = _dot(x_ref[...], into_ref[...])
    ar = decay_ref[0, 0:1, :]
    ai = decay_ref[0, 1:2, :]

    def step(c, carry):
        new = []
        for q in range(n_seq):
            s_re, s_im = carry[2 * q], carry[2 * q + 1]
            row = pl.ds(q * n_chunks + c, 1)
            prev_ref[row, :SSM_OCT_STATE] = s_re
            prev_ref[row, SSM_OCT_STATE:] = s_im
            new.append(ar * s_re - ai * s_im + loc_ref[row, :SSM_OCT_STATE])
            new.append(ar * s_im + ai * s_re + loc_ref[row, SSM_OCT_STATE:])
        return tuple(new)

    zero = jnp.zeros((1, SSM_OCT_STATE), F32)
    lax.fori_loop(0, n_chunks, step, (zero,) * (2 * n_seq), unroll=True)
    prev = prev_ref[...].astype(BF16)
    for j in range(SSM_OCT_IN // MXU_TILE):
        lo, hi = j * MXU_TILE, (j + 1) * MXU_TILE
        y = _dot(x_ref[:, :hi], toep_ref[:hi, lo:hi]) + _dot_nt(prev, outof_ref[lo + LANES:hi + LANES, :])
        y_ref[:, lo:hi] = jax.nn.gelu(y).astype(BF16)


def _ssm(u_rows, compact, bsz):
    into_c, outof_c, decay, skip = compact
    spread, keep = _ssm_expanders()
    n_chunks = u_rows.shape[0] // bsz
    n_seq = SSM_SEQ_PER_STEP
    per_oct = lambda *shape: pl.BlockSpec((1,) + shape, lambda o, b: (o,) + (0,) * len(shape))
    rows = pl.BlockSpec((n_seq * n_chunks, SSM_OCT_IN), lambda o, b: (b, o))
    wide = 2 * SSM_OCT_STATE
    return pl.pallas_call(
        functools.partial(_ssm_kernel, n_chunks=n_chunks, n_seq=n_seq),
        grid=(SSM_N_OCT, bsz // n_seq),
        in_specs=[rows, per_oct(SSM_OCT_IN, LANES), per_oct(SSM_OCT_IN + LANES, LANES),
                  per_oct(8, SSM_OCT_STATE), per_oct(1, LANES), _resident((LANES, wide)), _resident((LANES, wide))],
        out_specs=rows,
        out_shape=jax.ShapeDtypeStruct(u_rows.shape, BF16),
        scratch_shapes=[pltpu.VMEM((SSM_OCT_IN, SSM_OCT_IN), BF16),
                        pltpu.VMEM((SSM_OCT_IN, wide), BF16),
                        pltpu.VMEM((SSM_OCT_IN + LANES, wide), BF16),
                        pltpu.VMEM((n_seq * n_chunks, wide), F32),
                        pltpu.VMEM((n_seq * n_chunks, wide), F32)],
        compiler_params=_params(2),
        name="ssm",
    )(u_rows, into_c, outof_c, decay, skip, spread, keep)


def _t5_buckets(dist):
    dist = np.asarray(dist, np.int32)
    max_exact = REL_BUCKETS // 2
    safe = np.maximum(dist, 1).astype(np.float32)
    large = max_exact + (np.log(safe / max_exact) / np.log(REL_MAX_DIST / max_exact)
                         * (REL_BUCKETS - max_exact)).astype(np.int32)
    large = np.minimum(large, REL_BUCKETS - 1)
    return np.where(dist < max_exact, dist, large).astype(np.int32)


def _band_bias(rel_table, pattern):
    dilation = ATTN_PATTERNS[pattern][1]
    buckets = _t5_buckets(np.arange(WINDOW_KEYS + 1) * dilation)
    table = rel_table[:, pattern * HEADS_PER_PATTERN:(pattern + 1) * HEADS_PER_PATTERN].T
    runs = []
    for b in np.unique(buckets):
        runs.append(jnp.broadcast_to(table[:, b:b + 1], (HEADS_PER_PATTERN, int(np.sum(buckets == b)))))
    per_step = jnp.concatenate(runs, axis=1)
    period = 3 * SUB_Q
    masked = lambda n: jnp.full((HEADS_PER_PATTERN, n), NEG_BIG, F32)
    w = jnp.concatenate([masked(SUB_Q - 1), per_step[:, ::-1], masked(period - SUB_Q - WINDOW_KEYS)], axis=1)
    skew = jnp.tile(w, (1, SUB_Q))[:, :SUB_Q * (period - 1)].reshape(HEADS_PER_PATTERN, SUB_Q, period - 1)
    bias = skew[:, :, SUB_Q - 1:3 * SUB_Q - 1]
    first = jnp.where((np.arange(2 * SUB_Q) >= SUB_Q)[None, None, :], bias, NEG_BIG)
    pairs = lambda b: b.reshape(HEADS_PER_PATTERN // 2, 2 * SUB_Q, 2 * SUB_Q)
    return jnp.stack([pairs(bias), pairs(first)])


def _attn_kernel(q_ref, kc_ref, kh_ref, vc_ref, vh_ref, bias_ref, o_ref):
    n_res, tq = q_ref.shape[1], q_ref.shape[2]
    first = jnp.where(pl.program_id(2) == 0, 1, 0)
    lane = lax.broadcasted_iota(jnp.int32, (SUB_Q, 2 * HEAD_DIM), 1)
    low_half = lane < HEAD_DIM
    stat_is_max = lax.broadcasted_iota(jnp.int32, (2 * SUB_Q, 2 * HEAD_DIM), 1) % HEAD_DIM < HEAD_DIM // 2
    for r in range(n_res):
        for j in range(tq // SUB_Q):
            rows = slice(j * SUB_Q, (j + 1) * SUB_Q)
            q = q_ref[0, r, rows, :]
            if j == 0:
                k = jnp.concatenate([kh_ref[0, r], kc_ref[0, r, rows, :]], axis=0)
                v = jnp.concatenate([vh_ref[0, r], vc_ref[0, r, rows, :]], axis=0)
            else:
                k = kc_ref[0, r, (j - 1) * SUB_Q:(j + 1) * SUB_Q, :]
                v = vc_ref[0, r, (j - 1) * SUB_Q:(j + 1) * SUB_Q, :]
            for pair in range(HEADS_PER_PATTERN // 2):
                cols = slice(pair * 2 * HEAD_DIM, (pair + 1) * 2 * HEAD_DIM)
                q2, k2, v2 = q[:, cols], k[:, cols], v[:, cols]
                zero = jnp.zeros_like(q2)
                qq = jnp.concatenate([jnp.where(low_half, q2, zero), jnp.where(low_half, zero, q2)], axis=0)
                s = _dot_nt(qq, k2) + (bias_ref[first, pair] if j == 0 else bias_ref[0, pair])
                m = jnp.max(s, axis=-1, keepdims=True)
                p = jnp.exp(s - m)
                l = jnp.sum(p, axis=-1, keepdims=True)
                o = _dot(p.astype(BF16), v2)
                wide = (2 * SUB_Q, 2 * HEAD_DIM)
                stat = jnp.where(stat_is_max, jnp.broadcast_to(m, wide), jnp.broadcast_to(l, wide))
                o_ref[0, r, rows, cols] = jnp.where(low_half, o[:SUB_Q], o[SUB_Q:])
                o_ref[0, r, rows, PATTERN_WIDTH + pair * 2 * HEAD_DIM:PATTERN_WIDTH + (pair + 1) * 2 * HEAD_DIM] = (
                    jnp.where(low_half, stat[:SUB_Q], stat[SUB_Q:]))


def _dilated_attention(q, k, v, rel_table, pattern):
    bsz, d, res_len, _ = q.shape
    tq = min(res_len, ATTN_STEP_ROWS)
    n_res = ATTN_STEP_ROWS // tq
    cur = pl.BlockSpec((1, n_res, tq, PATTERN_WIDTH), lambda b, r, i: (b, r, i, 0))
    halo = pl.BlockSpec((1, n_res, SUB_Q, PATTERN_WIDTH),
                        lambda b, r, i: (b, r, jnp.maximum(i * (tq // SUB_Q) - 1, 0), 0))
    return pl.pallas_call(
        _attn_kernel,
        grid=(bsz, d // n_res, res_len // tq),
        in_specs=[cur, cur, halo, cur, halo, _resident((2, HEADS_PER_PATTERN // 2, 2 * SUB_Q, 2 * SUB_Q))],
        out_specs=pl.BlockSpec((1, n_res, tq, 2 * PATTERN_WIDTH), lambda b, r, i: (b, r, i, 0)),
        out_shape=jax.ShapeDtypeStruct((bsz, d, res_len, 2 * PATTERN_WIDTH), F32),
        compiler_params=_params(3),
        name=f"attn_d{d}",
    )(q, k, k, v, v, _band_bias(rel_table, pattern))


def _memkv_kernel(mem_ref, g_ref, w_ref, k_ref, v_ref):
    mn = _rmsnorm(mem_ref[...], g_ref[...]).astype(BF16)
    k_ref[...] = _dot(mn, w_ref[:, :XATTN_WIDTH]).astype(BF16)
    v_ref[...] = _dot(mn, w_ref[:, XATTN_WIDTH:]).astype(BF16)


def _memkv(mem2d, norm_g, w_kv):
    rows = mem2d.shape[0]
    full = lambda *shape: pl.BlockSpec(shape, lambda i: (0,) * len(shape))
    return pl.pallas_call(
        _memkv_kernel,
        grid=(1,),
        in_specs=[full(rows, D_MODEL), full(1, D_MODEL), full(D_MODEL, 2 * XATTN_WIDTH)],
        out_specs=[full(rows, XATTN_WIDTH), full(rows, XATTN_WIDTH)],
        out_shape=[jax.ShapeDtypeStruct((rows, XATTN_WIDTH), BF16)] * 2,
        compiler_params=_params(1),
        name="memkv",
    )(mem2d, norm_g.reshape(1, D_MODEL), w_kv)


def _merge_kernel(x_ref, y_ref, o0_ref, o1_ref, o2_ref, mk_ref, mv_ref, g_ref, wpost_ref, wglu_ref,
                  wau_ref, wxu_ref, wout_ref, out_ref, ys_ref, s1_ref, s2_ref):
    x = x_ref[...]
    un = _rmsnorm(x, g_ref[...]).astype(BF16)

    def gate(index):
        lo = XATTN_WIDTH + index * D_MODEL
        return _sigmoid(_dot(un, wpost_ref[:, lo:lo + D_MODEL]))

    slabs = 2 * PATTERN_WIDTH // LANES
    for o_ref, s_ref in ((o1_ref, s1_ref), (o2_ref, s2_ref)):
        d = o_ref.shape[1]
        for r in range(d):
            for j in range(slabs):
                s_ref[j, pl.ds(r, TM_MERGE // d, stride=d), :] = o_ref[0, r, :, j * LANES:(j + 1) * LANES]
    half = slabs // 2
    is_max = lax.broadcasted_iota(jnp.int32, (TM_MERGE, LANES), 1) % HEAD_DIM < HEAD_DIM // 2

    def unpack(stat):
        return (jnp.where(is_max, stat, pltpu.roll(stat, HEAD_DIM // 2, axis=1)),
                jnp.where(is_max, pltpu.roll(stat, LANES - HEAD_DIM // 2, axis=1), stat))

    def pattern(read):
        out = jnp.concatenate([read(j) for j in range(half)], axis=-1)
        stats = [unpack(read(j)) for j in range(half, slabs)]
        return (out, jnp.concatenate([m for m, _ in stats], axis=-1),
                jnp.concatenate([l for _, l in stats], axis=-1))

    parts = [pattern(lambda j: o0_ref[0, 0, :, j * LANES:(j + 1) * LANES]),
             pattern(lambda j: s1_ref[j]), pattern(lambda j: s2_ref[j])]
    top = jnp.maximum(jnp.maximum(parts[0][1], parts[1][1]), parts[2][1])
    ws = [jnp.exp(m - top) for _, m, _ in parts]
    att = ((ws[0] * parts[0][0] + ws[1] * parts[1][0] + ws[2] * parts[2][0])
           / (ws[0] * parts[0][2] + ws[1] * parts[1][2] + ws[2] * parts[2][2]))
    att = att.astype(BF16)

    chunk_rows = TM_MERGE // SSM_CHUNK
    for o in range(SSM_N_OCT):
        for t in range(SSM_CHUNK):
            lo = o * SSM_OCT_IN + t * LANES
            ys_ref[o, pl.ds(t, chunk_rows, stride=SSM_CHUNK), :] = y_ref[:, lo:lo + LANES].astype(F32)
    ys = jnp.concatenate([ys_ref[o] for o in range(SSM_N_OCT)], axis=-1).astype(BF16)
    glu = _dot(ys, wglu_ref[:, :D_MODEL]) * _sigmoid(_dot(ys, wglu_ref[:, D_MODEL:]))
    merged = gate(0) * glu

    xq = _dot(un, wpost_ref[:, :XATTN_WIDTH]).astype(BF16)
    heads = []
    for h in range(XATTN_HEADS):
        cols = slice(h * XATTN_HEAD_DIM, (h + 1) * XATTN_HEAD_DIM)
        s = _dot_nt(xq[:, cols], mk_ref[0, :, cols]) * (XATTN_HEAD_DIM ** -0.5)
        p = jnp.exp(s - jnp.max(s, axis=-1, keepdims=True))
        heads.append(_dot(p.astype(BF16), mv_ref[0, :, cols]) / jnp.sum(p, axis=-1, keepdims=True))
    xo = jnp.concatenate(heads, axis=-1).astype(BF16)
    merged = merged + gate(2) * _dot(xo, wxu_ref[...])

    merged = merged + gate(1) * _dot(att, wau_ref[...])

    out_ref[...] = x + _dot(merged.astype(BF16), wout_ref[...])


def _merge(x, y_rows, attn_outs, mk, mv, norm_g, w_post, w_glu, w_au, w_xu, w_out, seqlen):
    n = x.shape[0]
    tiles_per_seq = seqlen // TM_MERGE
    rows = lambda w: pl.BlockSpec((TM_MERGE, w), lambda i: (i, 0))
    mem = pl.BlockSpec((1, MEM_LEN, XATTN_WIDTH), lambda i: (i // tiles_per_seq, 0, 0))
    attn_specs = [pl.BlockSpec((1, d, TM_MERGE // d, 2 * PATTERN_WIDTH),
                               lambda i: (i // tiles_per_seq, 0, i % tiles_per_seq, 0))
                  for _, d in ATTN_PATTERNS]
    slabs = lambda k: pltpu.VMEM((k, TM_MERGE, LANES), F32)
    return pl.pallas_call(
        _merge_kernel,
        grid=(n // TM_MERGE,),
        in_specs=[rows(D_MODEL), pl.BlockSpec((TM_MERGE // SSM_CHUNK, SSM_ROW), lambda i: (i, 0))] + attn_specs
                 + [mem, mem, _resident((1, D_MODEL)), _resident((D_MODEL, POST_WIDTH)),
                    _resident((SSM_WIDTH, 2 * D_MODEL)), _resident((PATTERN_WIDTH, D_MODEL)),
                    _resident((XATTN_WIDTH, D_MODEL)), _resident((D_MODEL, D_MODEL))],
        out_specs=rows(D_MODEL),
        out_shape=jax.ShapeDtypeStruct((n, D_MODEL), F32),
        scratch_shapes=[slabs(SSM_N_OCT), slabs(2 * PATTERN_WIDTH // LANES), slabs(2 * PATTERN_WIDTH // LANES)],
        compiler_params=_params(1),
        name="merge",
    )(x, y_rows, *attn_outs, mk, mv, norm_g.reshape(1, D_MODEL), w_post, w_glu, w_au, w_xu, w_out)


def _layer(x, mem, rel_table, ffn1_norm, ffn1_w_in, ffn1_w_down, mix_norm, w_in,
           ssm_a_re, ssm_a_im, ssm_log_dt, ssm_b_re, ssm_b_im, ssm_c_re, ssm_c_im, ssm_d, ssm_w_glu,
           attn_w_up, mem_norm, xattn_w_kv, xattn_w_up, w_out, ffn2_norm, ffn2_w_in, ffn2_w_down,
           final_norm, bsz, seqlen):
    cast = [(w_in, (PRE_WIDTH, POST_WIDTH)), (ssm_w_glu, (2 * D_MODEL,)), (attn_w_up, (D_MODEL,)),
            (xattn_w_kv, (2 * XATTN_WIDTH,)), (xattn_w_up, (D_MODEL,)), (w_out, (D_MODEL,)),
            (ffn2_w_in, (2 * D_FF,)), (ffn2_w_down, (D_MODEL,))]
    x, (w_pre, w_post, w_glu, w_au, w_kv, w_xu, w_o, w2_in, w2_down) = _ffn(
        x, ffn1_norm, ffn1_w_in.astype(BF16), ffn1_w_down.astype(BF16), cast=cast)
    u_rows, q, k, v = _proj(x, mix_norm, w_pre, bsz, seqlen)
    compact = _ssm_compact(ssm_a_re, ssm_a_im, ssm_log_dt, ssm_b_re, ssm_b_im, ssm_c_re, ssm_c_im, ssm_d)
    y_rows = _ssm(u_rows, compact, bsz)
    attn_outs = [_dilated_attention(q[g], k[g], v[g], rel_table, g) for g in range(N_PATTERNS)]
    mk, mv = _memkv(mem.reshape(bsz * MEM_LEN, D_MODEL), mem_norm, w_kv)
    mk = mk.reshape(bsz, MEM_LEN, XATTN_WIDTH)
    mv = mv.reshape(bsz, MEM_LEN, XATTN_WIDTH)
    x = _merge(x, y_rows, attn_outs, mk, mv, mix_norm, w_post, w_glu, w_au, w_xu, w_o, seqlen)
    return _ffn(x, ffn2_norm, w2_in, w2_down, final_g=final_norm)[0]


def kernel(x, mem, ffn1_norm, ffn1_w_in, ffn1_w_down, mix_norm, w_in, ssm_a_re, ssm_a_im, ssm_log_dt,
           ssm_b_re, ssm_b_im, ssm_c_re, ssm_c_im, ssm_d, ssm_w_glu, rel_table, attn_w_up, mem_norm,
           xattn_w_kv, xattn_w_up, w_out, ffn2_norm, ffn2_w_in, ffn2_w_down, final_norm):
    bsz, seqlen, _ = x.shape
    assert ffn1_norm.shape[0] == 1, "single-layer trunk"
    assert x.shape[2] == D_MODEL and mem.shape == (bsz, MEM_LEN, D_MODEL)
    assert w_in.shape[1:] == (D_MODEL, PRE_WIDTH + POST_WIDTH) and ffn1_w_in.shape[1:] == (D_MODEL, 2 * D_FF)
    assert ssm_b_re.shape[1:] == (SSM_WIDTH // SSM_GROUP, SSM_STATE, SSM_GROUP)
    assert rel_table.shape == (REL_BUCKETS, N_PATTERNS * HEADS_PER_PATTERN)
    assert seqlen % ATTN_STEP_ROWS == 0 and (seqlen // ATTN_PATTERNS[-1][1]) % SUB_Q == 0
    assert seqlen % TM_PROJ == 0 and seqlen % TM_MERGE == 0 and (bsz * seqlen) % TM_FFN == 0
    assert bsz % SSM_SEQ_PER_STEP == 0
    h = _layer(x.reshape(bsz * seqlen, D_MODEL), mem, rel_table, ffn1_norm[0], ffn1_w_in[0], ffn1_w_down[0],
               mix_norm[0], w_in[0], ssm_a_re[0], ssm_a_im[0], ssm_log_dt[0], ssm_b_re[0], ssm_b_im[0],
               ssm_c_re[0], ssm_c_im[0], ssm_d[0], ssm_w_glu[0], attn_w_up[0], mem_norm[0], xattn_w_kv[0],
               xattn_w_up[0], w_out[0], ffn2_norm[0], ffn2_w_in[0], ffn2_w_down[0], final_norm,
               bsz, seqlen)
    return h.reshape(bsz, seqlen, D_MODEL)
```

```python
import functools

import numpy as np
import jax
import jax.numpy as jnp
from jax import lax
from jax.experimental import pallas as pl
from jax.experimental.pallas import tpu as pltpu

F32 = jnp.float32
BF16 = jnp.bfloat16

LANES = 128
BF16_SUBLANES = 16
MXU_TILE = 256

D_MODEL = 1024
D_FF = 2816
EPS = 1e-6

HEAD_DIM = 64
ATTN_PATTERNS = ((128, 1), (512, 4), (2048, 16))
N_PATTERNS = 3
HEADS_PER_PATTERN = 4
ATTN_WIDTH = N_PATTERNS * HEADS_PER_PATTERN * HEAD_DIM
PATTERN_WIDTH = HEADS_PER_PATTERN * HEAD_DIM
WINDOW_KEYS = 128
REL_BUCKETS = 32
REL_MAX_DIST = 2048

SSM_GROUP = 16
SSM_STATE = 64
SSM_WIDTH = 512
SSM_CHUNK = 16
SSM_OCT = LANES // SSM_GROUP
SSM_N_OCT = SSM_WIDTH // LANES
SSM_OCT_IN = SSM_CHUNK * LANES
SSM_OCT_STATE = SSM_OCT * SSM_STATE
SSM_ROW = SSM_CHUNK * SSM_WIDTH

MEM_LEN = 256
XATTN_HEADS = 4
XATTN_HEAD_DIM = 128
XATTN_WIDTH = XATTN_HEADS * XATTN_HEAD_DIM

PRE_WIDTH = SSM_WIDTH + 3 * ATTN_WIDTH
POST_WIDTH = XATTN_WIDTH + 3 * D_MODEL

NEG_BIG = -1e30

VMEM_LIMIT_BYTES = 56 * 1024 * 1024

TM_FFN = 1024
FFN_ROW_PARTS = 4
FFN_CHUNKS = ((0, 1536), (1536, D_FF))
TM_PROJ = 1024
PROJ_HALVES = 2
SPLIT_STRIDE = 4
TM_MERGE = 512
ATTN_STEP_ROWS = 4096
SSM_SEQ_PER_STEP = 2
SUB_Q = 128


def _rmsnorm(x, g):
    return x * lax.rsqrt(jnp.mean(x * x, axis=-1, keepdims=True) + EPS) * g


def _sigmoid(x):
    return 0.5 * jnp.tanh(0.5 * x) + 0.5


def _dot(a, b):
    return jnp.dot(a, b, preferred_element_type=F32)


def _dot_nt(a, b):
    return lax.dot_general(a, b, (((1,), (1,)), ((), ())), preferred_element_type=F32)


def _resident(shape):
    nd = len(shape)
    return pl.BlockSpec(shape, lambda *_: (0,) * nd, pipeline_mode=pl.Buffered(1))


def _params(n_axes):
    return pltpu.CompilerParams(dimension_semantics=("arbitrary",) * n_axes,
                                vmem_limit_bytes=VMEM_LIMIT_BYTES)


def _ffn_kernel(x_ref, g_ref, win_ref, wdn_ref, *rest, final_norm, n_cast):
    rest = list(rest)
    fg_ref = rest.pop(0) if final_norm else None
    cast_src, o_ref, cast_dst = rest[:n_cast], rest[n_cast], rest[n_cast + 1:]
    for part in range(FFN_ROW_PARTS):
        rows = slice(part * TM_FFN // FFN_ROW_PARTS, (part + 1) * TM_FFN // FFN_ROW_PARTS)
        x = x_ref[rows, :]
        un = _rmsnorm(x, g_ref[...]).astype(BF16)
        acc = None
        for lo, hi in FFN_CHUNKS:
            a = _dot(un, win_ref[:, lo:hi])
            b = _dot(un, win_ref[:, D_FF + lo:D_FF + hi])
            h = (a * _sigmoid(a) * b).astype(BF16)
            part = _dot(h, wdn_ref[lo:hi, :])
            acc = part if acc is None else acc + part
        y = x + 0.5 * acc
        if final_norm:
            y = _rmsnorm(y, fg_ref[...])
        o_ref[rows, :] = y

    dst = iter(cast_dst)
    for src in cast_src:
        lo = 0
        while lo < src.shape[1]:
            out = next(dst)
            out[...] = src[:, lo:lo + out.shape[1]].astype(BF16)
            lo += out.shape[1]


def _cast_rows(n_rows, n_steps):
    rows = BF16_SUBLANES
    while n_rows % rows or n_rows // rows > n_steps:
        rows += BF16_SUBLANES
    return rows


def _ffn(x, norm_g, w_in, w_down, final_g=None, cast=()):
    n = x.shape[0]
    n_steps = n // TM_FFN
    row = pl.BlockSpec((TM_FFN, D_MODEL), lambda i: (i, 0))
    in_specs = [row, _resident((1, D_MODEL)), _resident((D_MODEL, 2 * D_FF)), _resident((D_FF, D_MODEL))]
    args = [x, norm_g.reshape(1, D_MODEL), w_in, w_down]
    if final_g is not None:
        in_specs.append(_resident((1, D_MODEL)))
        args.append(final_g.reshape(1, D_MODEL))
    out_specs, out_shapes = [row], [jax.ShapeDtypeStruct((n, D_MODEL), F32)]
    for w, widths in cast:
        assert sum(widths) == w.shape[1]
        rows = _cast_rows(w.shape[0], n_steps)
        last = w.shape[0] // rows - 1
        block = lambda width, last=last, rows=rows: pl.BlockSpec((rows, width), lambda i: (jnp.minimum(i, last), 0))
        in_specs.append(block(w.shape[1]))
        args.append(w)
        out_specs += [block(width) for width in widths]
        out_shapes += [jax.ShapeDtypeStruct((w.shape[0], width), BF16) for width in widths]
    outs = pl.pallas_call(
        functools.partial(_ffn_kernel, final_norm=final_g is not None, n_cast=len(cast)),
        grid=(n_steps,),
        in_specs=in_specs,
        out_specs=out_specs,
        out_shape=out_shapes,
        compiler_params=_params(1),
        name="ffn_final" if final_g is not None else "ffn",
    )(*args)
    return outs[0], outs[1:]


def _rows_by_residue(slab_ref, j, stage_ref, slot, n_rows, d):
    if d == 1:
        return [slab_ref[j]]
    if d == SPLIT_STRIDE:
        return [slab_ref[j, pl.ds(r, n_rows // d, stride=d), :] for r in range(d)]
    assert d == SPLIT_STRIDE * SPLIT_STRIDE
    for low in range(SPLIT_STRIDE):
        stage_ref[slot, low] = slab_ref[j, pl.ds(low, n_rows // SPLIT_STRIDE, stride=SPLIT_STRIDE), :]
    return [stage_ref[slot, r % SPLIT_STRIDE, pl.ds(r // SPLIT_STRIDE, n_rows // d, stride=SPLIT_STRIDE), :]
            for r in range(d)]


def _proj_kernel(x_ref, g_ref, w_ref, u_ref, *rest):
    qkv_refs, slabs = rest[:3 * N_PATTERNS], rest[3 * N_PATTERNS:]
    n_rows = TM_PROJ // PROJ_HALVES
    per_half = len(slabs) // PROJ_HALVES
    for half in range(PROJ_HALVES):
        base = half * n_rows
        su_ref, sq_ref, sk_ref, sv_ref, stage_ref = slabs[per_half * half:per_half * (half + 1)]
        un = _rmsnorm(x_ref[base:base + n_rows, :], g_ref[...]).astype(BF16)
        full = _dot(un, w_ref[...])
        slot = 0

        u = full[:, :SSM_WIDTH]
        chunks = slice(base // SSM_CHUNK, (base + n_rows) // SSM_CHUNK)
        for o in range(SSM_N_OCT):
            su_ref[o] = u[:, o * LANES:(o + 1) * LANES]
            for t, rows in enumerate(_rows_by_residue(su_ref, o, stage_ref, slot, n_rows, SSM_CHUNK)):
                lo = o * SSM_OCT_IN + t * LANES
                u_ref[chunks, lo:lo + LANES] = rows.astype(BF16)
            slot += 1

        for which, scr in enumerate((sq_ref, sk_ref, sv_ref)):
            lo = SSM_WIDTH + which * ATTN_WIDTH
            res = full[:, lo:lo + ATTN_WIDTH]
            if which == 0:
                res = res * (HEAD_DIM ** -0.5)
            for j in range(ATTN_WIDTH // LANES):
                scr[j] = res[:, j * LANES:(j + 1) * LANES]
            for g, (_, d) in enumerate(ATTN_PATTERNS):
                out = qkv_refs[which * N_PATTERNS + g]
                for jj in range(PATTERN_WIDTH // LANES):
                    j = g * (PATTERN_WIDTH // LANES) + jj
                    for r, rows in enumerate(_rows_by_residue(scr, j, stage_ref, slot, n_rows, d)):
                        out[0, r, base // d:(base + n_rows) // d, jj * LANES:(jj + 1) * LANES] = rows.astype(BF16)
                    slot += d == SPLIT_STRIDE * SPLIT_STRIDE


def _proj(x, norm_g, w_pre, bsz, seqlen):
    n = x.shape[0]
    tiles_per_seq = seqlen // TM_PROJ
    rows = lambda w: pl.BlockSpec((TM_PROJ, w), lambda i: (i, 0))
    qkv_specs, qkv_shapes = [], []
    for _ in range(3):
        for _, d in ATTN_PATTERNS:
            qkv_specs.append(pl.BlockSpec((1, d, TM_PROJ // d, PATTERN_WIDTH),
                                          lambda i: (i // tiles_per_seq, 0, i % tiles_per_seq, 0)))
            qkv_shapes.append(jax.ShapeDtypeStruct((bsz, d, seqlen // d, PATTERN_WIDTH), BF16))
    half_rows = TM_PROJ // PROJ_HALVES
    slabs = lambda k: pltpu.VMEM((k, half_rows, LANES), F32)
    n_staged = SSM_N_OCT + 3 * PATTERN_WIDTH // LANES
    stage = pltpu.VMEM((n_staged, SPLIT_STRIDE, half_rows // SPLIT_STRIDE, LANES), F32)
    outs = pl.pallas_call(
        _proj_kernel,
        grid=(n // TM_PROJ,),
        in_specs=[rows(D_MODEL), _resident((1, D_MODEL)), _resident((D_MODEL, PRE_WIDTH))],
        out_specs=[pl.BlockSpec((TM_PROJ // SSM_CHUNK, SSM_ROW), lambda i: (i, 0))] + qkv_specs,
        out_shape=[jax.ShapeDtypeStruct((n // SSM_CHUNK, SSM_ROW), BF16)] + qkv_shapes,
        scratch_shapes=([slabs(SSM_N_OCT)] + [slabs(ATTN_WIDTH // LANES)] * 3 + [stage]) * PROJ_HALVES,
        compiler_params=_params(1),
        name="proj",
    )(x, norm_g.reshape(1, D_MODEL), w_pre)
    u = outs[0]
    q, k, v = (outs[1 + i * N_PATTERNS:1 + (i + 1) * N_PATTERNS] for i in range(3))
    return u, q, k, v


def _ssm_compact(a_re, a_im, log_dt, b_re, b_im, c_re, c_im, d_skip):
    t_len = SSM_CHUNK
    dt = jnp.exp(log_dt)[:, None]
    mag = jnp.exp(a_re * dt)
    ang = a_im * dt
    abar_re = mag * jnp.cos(ang)
    abar_im = mag * jnp.sin(ang)
    nr = abar_re - 1.0
    ni = abar_im
    den = a_re * a_re + a_im * a_im
    coef_re = (nr * a_re + ni * a_im) / den
    coef_im = (ni * a_re - nr * a_im) / den
    bbar_re = coef_re[..., None] * b_re - coef_im[..., None] * b_im
    bbar_im = coef_re[..., None] * b_im + coef_im[..., None] * b_re
    bt_re = jnp.swapaxes(bbar_re, -1, -2)
    bt_im = jnp.swapaxes(bbar_im, -1, -2)
    pw_re = [jnp.ones_like(abar_re)]
    pw_im = [jnp.zeros_like(abar_re)]
    for _ in range(t_len):
        pr, pi = pw_re[-1], pw_im[-1]
        pw_re.append(pr * abar_re - pi * abar_im)
        pw_im.append(pr * abar_im + pi * abar_re)
    rev_re = jnp.stack(pw_re[t_len - 1::-1])[:, :, None, :]
    rev_im = jnp.stack(pw_im[t_len - 1::-1])[:, :, None, :]
    pw_re = jnp.stack(pw_re)
    pw_im = jnp.stack(pw_im)
    n_oct, oct_, grp, st = SSM_N_OCT, SSM_OCT, SSM_GROUP, SSM_STATE

    def per_octet(re, im):
        both = jnp.concatenate([re, im], axis=-1)
        n = both.shape[0]
        return both.reshape(n, n_oct, oct_ * grp, 2 * st).transpose(1, 0, 2, 3).reshape(n_oct, n * LANES, 2 * st)

    into_c = per_octet(rev_re * bt_re[None] - rev_im * bt_im[None], rev_re * bt_im[None] + rev_im * bt_re[None])
    cp_re = c_re[None] * pw_re[:, :, None, :] - c_im[None] * pw_im[:, :, None, :]
    cp_im = c_re[None] * pw_im[:, :, None, :] + c_im[None] * pw_re[:, :, None, :]
    outof_c = per_octet(cp_re, -cp_im)
    decay = jnp.stack([pw_re[t_len].reshape(n_oct, SSM_OCT_STATE), pw_im[t_len].reshape(n_oct, SSM_OCT_STATE)]
                      + [jnp.zeros((n_oct, SSM_OCT_STATE), F32)] * 6, axis=1)
    skip = d_skip.reshape(n_oct, 1, LANES)
    return into_c.astype(BF16), outof_c.astype(BF16), decay, skip


def _ssm_expanders():
    row = np.arange(LANES)
    col = np.arange(2 * SSM_OCT_STATE)
    spread = (row[:, None] // SSM_STATE == col[None, :] // SSM_OCT_STATE) & (
        row[:, None] % SSM_STATE == col[None, :] % SSM_STATE)
    keep = row[:, None] // SSM_GROUP == (col[None, :] % SSM_OCT_STATE) // SSM_STATE
    return jnp.asarray(spread, BF16), jnp.asarray(keep, F32)


def _ssm_kernel(x_ref, intoc_ref, outofc_ref, decay_ref, skip_ref, spread_ref, keep_ref, y_ref,
                toep_ref, into_ref, outof_ref, loc_ref, prev_ref, part_ref, *, n_chunks, n_seq):
    blk = lambda i: slice(i * LANES, (i + 1) * LANES)

    @pl.when(pl.program_id(1) == 0)
    def _build_octet_matrices():
        spread, keep = spread_ref[...], keep_ref[...]
        for s in range(SSM_CHUNK):
            into_ref[blk(s), :] = (_dot(intoc_ref[0, blk(s), :], spread) * keep).astype(BF16)
        for j in range(SSM_CHUNK + 1):
            outof_ref[blk(j), :] = (_dot(outofc_ref[0, blk(j), :], spread) * keep).astype(BF16)
        bbar = into_ref[blk(SSM_CHUNK - 1), :]
        diag = (lax.broadcasted_iota(jnp.int32, (LANES, LANES), 0)
                == lax.broadcasted_iota(jnp.int32, (LANES, LANES), 1))
        zeros = jnp.zeros((LANES, LANES), BF16)
        for j in range(SSM_CHUNK):
            lag = _dot_nt(bbar, outof_ref[blk(j), :])
            if j == 0:
                lag = lag + jnp.where(diag, skip_ref[0], 0.0)
            lag = lag.astype(BF16)
            for s in range(SSM_CHUNK - j):
                toep_ref[blk(s), blk(s + j)] = lag
            if j > 0:
                for s in range(j, SSM_CHUNK):
                    toep_ref[blk(s), blk(s - j)] = zeros

    loc_ref[...] = _dot(x_ref[...], into_ref[...])
    n_tiles = SSM_OCT_IN // MXU_TILE
    early = n_tiles - part_ref.shape[1] // MXU_TILE
    in_chunk = lambda j: _dot(x_ref[:, :(j + 1) * MXU_TILE], toep_ref[:(j + 1) * MXU_TILE, j * MXU_TILE:(j + 1) * MXU_TILE])
    for j in range(early, n_tiles):
        part_ref[:, (j - early) * MXU_TILE:(j - early + 1) * MXU_TILE] = in_chunk(j)
    ar = decay_ref[0, 0:1, :]
    ai = decay_ref[0, 1:2, :]

    def step(c, carry):
        new = []
        for q in range(n_seq):
            s_re, s_im = carry[2 * q], carry[2 * q + 1]
            row = pl.ds(q * n_chunks + c, 1)
            prev_ref[row, :SSM_OCT_STATE] = s_re
            prev_ref[row, SSM_OCT_STATE:] = s_im
            new.append(ar * s_re - ai * s_im + loc_ref[row, :SSM_OCT_STATE])
            new.append(ar * s_im + ai * s_re + loc_ref[row, SSM_OCT_STATE:])
        return tuple(new)

    zero = jnp.zeros((1, SSM_OCT_STATE), F32)
    lax.fori_loop(0, n_chunks, step, (zero,) * (2 * n_seq), unroll=True)
    prev = prev_ref[...].astype(BF16)
    for j in range(n_tiles):
        lo, hi = j * MXU_TILE, (j + 1) * MXU_TILE
        within = in_chunk(j) if j < early else part_ref[:, (j - early) * MXU_TILE:(j - early + 1) * MXU_TILE]
        y = within + _dot_nt(prev, outof_ref[lo + LANES:hi + LANES, :])
        y_ref[:, lo:hi] = jax.nn.gelu(y).astype(BF16)


def _ssm(u_rows, compact, bsz):
    into_c, outof_c, decay, skip = compact
    spread, keep = _ssm_expanders()
    n_chunks = u_rows.shape[0] // bsz
    n_seq = SSM_SEQ_PER_STEP
    per_oct = lambda *shape: pl.BlockSpec((1,) + shape, lambda o, b: (o,) + (0,) * len(shape))
    rows = pl.BlockSpec((n_seq * n_chunks, SSM_OCT_IN), lambda o, b: (b, o))
    wide = 2 * SSM_OCT_STATE
    return pl.pallas_call(
        functools.partial(_ssm_kernel, n_chunks=n_chunks, n_seq=n_seq),
        grid=(SSM_N_OCT, bsz // n_seq),
        in_specs=[rows, per_oct(SSM_OCT_IN, LANES), per_oct(SSM_OCT_IN + LANES, LANES),
                  per_oct(8, SSM_OCT_STATE), per_oct(1, LANES), _resident((LANES, wide)), _resident((LANES, wide))],
        out_specs=rows,
        out_shape=jax.ShapeDtypeStruct(u_rows.shape, BF16),
        scratch_shapes=[pltpu.VMEM((SSM_OCT_IN, SSM_OCT_IN), BF16),
                        pltpu.VMEM((SSM_OCT_IN, wide), BF16),
                        pltpu.VMEM((SSM_OCT_IN + LANES, wide), BF16),
                        pltpu.VMEM((n_seq * n_chunks, wide), F32),
                        pltpu.VMEM((n_seq * n_chunks, wide), F32),
                        pltpu.VMEM((n_seq * n_chunks, SSM_OCT_IN // 2), F32)],
        compiler_params=_params(2),
        name="ssm",
    )(u_rows, into_c, outof_c, decay, skip, spread, keep)


def _t5_buckets(dist):
    dist = np.asarray(dist, np.int32)
    max_exact = REL_BUCKETS // 2
    safe = np.maximum(dist, 1).astype(np.float32)
    large = max_exact + (np.log(safe / max_exact) / np.log(REL_MAX_DIST / max_exact)
                         * (REL_BUCKETS - max_exact)).astype(np.int32)
    large = np.minimum(large, REL_BUCKETS - 1)
    return np.where(dist < max_exact, dist, large).astype(np.int32)


def _band_bias(rel_table, pattern):
    dilation = ATTN_PATTERNS[pattern][1]
    buckets = _t5_buckets(np.arange(WINDOW_KEYS + 1) * dilation)
    table = rel_table[:, pattern * HEADS_PER_PATTERN:(pattern + 1) * HEADS_PER_PATTERN].T
    runs = []
    for b in np.unique(buckets):
        runs.append(jnp.broadcast_to(table[:, b:b + 1], (HEADS_PER_PATTERN, int(np.sum(buckets == b)))))
    per_step = jnp.concatenate(runs, axis=1)
    period = 3 * SUB_Q
    masked = lambda n: jnp.full((HEADS_PER_PATTERN, n), NEG_BIG, F32)
    w = jnp.concatenate([masked(SUB_Q - 1), per_step[:, ::-1], masked(period - SUB_Q - WINDOW_KEYS)], axis=1)
    skew = jnp.tile(w, (1, SUB_Q))[:, :SUB_Q * (period - 1)].reshape(HEADS_PER_PATTERN, SUB_Q, period - 1)
    bias = skew[:, :, SUB_Q - 1:3 * SUB_Q - 1]
    first = jnp.where((np.arange(2 * SUB_Q) >= SUB_Q)[None, None, :], bias, NEG_BIG)
    pairs = lambda b: b.reshape(HEADS_PER_PATTERN // 2, 2 * SUB_Q, 2 * SUB_Q)
    return jnp.stack([pairs(bias), pairs(first)])


def _attn_kernel(q_ref, kc_ref, kh_ref, vc_ref, vh_ref, bias_ref, o_ref):
    n_res, tq = q_ref.shape[1], q_ref.shape[2]
    first = jnp.where(pl.program_id(2) == 0, 1, 0)
    lane = lax.broadcasted_iota(jnp.int32, (SUB_Q, 2 * HEAD_DIM), 1)
    low_half = lane < HEAD_DIM
    stat_is_max = lax.broadcasted_iota(jnp.int32, (2 * SUB_Q, 2 * HEAD_DIM), 1) % HEAD_DIM < HEAD_DIM // 2
    for r in range(n_res):
        for j in range(tq // SUB_Q):
            rows = slice(j * SUB_Q, (j + 1) * SUB_Q)
            q = q_ref[0, r, rows, :]
            if j == 0:
                k = jnp.concatenate([kh_ref[0, r], kc_ref[0, r, rows, :]], axis=0)
                v = jnp.concatenate([vh_ref[0, r], vc_ref[0, r, rows, :]], axis=0)
            else:
                k = kc_ref[0, r, (j - 1) * SUB_Q:(j + 1) * SUB_Q, :]
                v = vc_ref[0, r, (j - 1) * SUB_Q:(j + 1) * SUB_Q, :]
            for pair in range(HEADS_PER_PATTERN // 2):
                cols = slice(pair * 2 * HEAD_DIM, (pair + 1) * 2 * HEAD_DIM)
                q2, k2, v2 = q[:, cols], k[:, cols], v[:, cols]
                zero = jnp.zeros_like(q2)
                qq = jnp.concatenate([jnp.where(low_half, q2, zero), jnp.where(low_half, zero, q2)], axis=0)
                s = _dot_nt(qq, k2) + (bias_ref[first, pair] if j == 0 else bias_ref[0, pair])
                m = jnp.max(s, axis=-1, keepdims=True)
                p = jnp.exp(s - m)
                l = jnp.sum(p, axis=-1, keepdims=True)
                o = _dot(p.astype(BF16), v2)
                wide = (2 * SUB_Q, 2 * HEAD_DIM)
                stat = jnp.where(stat_is_max, jnp.broadcast_to(m, wide), jnp.broadcast_to(l, wide))
                o_ref[0, r, rows, cols] = jnp.where(low_half, o[:SUB_Q], o[SUB_Q:])
                o_ref[0, r, rows, PATTERN_WIDTH + pair * 2 * HEAD_DIM:PATTERN_WIDTH + (pair + 1) * 2 * HEAD_DIM] = (
                    jnp.where(low_half, stat[:SUB_Q], stat[SUB_Q:]))


def _dilated_attention(q, k, v, rel_table, pattern):
    bsz, d, res_len, _ = q.shape
    tq = min(res_len, ATTN_STEP_ROWS)
    n_res = ATTN_STEP_ROWS // tq
    cur = pl.BlockSpec((1, n_res, tq, PATTERN_WIDTH), lambda b, r, i: (b, r, i, 0))
    halo = pl.BlockSpec((1, n_res, SUB_Q, PATTERN_WIDTH),
                        lambda b, r, i: (b, r, jnp.maximum(i * (tq // SUB_Q) - 1, 0), 0))
    return pl.pallas_call(
        _attn_kernel,
        grid=(bsz, d // n_res, res_len // tq),
        in_specs=[cur, cur, halo, cur, halo, _resident((2, HEADS_PER_PATTERN // 2, 2 * SUB_Q, 2 * SUB_Q))],
        out_specs=pl.BlockSpec((1, n_res, tq, 2 * PATTERN_WIDTH), lambda b, r, i: (b, r, i, 0)),
        out_shape=jax.ShapeDtypeStruct((bsz, d, res_len, 2 * PATTERN_WIDTH), F32),
        compiler_params=_params(3),
        name=f"attn_d{d}",
    )(q, k, k, v, v, _band_bias(rel_table, pattern))


def _memkv_kernel(mem_ref, g_ref, w_ref, k_ref, v_ref):
    mn = _rmsnorm(mem_ref[...], g_ref[...]).astype(BF16)
    k_ref[...] = _dot(mn, w_ref[:, :XATTN_WIDTH]).astype(BF16)
    v_ref[...] = _dot(mn, w_ref[:, XATTN_WIDTH:]).astype(BF16)


def _memkv(mem2d, norm_g, w_kv):
    rows = mem2d.shape[0]
    full = lambda *shape: pl.BlockSpec(shape, lambda i: (0,) * len(shape))
    return pl.pallas_call(
        _memkv_kernel,
        grid=(1,),
        in_specs=[full(rows, D_MODEL), full(1, D_MODEL), full(D_MODEL, 2 * XATTN_WIDTH)],
        out_specs=[full(rows, XATTN_WIDTH), full(rows, XATTN_WIDTH)],
        out_shape=[jax.ShapeDtypeStruct((rows, XATTN_WIDTH), BF16)] * 2,
        compiler_params=_params(1),
        name="memkv",
    )(mem2d, norm_g.reshape(1, D_MODEL), w_kv)


def _merge_kernel(x_ref, y_ref, o0_ref, o1_ref, o2_ref, mk_ref, mv_ref, g_ref, wpost_ref, wglu_ref,
                  wau_ref, wxu_ref, wout_ref, out_ref, ys_ref, s1_ref, s2_ref):
    x = x_ref[...]
    un = _rmsnorm(x, g_ref[...]).astype(BF16)

    def gate(index):
        lo = XATTN_WIDTH + index * D_MODEL
        return _sigmoid(_dot(un, wpost_ref[:, lo:lo + D_MODEL]))

    slabs = 2 * PATTERN_WIDTH // LANES
    for o_ref, s_ref in ((o1_ref, s1_ref), (o2_ref, s2_ref)):
        d = o_ref.shape[1]
        for r in range(d):
            for j in range(slabs):
                s_ref[j, pl.ds(r, TM_MERGE // d, stride=d), :] = o_ref[0, r, :, j * LANES:(j + 1) * LANES]
    half = slabs // 2
    is_max = lax.broadcasted_iota(jnp.int32, (TM_MERGE, LANES), 1) % HEAD_DIM < HEAD_DIM // 2

    def unpack(stat):
        return (jnp.where(is_max, stat, pltpu.roll(stat, HEAD_DIM // 2, axis=1)),
                jnp.where(is_max, pltpu.roll(stat, LANES - HEAD_DIM // 2, axis=1), stat))

    def pattern(read):
        out = jnp.concatenate([read(j) for j in range(half)], axis=-1)
        stats = [unpack(read(j)) for j in range(half, slabs)]
        return (out, jnp.concatenate([m for m, _ in stats], axis=-1),
                jnp.concatenate([l for _, l in stats], axis=-1))

    parts = [pattern(lambda j: o0_ref[0, 0, :, j * LANES:(j + 1) * LANES]),
             pattern(lambda j: s1_ref[j]), pattern(lambda j: s2_ref[j])]
    top = jnp.maximum(jnp.maximum(parts[0][1], parts[1][1]), parts[2][1])
    ws = [jnp.exp(m - top) for _, m, _ in parts]
    att = ((ws[0] * parts[0][0] + ws[1] * parts[1][0] + ws[2] * parts[2][0])
           / (ws[0] * parts[0][2] + ws[1] * parts[1][2] + ws[2] * parts[2][2]))
    att = att.astype(BF16)

    chunk_rows = TM_MERGE // SSM_CHUNK
    for o in range(SSM_N_OCT):
        for t in range(SSM_CHUNK):
            lo = o * SSM_OCT_IN + t * LANES
            ys_ref[o, pl.ds(t, chunk_rows, stride=SSM_CHUNK), :] = y_ref[:, lo:lo + LANES].astype(F32)
    ys = jnp.concatenate([ys_ref[o] for o in range(SSM_N_OCT)], axis=-1).astype(BF16)
    glu = _dot(ys, wglu_ref[:, :D_MODEL]) * _sigmoid(_dot(ys, wglu_ref[:, D_MODEL:]))
    merged = gate(0) * glu

    xq = _dot(un, wpost_ref[:, :XATTN_WIDTH]).astype(BF16)
    heads = []
    for h in range(XATTN_HEADS):
        cols = slice(h * XATTN_HEAD_DIM, (h + 1) * XATTN_HEAD_DIM)
        s = _dot_nt(xq[:, cols], mk_ref[0, :, cols]) * (XATTN_HEAD_DIM ** -0.5)
        p = jnp.exp(s - jnp.max(s, axis=-1, keepdims=True))
        heads.append(_dot(p.astype(BF16), mv_ref[0, :, cols]) / jnp.sum(p, axis=-1, keepdims=True))
    xo = jnp.concatenate(heads, axis=-1).astype(BF16)
    merged = merged + gate(2) * _dot(xo, wxu_ref[...])

    merged = merged + gate(1) * _dot(att, wau_ref[...])

    out_ref[...] = x + _dot(merged.astype(BF16), wout_ref[...])


def _merge(x, y_rows, attn_outs, mk, mv, norm_g, w_post, w_glu, w_au, w_xu, w_out, seqlen):
    n = x.shape[0]
    tiles_per_seq = seqlen // TM_MERGE
    rows = lambda w: pl.BlockSpec((TM_MERGE, w), lambda i: (i, 0))
    mem = pl.BlockSpec((1, MEM_LEN, XATTN_WIDTH), lambda i: (i // tiles_per_seq, 0, 0))
    attn_specs = [pl.BlockSpec((1, d, TM_MERGE // d, 2 * PATTERN_WIDTH),
                               lambda i: (i // tiles_per_seq, 0, i % tiles_per_seq, 0))
                  for _, d in ATTN_PATTERNS]
    slabs = lambda k: pltpu.VMEM((k, TM_MERGE, LANES), F32)
    return pl.pallas_call(
        _merge_kernel,
        grid=(n // TM_MERGE,),
        in_specs=[rows(D_MODEL), pl.BlockSpec((TM_MERGE // SSM_CHUNK, SSM_ROW), lambda i: (i, 0))] + attn_specs
                 + [mem, mem, _resident((1, D_MODEL)), _resident((D_MODEL, POST_WIDTH)),
                    _resident((SSM_WIDTH, 2 * D_MODEL)), _resident((PATTERN_WIDTH, D_MODEL)),
                    _resident((XATTN_WIDTH, D_MODEL)), _resident((D_MODEL, D_MODEL))],
        out_specs=rows(D_MODEL),
        out_shape=jax.ShapeDtypeStruct((n, D_MODEL), F32),
        scratch_shapes=[slabs(SSM_N_OCT), slabs(2 * PATTERN_WIDTH // LANES), slabs(2 * PATTERN_WIDTH // LANES)],
        compiler_params=_params(1),
        name="merge",
    )(x, y_rows, *attn_outs, mk, mv, norm_g.reshape(1, D_MODEL), w_post, w_glu, w_au, w_xu, w_out)


def _layer(x, mem, rel_table, ffn1_norm, ffn1_w_in, ffn1_w_down, mix_norm, w_in,
           ssm_a_re, ssm_a_im, ssm_log_dt, ssm_b_re, ssm_b_im, ssm_c_re, ssm_c_im, ssm_d, ssm_w_glu,
           attn_w_up, mem_norm, xattn_w_kv, xattn_w_up, w_out, ffn2_norm, ffn2_w_in, ffn2_w_down,
           final_norm, bsz, seqlen):
    cast = [(w_in, (PRE_WIDTH, POST_WIDTH)), (ssm_w_glu, (2 * D_MODEL,)), (attn_w_up, (D_MODEL,)),
            (xattn_w_kv, (2 * XATTN_WIDTH,)), (xattn_w_up, (D_MODEL,)), (w_out, (D_MODEL,)),
            (ffn2_w_in, (2 * D_FF,)), (ffn2_w_down, (D_MODEL,))]
    x, (w_pre, w_post, w_glu, w_au, w_kv, w_xu, w_o, w2_in, w2_down) = _ffn(
        x, ffn1_norm, ffn1_w_in.astype(BF16), ffn1_w_down.astype(BF16), cast=cast)
    u_rows, q, k, v = _proj(x, mix_norm, w_pre, bsz, seqlen)
    compact = _ssm_compact(ssm_a_re, ssm_a_im, ssm_log_dt, ssm_b_re, ssm_b_im, ssm_c_re, ssm_c_im, ssm_d)
    y_rows = _ssm(u_rows, compact, bsz)
    attn_outs = [_dilated_attention(q[g], k[g], v[g], rel_table, g) for g in range(N_PATTERNS)]
    mk, mv = _memkv(mem.reshape(bsz * MEM_LEN, D_MODEL), mem_norm, w_kv)
    mk = mk.reshape(bsz, MEM_LEN, XATTN_WIDTH)
    mv = mv.reshape(bsz, MEM_LEN, XATTN_WIDTH)
    x = _merge(x, y_rows, attn_outs, mk, mv, mix_norm, w_post, w_glu, w_au, w_xu, w_o, seqlen)
    return _ffn(x, ffn2_norm, w2_in, w2_down, final_g=final_norm)[0]


def kernel(x, mem, ffn1_norm, ffn1_w_in, ffn1_w_down, mix_norm, w_in, ssm_a_re, ssm_a_im, ssm_log_dt,
           ssm_b_re, ssm_b_im, ssm_c_re, ssm_c_im, ssm_d, ssm_w_glu, rel_table, attn_w_up, mem_norm,
           xattn_w_kv, xattn_w_up, w_out, ffn2_norm, ffn2_w_in, ffn2_w_down, final_norm):
    bsz, seqlen, _ = x.shape
    assert ffn1_norm.shape[0] == 1, "single-layer trunk"
    assert x.shape[2] == D_MODEL and mem.shape == (bsz, MEM_LEN, D_MODEL)
    assert w_in.shape[1:] == (D_MODEL, PRE_WIDTH + POST_WIDTH) and ffn1_w_in.shape[1:] == (D_MODEL, 2 * D_FF)
    assert ssm_b_re.shape[1:] == (SSM_WIDTH // SSM_GROUP, SSM_STATE, SSM_GROUP)
    assert rel_table.shape == (REL_BUCKETS, N_PATTERNS * HEADS_PER_PATTERN)
    assert seqlen % ATTN_STEP_ROWS == 0 and (seqlen // ATTN_PATTERNS[-1][1]) % SUB_Q == 0
    assert seqlen % TM_PROJ == 0 and seqlen % TM_MERGE == 0 and (bsz * seqlen) % TM_FFN == 0
    assert bsz % SSM_SEQ_PER_STEP == 0
    h = _layer(x.reshape(bsz * seqlen, D_MODEL), mem, rel_table, ffn1_norm[0], ffn1_w_in[0], ffn1_w_down[0],
               mix_norm[0], w_in[0], ssm_a_re[0], ssm_a_im[0], ssm_log_dt[0], ssm_b_re[0], ssm_b_im[0],
               ssm_c_re[0], ssm_c_im[0], ssm_d[0], ssm_w_glu[0], attn_w_up[0], mem_norm[0], xattn_w_kv[0],
               xattn_w_up[0], w_out[0], ffn2_norm[0], ffn2_w_in[0], ffn2_w_down[0], final_norm,
               bsz, seqlen)
    return h.reshape(bsz, seqlen, D_MODEL)
```

```python
import functools

import numpy as np
import jax
import jax.numpy as jnp
from jax import lax
from jax.experimental import pallas as pl
from jax.experimental.pallas import tpu as pltpu

F32 = jnp.float32
BF16 = jnp.bfloat16

LANES = 128
BF16_SUBLANES = 16
MXU_TILE = 256

D_MODEL = 1024
D_FF = 2816
EPS = 1e-6

HEAD_DIM = 64
ATTN_PATTERNS = ((128, 1), (512, 4), (2048, 16))
N_PATTERNS = 3
HEADS_PER_PATTERN = 4
ATTN_WIDTH = N_PATTERNS * HEADS_PER_PATTERN * HEAD_DIM
PATTERN_WIDTH = HEADS_PER_PATTERN * HEAD_DIM
WINDOW_KEYS = 128
REL_BUCKETS = 32
REL_MAX_DIST = 2048

SSM_GROUP = 16
SSM_STATE = 64
SSM_WIDTH = 512
SSM_CHUNK = 16
SSM_OCT = LANES // SSM_GROUP
SSM_N_OCT = SSM_WIDTH // LANES
SSM_OCT_IN = SSM_CHUNK * LANES
SSM_OCT_STATE = SSM_OCT * SSM_STATE
SSM_ROW = SSM_CHUNK * SSM_WIDTH

MEM_LEN = 256
XATTN_HEADS = 4
XATTN_HEAD_DIM = 128
XATTN_WIDTH = XATTN_HEADS * XATTN_HEAD_DIM

PRE_WIDTH = SSM_WIDTH + 3 * ATTN_WIDTH
POST_WIDTH = XATTN_WIDTH + 3 * D_MODEL

NEG_BIG = -1e30

VMEM_LIMIT_BYTES = 56 * 1024 * 1024

TM_FFN = 1024
FFN_ROW_PARTS = 4
FFN_CHUNKS = ((0, 1536), (1536, D_FF))
TM_PROJ = 1024
PROJ_HALVES = 2
SPLIT_STRIDE = 4
TM_MERGE = 512
ATTN_STEP_ROWS = 4096
SSM_SEQ_PER_STEP = 2
SUB_Q = 128


def _rmsnorm(x, g):
    return x * lax.rsqrt(jnp.mean(x * x, axis=-1, keepdims=True) + EPS) * g


def _sigmoid(x):
    return 0.5 * jnp.tanh(0.5 * x) + 0.5


def _dot(a, b):
    return jnp.dot(a, b, preferred_element_type=F32)


def _dot_nt(a, b):
    return lax.dot_general(a, b, (((1,), (1,)), ((), ())), preferred_element_type=F32)


def _resident(shape):
    nd = len(shape)
    return pl.BlockSpec(shape, lambda *_: (0,) * nd, pipeline_mode=pl.Buffered(1))


def _params(n_axes):
    return pltpu.CompilerParams(dimension_semantics=("arbitrary",) * n_axes,
                                vmem_limit_bytes=VMEM_LIMIT_BYTES)


def _ffn_kernel(x_ref, g_ref, win_ref, wdn_ref, *rest, final_norm, n_cast):
    rest = list(rest)
    fg_ref = rest.pop(0) if final_norm else None
    cast_src, o_ref, cast_dst = rest[:n_cast], rest[n_cast], rest[n_cast + 1:]
    for part in range(FFN_ROW_PARTS):
        rows = slice(part * TM_FFN // FFN_ROW_PARTS, (part + 1) * TM_FFN // FFN_ROW_PARTS)
        x = x_ref[rows, :]
        un = _rmsnorm(x, g_ref[...]).astype(BF16)
        acc = None
        for lo, hi in FFN_CHUNKS:
            a = _dot(un, win_ref[:, lo:hi])
            b = _dot(un, win_ref[:, D_FF + lo:D_FF + hi])
            h = (a * _sigmoid(a) * b).astype(BF16)
            part = _dot(h, wdn_ref[lo:hi, :])
            acc = part if acc is None else acc + part
        y = x + 0.5 * acc
        if final_norm:
            y = _rmsnorm(y, fg_ref[...])
        o_ref[rows, :] = y

    dst = iter(cast_dst)
    for src in cast_src:
        lo = 0
        while lo < src.shape[1]:
            out = next(dst)
            out[...] = src[:, lo:lo + out.shape[1]].astype(BF16)
            lo += out.shape[1]


def _cast_rows(n_rows, n_steps):
    rows = BF16_SUBLANES
    while n_rows % rows or n_rows // rows > n_steps:
        rows += BF16_SUBLANES
    return rows


def _ffn(x, norm_g, w_in, w_down, final_g=None, cast=()):
    n = x.shape[0]
    n_steps = n // TM_FFN
    row = pl.BlockSpec((TM_FFN, D_MODEL), lambda i: (i, 0))
    in_specs = [row, _resident((1, D_MODEL)), _resident((D_MODEL, 2 * D_FF)), _resident((D_FF, D_MODEL))]
    args = [x, norm_g.reshape(1, D_MODEL), w_in, w_down]
    if final_g is not None:
        in_specs.append(_resident((1, D_MODEL)))
        args.append(final_g.reshape(1, D_MODEL))
    out_specs, out_shapes = [row], [jax.ShapeDtypeStruct((n, D_MODEL), F32)]
    for w, widths in cast:
        assert sum(widths) == w.shape[1]
        rows = _cast_rows(w.shape[0], n_steps)
        last = w.shape[0] // rows - 1
        block = lambda width, last=last, rows=rows: pl.BlockSpec((rows, width), lambda i: (jnp.minimum(i, last), 0))
        in_specs.append(block(w.shape[1]))
        args.append(w)
        out_specs += [block(width) for width in widths]
        out_shapes += [jax.ShapeDtypeStruct((w.shape[0], width), BF16) for width in widths]
    outs = pl.pallas_call(
        functools.partial(_ffn_kernel, final_norm=final_g is not None, n_cast=len(cast)),
        grid=(n_steps,),
        in_specs=in_specs,
        out_specs=out_specs,
        out_shape=out_shapes,
        compiler_params=_params(1),
        name="ffn_final" if final_g is not None else "ffn",
    )(*args)
    return outs[0], outs[1:]


def _rows_by_residue(slab_ref, j, stage_ref, slot, n_rows, d):
    if d == 1:
        return [slab_ref[j]]
    if d == SPLIT_STRIDE:
        return [slab_ref[j, pl.ds(r, n_rows // d, stride=d), :] for r in range(d)]
    assert d == SPLIT_STRIDE * SPLIT_STRIDE
    for low in range(SPLIT_STRIDE):
        stage_ref[slot, low] = slab_ref[j, pl.ds(low, n_rows // SPLIT_STRIDE, stride=SPLIT_STRIDE), :]
    return [stage_ref[slot, r % SPLIT_STRIDE, pl.ds(r // SPLIT_STRIDE, n_rows // d, stride=SPLIT_STRIDE), :]
            for r in range(d)]


def _proj_kernel(x_ref, g_ref, w_ref, u_ref, *rest):
    qkv_refs, slabs = rest[:3 * N_PATTERNS], rest[3 * N_PATTERNS:]
    n_rows = TM_PROJ // PROJ_HALVES
    per_half = len(slabs) // PROJ_HALVES
    for half in range(PROJ_HALVES):
        base = half * n_rows
        su_ref, sq_ref, sk_ref, sv_ref, stage_ref = slabs[per_half * half:per_half * (half + 1)]
        un = _rmsnorm(x_ref[base:base + n_rows, :], g_ref[...]).astype(BF16)
        full = _dot(un, w_ref[...])
        slot = 0

        u = full[:, :SSM_WIDTH]
        chunks = slice(base // SSM_CHUNK, (base + n_rows) // SSM_CHUNK)
        for o in range(SSM_N_OCT):
            su_ref[o] = u[:, o * LANES:(o + 1) * LANES]
            for t, rows in enumerate(_rows_by_residue(su_ref, o, stage_ref, slot, n_rows, SSM_CHUNK)):
                lo = o * SSM_OCT_IN + t * LANES
                u_ref[chunks, lo:lo + LANES] = rows.astype(BF16)
            slot += 1

        for which, scr in enumerate((sq_ref, sk_ref, sv_ref)):
            lo = SSM_WIDTH + which * ATTN_WIDTH
            res = full[:, lo:lo + ATTN_WIDTH]
            if which == 0:
                res = res * (HEAD_DIM ** -0.5)
            for j in range(ATTN_WIDTH // LANES):
                scr[j] = res[:, j * LANES:(j + 1) * LANES]
            for g, (_, d) in enumerate(ATTN_PATTERNS):
                out = qkv_refs[which * N_PATTERNS + g]
                for jj in range(PATTERN_WIDTH // LANES):
                    j = g * (PATTERN_WIDTH // LANES) + jj
                    for r, rows in enumerate(_rows_by_residue(scr, j, stage_ref, slot, n_rows, d)):
                        out[0, r, base // d:(base + n_rows) // d, jj * LANES:(jj + 1) * LANES] = rows.astype(BF16)
                    slot += d == SPLIT_STRIDE * SPLIT_STRIDE


def _proj(x, norm_g, w_pre, bsz, seqlen):
    n = x.shape[0]
    tiles_per_seq = seqlen // TM_PROJ
    rows = lambda w: pl.BlockSpec((TM_PROJ, w), lambda i: (i, 0))
    qkv_specs, qkv_shapes = [], []
    for _ in range(3):
        for _, d in ATTN_PATTERNS:
            qkv_specs.append(pl.BlockSpec((1, d, TM_PROJ // d, PATTERN_WIDTH),
                                          lambda i: (i // tiles_per_seq, 0, i % tiles_per_seq, 0)))
            qkv_shapes.append(jax.ShapeDtypeStruct((bsz, d, seqlen // d, PATTERN_WIDTH), BF16))
    half_rows = TM_PROJ // PROJ_HALVES
    slabs = lambda k: pltpu.VMEM((k, half_rows, LANES), F32)
    n_staged = SSM_N_OCT + 3 * PATTERN_WIDTH // LANES
    stage = pltpu.VMEM((n_staged, SPLIT_STRIDE, half_rows // SPLIT_STRIDE, LANES), F32)
    outs = pl.pallas_call(
        _proj_kernel,
        grid=(n // TM_PROJ,),
        in_specs=[rows(D_MODEL), _resident((1, D_MODEL)), _resident((D_MODEL, PRE_WIDTH))],
        out_specs=[pl.BlockSpec((TM_PROJ // SSM_CHUNK, SSM_ROW), lambda i: (i, 0))] + qkv_specs,
        out_shape=[jax.ShapeDtypeStruct((n // SSM_CHUNK, SSM_ROW), BF16)] + qkv_shapes,
        scratch_shapes=([slabs(SSM_N_OCT)] + [slabs(ATTN_WIDTH // LANES)] * 3 + [stage]) * PROJ_HALVES,
        compiler_params=_params(1),
        name="proj",
    )(x, norm_g.reshape(1, D_MODEL), w_pre)
    u = outs[0]
    q, k, v = (outs[1 + i * N_PATTERNS:1 + (i + 1) * N_PATTERNS] for i in range(3))
    return u, q, k, v


def _ssm_compact(a_re, a_im, log_dt, b_re, b_im, c_re, c_im, d_skip):
    t_len = SSM_CHUNK
    dt = jnp.exp(log_dt)[:, None]
    mag = jnp.exp(a_re * dt)
    ang = a_im * dt
    abar_re = mag * jnp.cos(ang)
    abar_im = mag * jnp.sin(ang)
    nr = abar_re - 1.0
    ni = abar_im
    den = a_re * a_re + a_im * a_im
    coef_re = (nr * a_re + ni * a_im) / den
    coef_im = (ni * a_re - nr * a_im) / den
    bbar_re = coef_re[..., None] * b_re - coef_im[..., None] * b_im
    bbar_im = coef_re[..., None] * b_im + coef_im[..., None] * b_re
    bt_re = jnp.swapaxes(bbar_re, -1, -2)
    bt_im = jnp.swapaxes(bbar_im, -1, -2)
    pw_re = [jnp.ones_like(abar_re)]
    pw_im = [jnp.zeros_like(abar_re)]
    for _ in range(t_len):
        pr, pi = pw_re[-1], pw_im[-1]
        pw_re.append(pr * abar_re - pi * abar_im)
        pw_im.append(pr * abar_im + pi * abar_re)
    rev_re = jnp.stack(pw_re[t_len - 1::-1])[:, :, None, :]
    rev_im = jnp.stack(pw_im[t_len - 1::-1])[:, :, None, :]
    pw_re = jnp.stack(pw_re)
    pw_im = jnp.stack(pw_im)
    n_oct, oct_, grp, st = SSM_N_OCT, SSM_OCT, SSM_GROUP, SSM_STATE

    def per_octet(re, im):
        both = jnp.concatenate([re, im], axis=-1)
        n = both.shape[0]
        return both.reshape(n, n_oct, oct_ * grp, 2 * st).transpose(1, 0, 2, 3).reshape(n_oct, n * LANES, 2 * st)

    into_c = per_octet(rev_re * bt_re[None] - rev_im * bt_im[None], rev_re * bt_im[None] + rev_im * bt_re[None])
    cp_re = c_re[None] * pw_re[:, :, None, :] - c_im[None] * pw_im[:, :, None, :]
    cp_im = c_re[None] * pw_im[:, :, None, :] + c_im[None] * pw_re[:, :, None, :]
    outof_c = per_octet(cp_re, -cp_im)
    decay = jnp.stack([pw_re[t_len].reshape(n_oct, SSM_OCT_STATE), pw_im[t_len].reshape(n_oct, SSM_OCT_STATE)],
                      axis=1)
    skip = d_skip.reshape(n_oct, 1, LANES)
    return into_c.astype(BF16), outof_c.astype(BF16), decay, skip


def _ssm_expanders():
    row = np.arange(LANES)
    col = np.arange(2 * SSM_OCT_STATE)
    spread = (row[:, None] // SSM_STATE == col[None, :] // SSM_OCT_STATE) & (
        row[:, None] % SSM_STATE == col[None, :] % SSM_STATE)
    keep = row[:, None] // SSM_GROUP == (col[None, :] % SSM_OCT_STATE) // SSM_STATE
    return jnp.asarray(spread, BF16), jnp.asarray(keep, F32)


def _ssm_kernel(x_ref, intoc_ref, outofc_ref, decay_ref, skip_ref, spread_ref, keep_ref, y_ref,
                toep_ref, into_ref, outof_ref, loc_ref, prev_ref, part_ref, *, n_chunks, n_seq):
    blk = lambda i: slice(i * LANES, (i + 1) * LANES)

    @pl.when(pl.program_id(1) == 0)
    def _build_octet_matrices():
        spread, keep = spread_ref[...], keep_ref[...]
        for s in range(SSM_CHUNK):
            into_ref[blk(s), :] = (_dot(intoc_ref[0, blk(s), :], spread) * keep).astype(BF16)
        for j in range(SSM_CHUNK + 1):
            outof_ref[blk(j), :] = (_dot(outofc_ref[0, blk(j), :], spread) * keep).astype(BF16)
        bbar = into_ref[blk(SSM_CHUNK - 1), :]
        diag = (lax.broadcasted_iota(jnp.int32, (LANES, LANES), 0)
                == lax.broadcasted_iota(jnp.int32, (LANES, LANES), 1))
        zeros = jnp.zeros((LANES, LANES), BF16)
        for j in range(SSM_CHUNK):
            lag = _dot_nt(bbar, outof_ref[blk(j), :])
            if j == 0:
                lag = lag + jnp.where(diag, skip_ref[0], 0.0)
            lag = lag.astype(BF16)
            for s in range(SSM_CHUNK - j):
                toep_ref[blk(s), blk(s + j)] = lag
            if j > 0:
                for s in range(j, SSM_CHUNK):
                    toep_ref[blk(s), blk(s - j)] = zeros

    loc_ref[...] = _dot(x_ref[...], into_ref[...])
    n_tiles = SSM_OCT_IN // MXU_TILE
    early = n_tiles - part_ref.shape[1] // MXU_TILE
    in_chunk = lambda j: _dot(x_ref[:, :(j + 1) * MXU_TILE], toep_ref[:(j + 1) * MXU_TILE, j * MXU_TILE:(j + 1) * MXU_TILE])
    for j in range(early, n_tiles):
        part_ref[:, (j - early) * MXU_TILE:(j - early + 1) * MXU_TILE] = in_chunk(j)
    ar = decay_ref[0, 0:1, :]
    ai = decay_ref[0, 1:2, :]

    def step(c, carry):
        new = []
        for q in range(n_seq):
            s_re, s_im = carry[2 * q], carry[2 * q + 1]
            row = pl.ds(q * n_chunks + c, 1)
            prev_ref[row, :SSM_OCT_STATE] = s_re
            prev_ref[row, SSM_OCT_STATE:] = s_im
            new.append(ar * s_re - ai * s_im + loc_ref[row, :SSM_OCT_STATE])
            new.append(ar * s_im + ai * s_re + loc_ref[row, SSM_OCT_STATE:])
        return tuple(new)

    zero = jnp.zeros((1, SSM_OCT_STATE), F32)
    lax.fori_loop(0, n_chunks, step, (zero,) * (2 * n_seq), unroll=True)
    prev = prev_ref[...].astype(BF16)
    for j in range(n_tiles):
        lo, hi = j * MXU_TILE, (j + 1) * MXU_TILE
        within = in_chunk(j) if j < early else part_ref[:, (j - early) * MXU_TILE:(j - early + 1) * MXU_TILE]
        y = within + _dot_nt(prev, outof_ref[lo + LANES:hi + LANES, :])
        y_ref[:, lo:hi] = jax.nn.gelu(y).astype(BF16)


def _ssm(u_rows, compact, bsz):
    into_c, outof_c, decay, skip = compact
    spread, keep = _ssm_expanders()
    n_chunks = u_rows.shape[0] // bsz
    n_seq = SSM_SEQ_PER_STEP
    per_oct = lambda *shape: pl.BlockSpec((1,) + shape, lambda o, b: (o,) + (0,) * len(shape))
    rows = pl.BlockSpec((n_seq * n_chunks, SSM_OCT_IN), lambda o, b: (b, o))
    wide = 2 * SSM_OCT_STATE
    return pl.pallas_call(
        functools.partial(_ssm_kernel, n_chunks=n_chunks, n_seq=n_seq),
        grid=(SSM_N_OCT, bsz // n_seq),
        in_specs=[rows, per_oct(SSM_OCT_IN, LANES), per_oct(SSM_OCT_IN + LANES, LANES),
                  per_oct(2, SSM_OCT_STATE), per_oct(1, LANES), _resident((LANES, wide)), _resident((LANES, wide))],
        out_specs=rows,
        out_shape=jax.ShapeDtypeStruct(u_rows.shape, BF16),
        scratch_shapes=[pltpu.VMEM((SSM_OCT_IN, SSM_OCT_IN), BF16),
                        pltpu.VMEM((SSM_OCT_IN, wide), BF16),
                        pltpu.VMEM((SSM_OCT_IN + LANES, wide), BF16),
                        pltpu.VMEM((n_seq * n_chunks, wide), F32),
                        pltpu.VMEM((n_seq * n_chunks, wide), F32),
                        pltpu.VMEM((n_seq * n_chunks, SSM_OCT_IN // 2), F32)],
        compiler_params=_params(2),
        name="ssm",
    )(u_rows, into_c, outof_c, decay, skip, spread, keep)


def _t5_buckets(dist):
    dist = np.asarray(dist, np.int32)
    max_exact = REL_BUCKETS // 2
    safe = np.maximum(dist, 1).astype(np.float32)
    large = max_exact + (np.log(safe / max_exact) / np.log(REL_MAX_DIST / max_exact)
                         * (REL_BUCKETS - max_exact)).astype(np.int32)
    large = np.minimum(large, REL_BUCKETS - 1)
    return np.where(dist < max_exact, dist, large).astype(np.int32)


def _band_bias(rel_table, pattern):
    dilation = ATTN_PATTERNS[pattern][1]
    buckets = _t5_buckets(np.arange(WINDOW_KEYS + 1) * dilation)
    table = rel_table[:, pattern * HEADS_PER_PATTERN:(pattern + 1) * HEADS_PER_PATTERN].T
    runs = []
    for b in np.unique(buckets):
        runs.append(jnp.broadcast_to(table[:, b:b + 1], (HEADS_PER_PATTERN, int(np.sum(buckets == b)))))
    per_step = jnp.concatenate(runs, axis=1)
    period = 3 * SUB_Q
    masked = lambda n: jnp.full((HEADS_PER_PATTERN, n), NEG_BIG, F32)
    w = jnp.concatenate([masked(SUB_Q - 1), per_step[:, ::-1], masked(period - SUB_Q - WINDOW_KEYS)], axis=1)
    skew = jnp.tile(w, (1, SUB_Q))[:, :SUB_Q * (period - 1)].reshape(HEADS_PER_PATTERN, SUB_Q, period - 1)
    bias = skew[:, :, SUB_Q - 1:3 * SUB_Q - 1]
    first = jnp.where((np.arange(2 * SUB_Q) >= SUB_Q)[None, None, :], bias, NEG_BIG)
    pairs = lambda b: b.reshape(HEADS_PER_PATTERN // 2, 2 * SUB_Q, 2 * SUB_Q)
    return jnp.stack([pairs(bias), pairs(first)])


def _attn_kernel(q_ref, kc_ref, kh_ref, vc_ref, vh_ref, bias_ref, o_ref):
    n_res, tq = q_ref.shape[1], q_ref.shape[2]
    first = jnp.where(pl.program_id(2) == 0, 1, 0)
    lane = lax.broadcasted_iota(jnp.int32, (SUB_Q, 2 * HEAD_DIM), 1)
    low_half = lane < HEAD_DIM
    stat_is_max = lax.broadcasted_iota(jnp.int32, (2 * SUB_Q, 2 * HEAD_DIM), 1) % HEAD_DIM < HEAD_DIM // 2
    for r in range(n_res):
        for j in range(tq // SUB_Q):
            rows = slice(j * SUB_Q, (j + 1) * SUB_Q)
            q = q_ref[0, r, rows, :]
            if j == 0:
                k = jnp.concatenate([kh_ref[0, r], kc_ref[0, r, rows, :]], axis=0)
                v = jnp.concatenate([vh_ref[0, r], vc_ref[0, r, rows, :]], axis=0)
            else:
                k = kc_ref[0, r, (j - 1) * SUB_Q:(j + 1) * SUB_Q, :]
                v = vc_ref[0, r, (j - 1) * SUB_Q:(j + 1) * SUB_Q, :]
            for pair in range(HEADS_PER_PATTERN // 2):
                cols = slice(pair * 2 * HEAD_DIM, (pair + 1) * 2 * HEAD_DIM)
                q2, k2, v2 = q[:, cols], k[:, cols], v[:, cols]
                zero = jnp.zeros_like(q2)
                qq = jnp.concatenate([jnp.where(low_half, q2, zero), jnp.where(low_half, zero, q2)], axis=0)
                s = _dot_nt(qq, k2) + (bias_ref[first, pair] if j == 0 else bias_ref[0, pair])
                m = jnp.max(s, axis=-1, keepdims=True)
                p = jnp.exp(s - m)
                l = jnp.sum(p, axis=-1, keepdims=True)
                o = _dot(p.astype(BF16), v2)
                wide = (2 * SUB_Q, 2 * HEAD_DIM)
                stat = jnp.where(stat_is_max, jnp.broadcast_to(m, wide), jnp.broadcast_to(l, wide))
                o_ref[0, r, rows, cols] = jnp.where(low_half, o[:SUB_Q], o[SUB_Q:])
                o_ref[0, r, rows, PATTERN_WIDTH + pair * 2 * HEAD_DIM:PATTERN_WIDTH + (pair + 1) * 2 * HEAD_DIM] = (
                    jnp.where(low_half, stat[:SUB_Q], stat[SUB_Q:]))


def _dilated_attention(q, k, v, rel_table, pattern):
    bsz, d, res_len, _ = q.shape
    tq = min(res_len, ATTN_STEP_ROWS)
    n_res = ATTN_STEP_ROWS // tq
    cur = pl.BlockSpec((1, n_res, tq, PATTERN_WIDTH), lambda b, r, i: (b, r, i, 0))
    halo = pl.BlockSpec((1, n_res, SUB_Q, PATTERN_WIDTH),
                        lambda b, r, i: (b, r, jnp.maximum(i * (tq // SUB_Q) - 1, 0), 0))
    return pl.pallas_call(
        _attn_kernel,
        grid=(bsz, d // n_res, res_len // tq),
        in_specs=[cur, cur, halo, cur, halo, _resident((2, HEADS_PER_PATTERN // 2, 2 * SUB_Q, 2 * SUB_Q))],
        out_specs=pl.BlockSpec((1, n_res, tq, 2 * PATTERN_WIDTH), lambda b, r, i: (b, r, i, 0)),
        out_shape=jax.ShapeDtypeStruct((bsz, d, res_len, 2 * PATTERN_WIDTH), F32),
        compiler_params=_params(3),
        name=f"attn_d{d}",
    )(q, k, k, v, v, _band_bias(rel_table, pattern))


def _memkv_kernel(mem_ref, g_ref, w_ref, k_ref, v_ref):
    mn = _rmsnorm(mem_ref[...], g_ref[...]).astype(BF16)
    k_ref[...] = _dot(mn, w_ref[:, :XATTN_WIDTH]).astype(BF16)
    v_ref[...] = _dot(mn, w_ref[:, XATTN_WIDTH:]).astype(BF16)


def _memkv(mem2d, norm_g, w_kv):
    rows = mem2d.shape[0]
    full = lambda *shape: pl.BlockSpec(shape, lambda i: (0,) * len(shape))
    return pl.pallas_call(
        _memkv_kernel,
        grid=(1,),
        in_specs=[full(rows, D_MODEL), full(1, D_MODEL), full(D_MODEL, 2 * XATTN_WIDTH)],
        out_specs=[full(rows, XATTN_WIDTH), full(rows, XATTN_WIDTH)],
        out_shape=[jax.ShapeDtypeStruct((rows, XATTN_WIDTH), BF16)] * 2,
        compiler_params=_params(1),
        name="memkv",
    )(mem2d, norm_g.reshape(1, D_MODEL), w_kv)


def _merge_kernel(x_ref, y_ref, o0_ref, o1_ref, o2_ref, mk_ref, mv_ref, g_ref, wpost_ref, wglu_ref,
                  wau_ref, wxu_ref, wout_ref, out_ref, ys_ref, s1_ref, s2_ref):
    x = x_ref[...]
    un = _rmsnorm(x, g_ref[...]).astype(BF16)

    def gate(index):
        lo = XATTN_WIDTH + index * D_MODEL
        return _sigmoid(_dot(un, wpost_ref[:, lo:lo + D_MODEL]))

    slabs = 2 * PATTERN_WIDTH // LANES
    for o_ref, s_ref in ((o1_ref, s1_ref), (o2_ref, s2_ref)):
        d = o_ref.shape[1]
        for r in range(d):
            for j in range(slabs):
                s_ref[j, pl.ds(r, TM_MERGE // d, stride=d), :] = o_ref[0, r, :, j * LANES:(j + 1) * LANES]
    half = slabs // 2
    is_max = lax.broadcasted_iota(jnp.int32, (TM_MERGE, LANES), 1) % HEAD_DIM < HEAD_DIM // 2

    def unpack(stat):
        return (jnp.where(is_max, stat, pltpu.roll(stat, HEAD_DIM // 2, axis=1)),
                jnp.where(is_max, pltpu.roll(stat, LANES - HEAD_DIM // 2, axis=1), stat))

    def pattern(read):
        out = jnp.concatenate([read(j) for j in range(half)], axis=-1)
        stats = [unpack(read(j)) for j in range(half, slabs)]
        return (out, jnp.concatenate([m for m, _ in stats], axis=-1),
                jnp.concatenate([l for _, l in stats], axis=-1))

    parts = [pattern(lambda j: o0_ref[0, 0, :, j * LANES:(j + 1) * LANES]),
             pattern(lambda j: s1_ref[j]), pattern(lambda j: s2_ref[j])]
    top = jnp.maximum(jnp.maximum(parts[0][1], parts[1][1]), parts[2][1])
    ws = [jnp.exp(m - top) for _, m, _ in parts]
    att = ((ws[0] * parts[0][0] + ws[1] * parts[1][0] + ws[2] * parts[2][0])
           / (ws[0] * parts[0][2] + ws[1] * parts[1][2] + ws[2] * parts[2][2]))
    att = att.astype(BF16)

    chunk_rows = TM_MERGE // SSM_CHUNK
    for o in range(SSM_N_OCT):
        for t in range(SSM_CHUNK):
            lo = o * SSM_OCT_IN + t * LANES
            ys_ref[o, pl.ds(t, chunk_rows, stride=SSM_CHUNK), :] = y_ref[:, lo:lo + LANES].astype(F32)
    ys = jnp.concatenate([ys_ref[o] for o in range(SSM_N_OCT)], axis=-1).astype(BF16)
    glu = _dot(ys, wglu_ref[:, :D_MODEL]) * _sigmoid(_dot(ys, wglu_ref[:, D_MODEL:]))
    merged = gate(0) * glu

    xq = _dot(un, wpost_ref[:, :XATTN_WIDTH]).astype(BF16)
    heads = []
    for h in range(XATTN_HEADS):
        cols = slice(h * XATTN_HEAD_DIM, (h + 1) * XATTN_HEAD_DIM)
        s = _dot_nt(xq[:, cols], mk_ref[0, :, cols]) * (XATTN_HEAD_DIM ** -0.5)
        p = jnp.exp(s - jnp.max(s, axis=-1, keepdims=True))
        heads.append(_dot(p.astype(BF16), mv_ref[0, :, cols]) / jnp.sum(p, axis=-1, keepdims=True))
    xo = jnp.concatenate(heads, axis=-1).astype(BF16)
    merged = merged + gate(2) * _dot(xo, wxu_ref[...])

    merged = merged + gate(1) * _dot(att, wau_ref[...])

    out_ref[...] = x + _dot(merged.astype(BF16), wout_ref[...])


def _merge(x, y_rows, attn_outs, mk, mv, norm_g, w_post, w_glu, w_au, w_xu, w_out, seqlen):
    n = x.shape[0]
    tiles_per_seq = seqlen // TM_MERGE
    rows = lambda w: pl.BlockSpec((TM_MERGE, w), lambda i: (i, 0))
    mem = pl.BlockSpec((1, MEM_LEN, XATTN_WIDTH), lambda i: (i // tiles_per_seq, 0, 0))
    attn_specs = [pl.BlockSpec((1, d, TM_MERGE // d, 2 * PATTERN_WIDTH),
                               lambda i: (i // tiles_per_seq, 0, i % tiles_per_seq, 0))
                  for _, d in ATTN_PATTERNS]
    slabs = lambda k: pltpu.VMEM((k, TM_MERGE, LANES), F32)
    return pl.pallas_call(
        _merge_kernel,
        grid=(n // TM_MERGE,),
        in_specs=[rows(D_MODEL), pl.BlockSpec((TM_MERGE // SSM_CHUNK, SSM_ROW), lambda i: (i, 0))] + attn_specs
                 + [mem, mem, _resident((1, D_MODEL)), _resident((D_MODEL, POST_WIDTH)),
                    _resident((SSM_WIDTH, 2 * D_MODEL)), _resident((PATTERN_WIDTH, D_MODEL)),
                    _resident((XATTN_WIDTH, D_MODEL)), _resident((D_MODEL, D_MODEL))],
        out_specs=rows(D_MODEL),
        out_shape=jax.ShapeDtypeStruct((n, D_MODEL), F32),
        scratch_shapes=[slabs(SSM_N_OCT), slabs(2 * PATTERN_WIDTH // LANES), slabs(2 * PATTERN_WIDTH // LANES)],
        compiler_params=_params(1),
        name="merge",
    )(x, y_rows, *attn_outs, mk, mv, norm_g.reshape(1, D_MODEL), w_post, w_glu, w_au, w_xu, w_out)


def _layer(x, mem, rel_table, ffn1_norm, ffn1_w_in, ffn1_w_down, mix_norm, w_in,
           ssm_a_re, ssm_a_im, ssm_log_dt, ssm_b_re, ssm_b_im, ssm_c_re, ssm_c_im, ssm_d, ssm_w_glu,
           attn_w_up, mem_norm, xattn_w_kv, xattn_w_up, w_out, ffn2_norm, ffn2_w_in, ffn2_w_down,
           final_norm, bsz, seqlen):
    cast = [(w_in, (PRE_WIDTH, POST_WIDTH)), (ssm_w_glu, (2 * D_MODEL,)), (attn_w_up, (D_MODEL,)),
            (xattn_w_kv, (2 * XATTN_WIDTH,)), (xattn_w_up, (D_MODEL,)), (w_out, (D_MODEL,)),
            (ffn2_w_in, (2 * D_FF,)), (ffn2_w_down, (D_MODEL,))]
    x, (w_pre, w_post, w_glu, w_au, w_kv, w_xu, w_o, w2_in, w2_down) = _ffn(
        x, ffn1_norm, ffn1_w_in.astype(BF16), ffn1_w_down.astype(BF16), cast=cast)
    u_rows, q, k, v = _proj(x, mix_norm, w_pre, bsz, seqlen)
    compact = _ssm_compact(ssm_a_re, ssm_a_im, ssm_log_dt, ssm_b_re, ssm_b_im, ssm_c_re, ssm_c_im, ssm_d)
    y_rows = _ssm(u_rows, compact, bsz)
    attn_outs = [_dilated_attention(q[g], k[g], v[g], rel_table, g) for g in range(N_PATTERNS)]
    mk, mv = _memkv(mem.reshape(bsz * MEM_LEN, D_MODEL), mem_norm, w_kv)
    mk = mk.reshape(bsz, MEM_LEN, XATTN_WIDTH)
    mv = mv.reshape(bsz, MEM_LEN, XATTN_WIDTH)
    x = _merge(x, y_rows, attn_outs, mk, mv, mix_norm, w_post, w_glu, w_au, w_xu, w_o, seqlen)
    return _ffn(x, ffn2_norm, w2_in, w2_down, final_g=final_norm)[0]


def kernel(x, mem, ffn1_norm, ffn1_w_in, ffn1_w_down, mix_norm, w_in, ssm_a_re, ssm_a_im, ssm_log_dt,
           ssm_b_re, ssm_b_im, ssm_c_re, ssm_c_im, ssm_d, ssm_w_glu, rel_table, attn_w_up, mem_norm,
           xattn_w_kv, xattn_w_up, w_out, ffn2_norm, ffn2_w_in, ffn2_w_down, final_norm):
    bsz, seqlen, _ = x.shape
    assert ffn1_norm.shape[0] == 1, "single-layer trunk"
    assert x.shape[2] == D_MODEL and mem.shape == (bsz, MEM_LEN, D_MODEL)
    assert w_in.shape[1:] == (D_MODEL, PRE_WIDTH + POST_WIDTH) and ffn1_w_in.shape[1:] == (D_MODEL, 2 * D_FF)
    assert ssm_b_re.shape[1:] == (SSM_WIDTH // SSM_GROUP, SSM_STATE, SSM_GROUP)
    assert rel_table.shape == (REL_BUCKETS, N_PATTERNS * HEADS_PER_PATTERN)
    assert seqlen % ATTN_STEP_ROWS == 0 and (seqlen // ATTN_PATTERNS[-1][1]) % SUB_Q == 0
    assert seqlen % TM_PROJ == 0 and seqlen % TM_MERGE == 0 and (bsz * seqlen) % TM_FFN == 0
    assert bsz % SSM_SEQ_PER_STEP == 0
    h = _layer(x.reshape(bsz * seqlen, D_MODEL), mem, rel_table, ffn1_norm[0], ffn1_w_in[0], ffn1_w_down[0],
               mix_norm[0], w_in[0], ssm_a_re[0], ssm_a_im[0], ssm_log_dt[0], ssm_b_re[0], ssm_b_im[0],
               ssm_c_re[0], ssm_c_im[0], ssm_d[0], ssm_w_glu[0], attn_w_up[0], mem_norm[0], xattn_w_kv[0],
               xattn_w_up[0], w_out[0], ffn2_norm[0], ffn2_w_in[0], ffn2_w_down[0], final_norm,
               bsz, seqlen)
    return h.reshape(bsz, seqlen, D_MODEL)
```

```python
import functools

import numpy as np
import jax
import jax.numpy as jnp
from jax import lax
from jax.experimental import pallas as pl
from jax.experimental.pallas import tpu as pltpu

F32 = jnp.float32
BF16 = jnp.bfloat16

LANES = 128
BF16_SUBLANES = 16
MXU_TILE = 256

D_MODEL = 1024
D_FF = 2816
EPS = 1e-6

HEAD_DIM = 64
ATTN_PATTERNS = ((128, 1), (512, 4), (2048, 16))
N_PATTERNS = 3
HEADS_PER_PATTERN = 4
ATTN_WIDTH = N_PATTERNS * HEADS_PER_PATTERN * HEAD_DIM
PATTERN_WIDTH = HEADS_PER_PATTERN * HEAD_DIM
WINDOW_KEYS = 128
REL_BUCKETS = 32
REL_MAX_DIST = 2048

SSM_GROUP = 16
SSM_STATE = 64
SSM_WIDTH = 512
SSM_CHUNK = 16
SSM_OCT = LANES // SSM_GROUP
SSM_N_OCT = SSM_WIDTH // LANES
SSM_OCT_IN = SSM_CHUNK * LANES
SSM_OCT_STATE = SSM_OCT * SSM_STATE
SSM_ROW = SSM_CHUNK * SSM_WIDTH

MEM_LEN = 256
XATTN_HEADS = 4
XATTN_HEAD_DIM = 128
XATTN_WIDTH = XATTN_HEADS * XATTN_HEAD_DIM

PRE_WIDTH = SSM_WIDTH + 3 * ATTN_WIDTH
POST_WIDTH = XATTN_WIDTH + 3 * D_MODEL

NEG_BIG = -1e30

VMEM_LIMIT_BYTES = 56 * 1024 * 1024

TM_FFN = 1024
FFN_ROW_PARTS = 4
FFN_CHUNKS = ((0, 1536), (1536, D_FF))
TM_PROJ = 1024
PROJ_HALVES = 2
SPLIT_STRIDE = 4
TM_MERGE = 512
ATTN_STEP_ROWS = 4096
SSM_SEQ_PER_STEP = 2
SUB_Q = 128


def _rmsnorm(x, g):
    return x * lax.rsqrt(jnp.mean(x * x, axis=-1, keepdims=True) + EPS) * g


def _sigmoid(x):
    return 0.5 * jnp.tanh(0.5 * x) + 0.5


def _dot(a, b):
    return jnp.dot(a, b, preferred_element_type=F32)


def _dot_nt(a, b):
    return lax.dot_general(a, b, (((1,), (1,)), ((), ())), preferred_element_type=F32)


def _resident(shape):
    nd = len(shape)
    return pl.BlockSpec(shape, lambda *_: (0,) * nd, pipeline_mode=pl.Buffered(1))


def _params(n_axes):
    return pltpu.CompilerParams(dimension_semantics=("arbitrary",) * n_axes,
                                vmem_limit_bytes=VMEM_LIMIT_BYTES)


def _ffn_kernel(x_ref, g_ref, win_ref, wdn_ref, *rest, final_norm, n_cast):
    rest = list(rest)
    fg_ref = rest.pop(0) if final_norm else None
    cast_src, o_ref, cast_dst = rest[:n_cast], rest[n_cast], rest[n_cast + 1:]
    for part in range(FFN_ROW_PARTS):
        rows = slice(part * TM_FFN // FFN_ROW_PARTS, (part + 1) * TM_FFN // FFN_ROW_PARTS)
        x = x_ref[rows, :]
        un = _rmsnorm(x, g_ref[...]).astype(BF16)
        acc = None
        for lo, hi in FFN_CHUNKS:
            a = _dot(un, win_ref[:, lo:hi])
            b = _dot(un, win_ref[:, D_FF + lo:D_FF + hi])
            h = (a * _sigmoid(a) * b).astype(BF16)
            part = _dot(h, wdn_ref[lo:hi, :])
            acc = part if acc is None else acc + part
        y = x + 0.5 * acc
        if final_norm:
            y = _rmsnorm(y, fg_ref[...])
        o_ref[rows, :] = y

    dst = iter(cast_dst)
    for src in cast_src:
        lo = 0
        while lo < src.shape[1]:
            out = next(dst)
            out[...] = src[:, lo:lo + out.shape[1]].astype(BF16)
            lo += out.shape[1]


def _cast_rows(n_rows, n_steps):
    rows = BF16_SUBLANES
    while n_rows % rows or n_rows // rows > n_steps:
        rows += BF16_SUBLANES
    return rows


def _ffn(x, norm_g, w_in, w_down, final_g=None, cast=()):
    n = x.shape[0]
    n_steps = n // TM_FFN
    row = pl.BlockSpec((TM_FFN, D_MODEL), lambda i: (i, 0))
    in_specs = [row, _resident((1, D_MODEL)), _resident((D_MODEL, 2 * D_FF)), _resident((D_FF, D_MODEL))]
    args = [x, norm_g.reshape(1, D_MODEL), w_in, w_down]
    if final_g is not None:
        in_specs.append(_resident((1, D_MODEL)))
        args.append(final_g.reshape(1, D_MODEL))
    out_specs, out_shapes = [row], [jax.ShapeDtypeStruct((n, D_MODEL), F32)]
    for w, widths in cast:
        assert sum(widths) == w.shape[1]
        rows = _cast_rows(w.shape[0], n_steps)
        last = w.shape[0] // rows - 1
        block = lambda width, last=last, rows=rows: pl.BlockSpec((rows, width), lambda i: (jnp.minimum(i, last), 0))
        in_specs.append(block(w.shape[1]))
        args.append(w)
        out_specs += [block(width) for width in widths]
        out_shapes += [jax.ShapeDtypeStruct((w.shape[0], width), BF16) for width in widths]
    outs = pl.pallas_call(
        functools.partial(_ffn_kernel, final_norm=final_g is not None, n_cast=len(cast)),
        grid=(n_steps,),
        in_specs=in_specs,
        out_specs=out_specs,
        out_shape=out_shapes,
        compiler_params=_params(1),
        name="ffn_final" if final_g is not None else "ffn",
    )(*args)
    return outs[0], outs[1:]


def _rows_by_residue(slab_ref, j, stage_ref, slot, n_rows, d):
    if d == 1:
        return [slab_ref[j]]
    if d == SPLIT_STRIDE:
        return [slab_ref[j, pl.ds(r, n_rows // d, stride=d), :] for r in range(d)]
    assert d == SPLIT_STRIDE * SPLIT_STRIDE
    for low in range(SPLIT_STRIDE):
        stage_ref[slot, low] = slab_ref[j, pl.ds(low, n_rows // SPLIT_STRIDE, stride=SPLIT_STRIDE), :]
    return [stage_ref[slot, r % SPLIT_STRIDE, pl.ds(r // SPLIT_STRIDE, n_rows // d, stride=SPLIT_STRIDE), :]
            for r in range(d)]


def _proj_kernel(x_ref, g_ref, w_ref, u_ref, *rest):
    qkv_refs, slabs = rest[:N_PATTERNS], rest[N_PATTERNS:]
    n_rows = TM_PROJ // PROJ_HALVES
    per_half = len(slabs) // PROJ_HALVES
    for half in range(PROJ_HALVES):
        base = half * n_rows
        su_ref, sq_ref, sk_ref, sv_ref, stage_ref = slabs[per_half * half:per_half * (half + 1)]
        un = _rmsnorm(x_ref[base:base + n_rows, :], g_ref[...]).astype(BF16)
        full = _dot(un, w_ref[...])
        slot = 0

        u = full[:, :SSM_WIDTH]
        chunks = slice(base // SSM_CHUNK, (base + n_rows) // SSM_CHUNK)
        for o in range(SSM_N_OCT):
            su_ref[o] = u[:, o * LANES:(o + 1) * LANES]
            for t, rows in enumerate(_rows_by_residue(su_ref, o, stage_ref, slot, n_rows, SSM_CHUNK)):
                lo = o * SSM_OCT_IN + t * LANES
                u_ref[chunks, lo:lo + LANES] = rows.astype(BF16)
            slot += 1

        for which, scr in enumerate((sq_ref, sk_ref, sv_ref)):
            lo = SSM_WIDTH + which * ATTN_WIDTH
            res = full[:, lo:lo + ATTN_WIDTH]
            if which == 0:
                res = res * (HEAD_DIM ** -0.5)
            for j in range(ATTN_WIDTH // LANES):
                scr[j] = res[:, j * LANES:(j + 1) * LANES]
            for g, (_, d) in enumerate(ATTN_PATTERNS):
                out = qkv_refs[g]
                for jj in range(PATTERN_WIDTH // LANES):
                    j = g * (PATTERN_WIDTH // LANES) + jj
                    for r, rows in enumerate(_rows_by_residue(scr, j, stage_ref, slot, n_rows, d)):
                        out[which, 0, r, base // d:(base + n_rows) // d, jj * LANES:(jj + 1) * LANES] = (
                            rows.astype(BF16))
                    slot += d == SPLIT_STRIDE * SPLIT_STRIDE


def _proj(x, norm_g, w_pre, bsz, seqlen):
    n = x.shape[0]
    tiles_per_seq = seqlen // TM_PROJ
    rows = lambda w: pl.BlockSpec((TM_PROJ, w), lambda i: (i, 0))
    qkv_specs = [pl.BlockSpec((3, 1, d, TM_PROJ // d, PATTERN_WIDTH),
                              lambda i: (0, i // tiles_per_seq, 0, i % tiles_per_seq, 0)) for _, d in ATTN_PATTERNS]
    qkv_shapes = [jax.ShapeDtypeStruct((3, bsz, d, seqlen // d, PATTERN_WIDTH), BF16) for _, d in ATTN_PATTERNS]
    half_rows = TM_PROJ // PROJ_HALVES
    slabs = lambda k: pltpu.VMEM((k, half_rows, LANES), F32)
    n_staged = SSM_N_OCT + 3 * PATTERN_WIDTH // LANES
    stage = pltpu.VMEM((n_staged, SPLIT_STRIDE, half_rows // SPLIT_STRIDE, LANES), F32)
    outs = pl.pallas_call(
        _proj_kernel,
        grid=(n // TM_PROJ,),
        in_specs=[rows(D_MODEL), _resident((1, D_MODEL)), _resident((D_MODEL, PRE_WIDTH))],
        out_specs=[pl.BlockSpec((TM_PROJ // SSM_CHUNK, SSM_ROW), lambda i: (i, 0))] + qkv_specs,
        out_shape=[jax.ShapeDtypeStruct((n // SSM_CHUNK, SSM_ROW), BF16)] + qkv_shapes,
        scratch_shapes=([slabs(SSM_N_OCT)] + [slabs(ATTN_WIDTH // LANES)] * 3 + [stage]) * PROJ_HALVES,
        compiler_params=_params(1),
        name="proj",
    )(x, norm_g.reshape(1, D_MODEL), w_pre)
    return outs[0], outs[1:]


def _ssm_compact(a_re, a_im, log_dt, b_re, b_im, c_re, c_im, d_skip):
    t_len = SSM_CHUNK
    dt = jnp.exp(log_dt)[:, None]
    mag = jnp.exp(a_re * dt)
    ang = a_im * dt
    abar_re = mag * jnp.cos(ang)
    abar_im = mag * jnp.sin(ang)
    nr = abar_re - 1.0
    ni = abar_im
    den = a_re * a_re + a_im * a_im
    coef_re = (nr * a_re + ni * a_im) / den
    coef_im = (ni * a_re - nr * a_im) / den
    bbar_re = coef_re[..., None] * b_re - coef_im[..., None] * b_im
    bbar_im = coef_re[..., None] * b_im + coef_im[..., None] * b_re
    bt_re = jnp.swapaxes(bbar_re, -1, -2)
    bt_im = jnp.swapaxes(bbar_im, -1, -2)
    pw_re = [jnp.ones_like(abar_re)]
    pw_im = [jnp.zeros_like(abar_re)]
    for _ in range(t_len):
        pr, pi = pw_re[-1], pw_im[-1]
        pw_re.append(pr * abar_re - pi * abar_im)
        pw_im.append(pr * abar_im + pi * abar_re)
    rev_re = jnp.stack(pw_re[t_len - 1::-1])[:, :, None, :]
    rev_im = jnp.stack(pw_im[t_len - 1::-1])[:, :, None, :]
    pw_re = jnp.stack(pw_re)
    pw_im = jnp.stack(pw_im)
    n_oct, oct_, grp, st = SSM_N_OCT, SSM_OCT, SSM_GROUP, SSM_STATE

    def per_octet(re, im):
        both = jnp.concatenate([re, im], axis=-1)
        n = both.shape[0]
        return both.reshape(n, n_oct, oct_ * grp, 2 * st).transpose(1, 0, 2, 3).reshape(n_oct, n * LANES, 2 * st)

    into_c = per_octet(rev_re * bt_re[None] - rev_im * bt_im[None], rev_re * bt_im[None] + rev_im * bt_re[None])
    cp_re = c_re[None] * pw_re[:, :, None, :] - c_im[None] * pw_im[:, :, None, :]
    cp_im = c_re[None] * pw_im[:, :, None, :] + c_im[None] * pw_re[:, :, None, :]
    outof_c = per_octet(cp_re, -cp_im)
    decay = jnp.stack([pw_re[t_len].reshape(n_oct, SSM_OCT_STATE), pw_im[t_len].reshape(n_oct, SSM_OCT_STATE)],
                      axis=1)
    skip = d_skip.reshape(n_oct, 1, LANES)
    return into_c.astype(BF16), outof_c.astype(BF16), decay, skip


def _ssm_expanders():
    row = np.arange(LANES)
    col = np.arange(2 * SSM_OCT_STATE)
    spread = (row[:, None] // SSM_STATE == col[None, :] // SSM_OCT_STATE) & (
        row[:, None] % SSM_STATE == col[None, :] % SSM_STATE)
    keep = row[:, None] // SSM_GROUP == (col[None, :] % SSM_OCT_STATE) // SSM_STATE
    return jnp.asarray(spread, BF16), jnp.asarray(keep, F32)


def _ssm_kernel(x_ref, intoc_ref, outofc_ref, decay_ref, skip_ref, spread_ref, keep_ref, y_ref,
                toep_ref, into_ref, outof_ref, loc_ref, prev_ref, part_ref, *, n_chunks, n_seq):
    blk = lambda i: slice(i * LANES, (i + 1) * LANES)

    @pl.when(pl.program_id(1) == 0)
    def _build_octet_matrices():
        spread, keep = spread_ref[...], keep_ref[...]
        for s in range(SSM_CHUNK):
            into_ref[blk(s), :] = (_dot(intoc_ref[0, blk(s), :], spread) * keep).astype(BF16)
        for j in range(SSM_CHUNK + 1):
            outof_ref[blk(j), :] = (_dot(outofc_ref[0, blk(j), :], spread) * keep).astype(BF16)
        bbar = into_ref[blk(SSM_CHUNK - 1), :]
        diag = (lax.broadcasted_iota(jnp.int32, (LANES, LANES), 0)
                == lax.broadcasted_iota(jnp.int32, (LANES, LANES), 1))
        zeros = jnp.zeros((LANES, LANES), BF16)
        for j in range(SSM_CHUNK):
            lag = _dot_nt(bbar, outof_ref[blk(j), :])
            if j == 0:
                lag = lag + jnp.where(diag, skip_ref[0], 0.0)
            lag = lag.astype(BF16)
            for s in range(SSM_CHUNK - j):
                toep_ref[blk(s), blk(s + j)] = lag
            if j > 0:
                for s in range(j, SSM_CHUNK):
                    toep_ref[blk(s), blk(s - j)] = zeros

    loc_ref[...] = _dot(x_ref[...], into_ref[...])
    n_tiles = SSM_OCT_IN // MXU_TILE
    early = n_tiles - part_ref.shape[1] // MXU_TILE
    in_chunk = lambda j: _dot(x_ref[:, :(j + 1) * MXU_TILE], toep_ref[:(j + 1) * MXU_TILE, j * MXU_TILE:(j + 1) * MXU_TILE])
    for j in range(early, n_tiles):
        part_ref[:, (j - early) * MXU_TILE:(j - early + 1) * MXU_TILE] = in_chunk(j)
    ar = decay_ref[0, 0:1, :]
    ai = decay_ref[0, 1:2, :]

    def step(c, carry):
        new = []
        for q in range(n_seq):
            s_re, s_im = carry[2 * q], carry[2 * q + 1]
            row = pl.ds(q * n_chunks + c, 1)
            prev_ref[row, :SSM_OCT_STATE] = s_re
            prev_ref[row, SSM_OCT_STATE:] = s_im
            new.append(ar * s_re - ai * s_im + loc_ref[row, :SSM_OCT_STATE])
            new.append(ar * s_im + ai * s_re + loc_ref[row, SSM_OCT_STATE:])
        return tuple(new)

    zero = jnp.zeros((1, SSM_OCT_STATE), F32)
    lax.fori_loop(0, n_chunks, step, (zero,) * (2 * n_seq), unroll=True)
    prev = prev_ref[...].astype(BF16)
    for j in range(n_tiles):
        lo, hi = j * MXU_TILE, (j + 1) * MXU_TILE
        within = in_chunk(j) if j < early else part_ref[:, (j - early) * MXU_TILE:(j - early + 1) * MXU_TILE]
        y = within + _dot_nt(prev, outof_ref[lo + LANES:hi + LANES, :])
        y_ref[:, lo:hi] = jax.nn.gelu(y).astype(BF16)


def _ssm(u_rows, compact, bsz):
    into_c, outof_c, decay, skip = compact
    spread, keep = _ssm_expanders()
    n_chunks = u_rows.shape[0] // bsz
    n_seq = SSM_SEQ_PER_STEP
    per_oct = lambda *shape: pl.BlockSpec((1,) + shape, lambda o, b: (o,) + (0,) * len(shape))
    rows = pl.BlockSpec((n_seq * n_chunks, SSM_OCT_IN), lambda o, b: (b, o))
    wide = 2 * SSM_OCT_STATE
    return pl.pallas_call(
        functools.partial(_ssm_kernel, n_chunks=n_chunks, n_seq=n_seq),
        grid=(SSM_N_OCT, bsz // n_seq),
        in_specs=[rows, per_oct(SSM_OCT_IN, LANES), per_oct(SSM_OCT_IN + LANES, LANES),
                  per_oct(2, SSM_OCT_STATE), per_oct(1, LANES), _resident((LANES, wide)), _resident((LANES, wide))],
        out_specs=rows,
        out_shape=jax.ShapeDtypeStruct(u_rows.shape, BF16),
        scratch_shapes=[pltpu.VMEM((SSM_OCT_IN, SSM_OCT_IN), BF16),
                        pltpu.VMEM((SSM_OCT_IN, wide), BF16),
                        pltpu.VMEM((SSM_OCT_IN + LANES, wide), BF16),
                        pltpu.VMEM((n_seq * n_chunks, wide), F32),
                        pltpu.VMEM((n_seq * n_chunks, wide), F32),
                        pltpu.VMEM((n_seq * n_chunks, SSM_OCT_IN // 2), F32)],
        compiler_params=_params(2),
        name="ssm",
    )(u_rows, into_c, outof_c, decay, skip, spread, keep)


def _t5_buckets(dist):
    dist = np.asarray(dist, np.int32)
    max_exact = REL_BUCKETS // 2
    safe = np.maximum(dist, 1).astype(np.float32)
    large = max_exact + (np.log(safe / max_exact) / np.log(REL_MAX_DIST / max_exact)
                         * (REL_BUCKETS - max_exact)).astype(np.int32)
    large = np.minimum(large, REL_BUCKETS - 1)
    return np.where(dist < max_exact, dist, large).astype(np.int32)


def _band_bias(rel_table, pattern):
    dilation = ATTN_PATTERNS[pattern][1]
    buckets = _t5_buckets(np.arange(WINDOW_KEYS + 1) * dilation)
    table = rel_table[:, pattern * HEADS_PER_PATTERN:(pattern + 1) * HEADS_PER_PATTERN].T
    runs = []
    for b in np.unique(buckets):
        runs.append(jnp.broadcast_to(table[:, b:b + 1], (HEADS_PER_PATTERN, int(np.sum(buckets == b)))))
    per_step = jnp.concatenate(runs, axis=1)
    period = 3 * SUB_Q
    masked = lambda n: jnp.full((HEADS_PER_PATTERN, n), NEG_BIG, F32)
    w = jnp.concatenate([masked(SUB_Q - 1), per_step[:, ::-1], masked(period - SUB_Q - WINDOW_KEYS)], axis=1)
    skew = jnp.tile(w, (1, SUB_Q))[:, :SUB_Q * (period - 1)].reshape(HEADS_PER_PATTERN, SUB_Q, period - 1)
    bias = skew[:, :, SUB_Q - 1:3 * SUB_Q - 1]
    first = jnp.where((np.arange(2 * SUB_Q) >= SUB_Q)[None, None, :], bias, NEG_BIG)
    pairs = lambda b: b.reshape(HEADS_PER_PATTERN // 2, 2 * SUB_Q, 2 * SUB_Q)
    return jnp.stack([pairs(bias), pairs(first)])


def _attn_kernel(q_ref, kc_ref, kh_ref, vc_ref, vh_ref, bias_ref, o_ref):
    n_res, tq = q_ref.shape[2], q_ref.shape[3]
    first = jnp.where(pl.program_id(2) == 0, 1, 0)
    lane = lax.broadcasted_iota(jnp.int32, (SUB_Q, 2 * HEAD_DIM), 1)
    low_half = lane < HEAD_DIM
    stat_is_max = lax.broadcasted_iota(jnp.int32, (2 * SUB_Q, 2 * HEAD_DIM), 1) % HEAD_DIM < HEAD_DIM // 2
    for r in range(n_res):
        for j in range(tq // SUB_Q):
            rows = slice(j * SUB_Q, (j + 1) * SUB_Q)
            q = q_ref[0, 0, r, rows, :]
            if j == 0:
                k = jnp.concatenate([kh_ref[0, 0, r], kc_ref[0, 0, r, rows, :]], axis=0)
                v = jnp.concatenate([vh_ref[0, 0, r], vc_ref[0, 0, r, rows, :]], axis=0)
            else:
                k = kc_ref[0, 0, r, (j - 1) * SUB_Q:(j + 1) * SUB_Q, :]
                v = vc_ref[0, 0, r, (j - 1) * SUB_Q:(j + 1) * SUB_Q, :]
            for pair in range(HEADS_PER_PATTERN // 2):
                cols = slice(pair * 2 * HEAD_DIM, (pair + 1) * 2 * HEAD_DIM)
                q2, k2, v2 = q[:, cols], k[:, cols], v[:, cols]
                zero = jnp.zeros_like(q2)
                qq = jnp.concatenate([jnp.where(low_half, q2, zero), jnp.where(low_half, zero, q2)], axis=0)
                s = _dot_nt(qq, k2) + (bias_ref[first, pair] if j == 0 else bias_ref[0, pair])
                m = jnp.max(s, axis=-1, keepdims=True)
                p = jnp.exp(s - m)
                l = jnp.sum(p, axis=-1, keepdims=True)
                o = _dot(p.astype(BF16), v2)
                wide = (2 * SUB_Q, 2 * HEAD_DIM)
                stat = jnp.where(stat_is_max, jnp.broadcast_to(m, wide), jnp.broadcast_to(l, wide))
                o_ref[0, r, rows, cols] = jnp.where(low_half, o[:SUB_Q], o[SUB_Q:])
                o_ref[0, r, rows, PATTERN_WIDTH + pair * 2 * HEAD_DIM:PATTERN_WIDTH + (pair + 1) * 2 * HEAD_DIM] = (
                    jnp.where(low_half, stat[:SUB_Q], stat[SUB_Q:]))


def _dilated_attention(qkv, rel_table, pattern):
    _, bsz, d, res_len, _ = qkv.shape
    tq = min(res_len, ATTN_STEP_ROWS)
    n_res = ATTN_STEP_ROWS // tq
    cur = lambda which: pl.BlockSpec((1, 1, n_res, tq, PATTERN_WIDTH), lambda b, r, i: (which, b, r, i, 0))
    halo = lambda which: pl.BlockSpec((1, 1, n_res, SUB_Q, PATTERN_WIDTH),
                                      lambda b, r, i: (which, b, r, jnp.maximum(i * (tq // SUB_Q) - 1, 0), 0))
    return pl.pallas_call(
        _attn_kernel,
        grid=(bsz, d // n_res, res_len // tq),
        in_specs=[cur(0), cur(1), halo(1), cur(2), halo(2),
                  _resident((2, HEADS_PER_PATTERN // 2, 2 * SUB_Q, 2 * SUB_Q))],
        out_specs=pl.BlockSpec((1, n_res, tq, 2 * PATTERN_WIDTH), lambda b, r, i: (b, r, i, 0)),
        out_shape=jax.ShapeDtypeStruct((bsz, d, res_len, 2 * PATTERN_WIDTH), F32),
        compiler_params=_params(3),
        name=f"attn_d{d}",
    )(qkv, qkv, qkv, qkv, qkv, _band_bias(rel_table, pattern))


def _memkv_kernel(mem_ref, g_ref, w_ref, k_ref, v_ref):
    mn = _rmsnorm(mem_ref[...], g_ref[...]).astype(BF16)
    k_ref[...] = _dot(mn, w_ref[:, :XATTN_WIDTH]).astype(BF16)
    v_ref[...] = _dot(mn, w_ref[:, XATTN_WIDTH:]).astype(BF16)


def _memkv(mem2d, norm_g, w_kv):
    rows = mem2d.shape[0]
    full = lambda *shape: pl.BlockSpec(shape, lambda i: (0,) * len(shape))
    return pl.pallas_call(
        _memkv_kernel,
        grid=(1,),
        in_specs=[full(rows, D_MODEL), full(1, D_MODEL), full(D_MODEL, 2 * XATTN_WIDTH)],
        out_specs=[full(rows, XATTN_WIDTH), full(rows, XATTN_WIDTH)],
        out_shape=[jax.ShapeDtypeStruct((rows, XATTN_WIDTH), BF16)] * 2,
        compiler_params=_params(1),
        name="memkv",
    )(mem2d, norm_g.reshape(1, D_MODEL), w_kv)


def _merge_kernel(x_ref, y_ref, o0_ref, o1_ref, o2_ref, mk_ref, mv_ref, g_ref, wpost_ref, wglu_ref,
                  wau_ref, wxu_ref, wout_ref, out_ref, ys_ref, s1_ref, s2_ref):
    x = x_ref[...]
    un = _rmsnorm(x, g_ref[...]).astype(BF16)

    def gate(index):
        lo = XATTN_WIDTH + index * D_MODEL
        return _sigmoid(_dot(un, wpost_ref[:, lo:lo + D_MODEL]))

    slabs = 2 * PATTERN_WIDTH // LANES
    for o_ref, s_ref in ((o1_ref, s1_ref), (o2_ref, s2_ref)):
        d = o_ref.shape[1]
        for r in range(d):
            for j in range(slabs):
                s_ref[j, pl.ds(r, TM_MERGE // d, stride=d), :] = o_ref[0, r, :, j * LANES:(j + 1) * LANES]
    half = slabs // 2
    is_max = lax.broadcasted_iota(jnp.int32, (TM_MERGE, LANES), 1) % HEAD_DIM < HEAD_DIM // 2

    def unpack(stat):
        return (jnp.where(is_max, stat, pltpu.roll(stat, HEAD_DIM // 2, axis=1)),
                jnp.where(is_max, pltpu.roll(stat, LANES - HEAD_DIM // 2, axis=1), stat))

    def pattern(read):
        out = jnp.concatenate([read(j) for j in range(half)], axis=-1)
        stats = [unpack(read(j)) for j in range(half, slabs)]
        return (out, jnp.concatenate([m for m, _ in stats], axis=-1),
                jnp.concatenate([l for _, l in stats], axis=-1))

    parts = [pattern(lambda j: o0_ref[0, 0, :, j * LANES:(j + 1) * LANES]),
             pattern(lambda j: s1_ref[j]), pattern(lambda j: s2_ref[j])]
    top = jnp.maximum(jnp.maximum(parts[0][1], parts[1][1]), parts[2][1])
    ws = [jnp.exp(m - top) for _, m, _ in parts]
    att = ((ws[0] * parts[0][0] + ws[1] * parts[1][0] + ws[2] * parts[2][0])
           / (ws[0] * parts[0][2] + ws[1] * parts[1][2] + ws[2] * parts[2][2]))
    att = att.astype(BF16)

    chunk_rows = TM_MERGE // SSM_CHUNK
    for o in range(SSM_N_OCT):
        for t in range(SSM_CHUNK):
            lo = o * SSM_OCT_IN + t * LANES
            ys_ref[o, pl.ds(t, chunk_rows, stride=SSM_CHUNK), :] = y_ref[:, lo:lo + LANES].astype(F32)
    ys = jnp.concatenate([ys_ref[o] for o in range(SSM_N_OCT)], axis=-1).astype(BF16)
    glu = _dot(ys, wglu_ref[:, :D_MODEL]) * _sigmoid(_dot(ys, wglu_ref[:, D_MODEL:]))
    merged = gate(0) * glu

    xq = _dot(un, wpost_ref[:, :XATTN_WIDTH]).astype(BF16)
    heads = []
    for h in range(XATTN_HEADS):
        cols = slice(h * XATTN_HEAD_DIM, (h + 1) * XATTN_HEAD_DIM)
        s = _dot_nt(xq[:, cols], mk_ref[0, :, cols]) * (XATTN_HEAD_DIM ** -0.5)
        p = jnp.exp(s - jnp.max(s, axis=-1, keepdims=True))
        heads.append(_dot(p.astype(BF16), mv_ref[0, :, cols]) / jnp.sum(p, axis=-1, keepdims=True))
    xo = jnp.concatenate(heads, axis=-1).astype(BF16)
    merged = merged + gate(2) * _dot(xo, wxu_ref[...])

    merged = merged + gate(1) * _dot(att, wau_ref[...])

    out_ref[...] = x + _dot(merged.astype(BF16), wout_ref[...])


def _merge(x, y_rows, attn_outs, mk, mv, norm_g, w_post, w_glu, w_au, w_xu, w_out, seqlen):
    n = x.shape[0]
    tiles_per_seq = seqlen // TM_MERGE
    rows = lambda w: pl.BlockSpec((TM_MERGE, w), lambda i: (i, 0))
    mem = pl.BlockSpec((1, MEM_LEN, XATTN_WIDTH), lambda i: (i // tiles_per_seq, 0, 0))
    attn_specs = [pl.BlockSpec((1, d, TM_MERGE // d, 2 * PATTERN_WIDTH),
                               lambda i: (i // tiles_per_seq, 0, i % tiles_per_seq, 0))
                  for _, d in ATTN_PATTERNS]
    slabs = lambda k: pltpu.VMEM((k, TM_MERGE, LANES), F32)
    return pl.pallas_call(
        _merge_kernel,
        grid=(n // TM_MERGE,),
        in_specs=[rows(D_MODEL), pl.BlockSpec((TM_MERGE // SSM_CHUNK, SSM_ROW), lambda i: (i, 0))] + attn_specs
                 + [mem, mem, _resident((1, D_MODEL)), _resident((D_MODEL, POST_WIDTH)),
                    _resident((SSM_WIDTH, 2 * D_MODEL)), _resident((PATTERN_WIDTH, D_MODEL)),
                    _resident((XATTN_WIDTH, D_MODEL)), _resident((D_MODEL, D_MODEL))],
        out_specs=rows(D_MODEL),
        out_shape=jax.ShapeDtypeStruct((n, D_MODEL), F32),
        scratch_shapes=[slabs(SSM_N_OCT), slabs(2 * PATTERN_WIDTH // LANES), slabs(2 * PATTERN_WIDTH // LANES)],
        compiler_params=_params(1),
        name="merge",
    )(x, y_rows, *attn_outs, mk, mv, norm_g.reshape(1, D_MODEL), w_post, w_glu, w_au, w_xu, w_out)


def _layer(x, mem, rel_table, ffn1_norm, ffn1_w_in, ffn1_w_down, mix_norm, w_in,
           ssm_a_re, ssm_a_im, ssm_log_dt, ssm_b_re, ssm_b_im, ssm_c_re, ssm_c_im, ssm_d, ssm_w_glu,
           attn_w_up, mem_norm, xattn_w_kv, xattn_w_up, w_out, ffn2_norm, ffn2_w_in, ffn2_w_down,
           final_norm, bsz, seqlen):
    cast = [(w_in, (PRE_WIDTH, POST_WIDTH)), (ssm_w_glu, (2 * D_MODEL,)), (attn_w_up, (D_MODEL,)),
            (xattn_w_kv, (2 * XATTN_WIDTH,)), (xattn_w_up, (D_MODEL,)), (w_out, (D_MODEL,)),
            (ffn2_w_in, (2 * D_FF,)), (ffn2_w_down, (D_MODEL,))]
    x, (w_pre, w_post, w_glu, w_au, w_kv, w_xu, w_o, w2_in, w2_down) = _ffn(
        x, ffn1_norm, ffn1_w_in.astype(BF16), ffn1_w_down.astype(BF16), cast=cast)
    u_rows, qkv = _proj(x, mix_norm, w_pre, bsz, seqlen)
    compact = _ssm_compact(ssm_a_re, ssm_a_im, ssm_log_dt, ssm_b_re, ssm_b_im, ssm_c_re, ssm_c_im, ssm_d)
    y_rows = _ssm(u_rows, compact, bsz)
    attn_outs = [_dilated_attention(qkv[g], rel_table, g) for g in range(N_PATTERNS)]
    mk, mv = _memkv(mem.reshape(bsz * MEM_LEN, D_MODEL), mem_norm, w_kv)
    mk = mk.reshape(bsz, MEM_LEN, XATTN_WIDTH)
    mv = mv.reshape(bsz, MEM_LEN, XATTN_WIDTH)
    x = _merge(x, y_rows, attn_outs, mk, mv, mix_norm, w_post, w_glu, w_au, w_xu, w_o, seqlen)
    return _ffn(x, ffn2_norm, w2_in, w2_down, final_g=final_norm)[0]


def kernel(x, mem, ffn1_norm, ffn1_w_in, ffn1_w_down, mix_norm, w_in, ssm_a_re, ssm_a_im, ssm_log_dt,
           ssm_b_re, ssm_b_im, ssm_c_re, ssm_c_im, ssm_d, ssm_w_glu, rel_table, attn_w_up, mem_norm,
           xattn_w_kv, xattn_w_up, w_out, ffn2_norm, ffn2_w_in, ffn2_w_down, final_norm):
    bsz, seqlen, _ = x.shape
    assert ffn1_norm.shape[0] == 1, "single-layer trunk"
    assert x.shape[2] == D_MODEL and mem.shape == (bsz, MEM_LEN, D_MODEL)
    assert w_in.shape[1:] == (D_MODEL, PRE_WIDTH + POST_WIDTH) and ffn1_w_in.shape[1:] == (D_MODEL, 2 * D_FF)
    assert ssm_b_re.shape[1:] == (SSM_WIDTH // SSM_GROUP, SSM_STATE, SSM_GROUP)
    assert rel_table.shape == (REL_BUCKETS, N_PATTERNS * HEADS_PER_PATTERN)
    assert seqlen % ATTN_STEP_ROWS == 0 and (seqlen // ATTN_PATTERNS[-1][1]) % SUB_Q == 0
    assert seqlen % TM_PROJ == 0 and seqlen % TM_MERGE == 0 and (bsz * seqlen) % TM_FFN == 0
    assert bsz % SSM_SEQ_PER_STEP == 0
    h = _layer(x.reshape(bsz * seqlen, D_MODEL), mem, rel_table, ffn1_norm[0], ffn1_w_in[0], ffn1_w_down[0],
               mix_norm[0], w_in[0], ssm_a_re[0], ssm_a_im[0], ssm_log_dt[0], ssm_b_re[0], ssm_b_im[0],
               ssm_c_re[0], ssm_c_im[0], ssm_d[0], ssm_w_glu[0], attn_w_up[0], mem_norm[0], xattn_w_kv[0],
               xattn_w_up[0], w_out[0], ffn2_norm[0], ffn2_w_in[0], ffn2_w_down[0], final_norm,
               bsz, seqlen)
    return h.reshape(bsz, seqlen, D_MODEL)
```

```python
import functools

import numpy as np
import jax
import jax.numpy as jnp
from jax import lax
from jax.experimental import pallas as pl
from jax.experimental.pallas import tpu as pltpu

F32 = jnp.float32
BF16 = jnp.bfloat16

LANES = 128
BF16_SUBLANES = 16
MXU_TILE = 256

D_MODEL = 1024
D_FF = 2816
EPS = 1e-6

HEAD_DIM = 64
ATTN_PATTERNS = ((128, 1), (512, 4), (2048, 16))
N_PATTERNS = 3
HEADS_PER_PATTERN = 4
ATTN_WIDTH = N_PATTERNS * HEADS_PER_PATTERN * HEAD_DIM
PATTERN_WIDTH = HEADS_PER_PATTERN * HEAD_DIM
WINDOW_KEYS = 128
REL_BUCKETS = 32
REL_MAX_DIST = 2048

SSM_GROUP = 16
SSM_STATE = 64
SSM_WIDTH = 512
SSM_CHUNK = 16
SSM_OCT = LANES // SSM_GROUP
SSM_N_OCT = SSM_WIDTH // LANES
SSM_OCT_IN = SSM_CHUNK * LANES
SSM_OCT_STATE = SSM_OCT * SSM_STATE
SSM_ROW = SSM_CHUNK * SSM_WIDTH

MEM_LEN = 256
XATTN_HEADS = 4
XATTN_HEAD_DIM = 128
XATTN_WIDTH = XATTN_HEADS * XATTN_HEAD_DIM

PRE_WIDTH = SSM_WIDTH + 3 * ATTN_WIDTH
POST_WIDTH = XATTN_WIDTH + 3 * D_MODEL

NEG_BIG = -1e30

VMEM_LIMIT_BYTES = 56 * 1024 * 1024

TM_FFN = 1024
FFN_ROW_PARTS = 4
FFN_CHUNKS = ((0, 1536), (1536, D_FF))
TM_PROJ = 1024
PROJ_HALVES = 2
PROJ_IN_BUFFERS = 3
SPLIT_STRIDE = 4
TM_MERGE = 512
ATTN_STEP_ROWS = 4096
SSM_SEQ_PER_STEP = 2
SUB_Q = 128


def _rmsnorm(x, g):
    return x * lax.rsqrt(jnp.mean(x * x, axis=-1, keepdims=True) + EPS) * g


def _sigmoid(x):
    return 0.5 * jnp.tanh(0.5 * x) + 0.5


def _dot(a, b):
    return jnp.dot(a, b, preferred_element_type=F32)


def _dot_nt(a, b):
    return lax.dot_general(a, b, (((1,), (1,)), ((), ())), preferred_element_type=F32)


def _resident(shape):
    nd = len(shape)
    return pl.BlockSpec(shape, lambda *_: (0,) * nd, pipeline_mode=pl.Buffered(1))


def _params(n_axes):
    return pltpu.CompilerParams(dimension_semantics=("arbitrary",) * n_axes,
                                vmem_limit_bytes=VMEM_LIMIT_BYTES)


def _ffn_kernel(x_ref, g_ref, win_ref, wdn_ref, *rest, final_norm, n_cast):
    rest = list(rest)
    fg_ref = rest.pop(0) if final_norm else None
    cast_src, o_ref, cast_dst = rest[:n_cast], rest[n_cast], rest[n_cast + 1:]
    for part in range(FFN_ROW_PARTS):
        rows = slice(part * TM_FFN // FFN_ROW_PARTS, (part + 1) * TM_FFN // FFN_ROW_PARTS)
        x = x_ref[rows, :]
        un = _rmsnorm(x, g_ref[...]).astype(BF16)
        acc = None
        for lo, hi in FFN_CHUNKS:
            a = _dot(un, win_ref[:, lo:hi])
            b = _dot(un, win_ref[:, D_FF + lo:D_FF + hi])
            h = (a * _sigmoid(a) * b).astype(BF16)
            part = _dot(h, wdn_ref[lo:hi, :])
            acc = part if acc is None else acc + part
        y = x + 0.5 * acc
        if final_norm:
            y = _rmsnorm(y, fg_ref[...])
        o_ref[rows, :] = y

    dst = iter(cast_dst)
    for src in cast_src:
        lo = 0
        while lo < src.shape[1]:
            out = next(dst)
            out[...] = src[:, lo:lo + out.shape[1]].astype(BF16)
            lo += out.shape[1]


def _cast_rows(n_rows, n_steps):
    rows = BF16_SUBLANES
    while n_rows % rows or n_rows // rows > n_steps:
        rows += BF16_SUBLANES
    return rows


def _ffn(x, norm_g, w_in, w_down, final_g=None, cast=()):
    n = x.shape[0]
    n_steps = n // TM_FFN
    row = pl.BlockSpec((TM_FFN, D_MODEL), lambda i: (i, 0))
    in_specs = [row, _resident((1, D_MODEL)), _resident((D_MODEL, 2 * D_FF)), _resident((D_FF, D_MODEL))]
    args = [x, norm_g.reshape(1, D_MODEL), w_in, w_down]
    if final_g is not None:
        in_specs.append(_resident((1, D_MODEL)))
        args.append(final_g.reshape(1, D_MODEL))
    out_specs, out_shapes = [row], [jax.ShapeDtypeStruct((n, D_MODEL), F32)]
    for w, widths in cast:
        assert sum(widths) == w.shape[1]
        rows = _cast_rows(w.shape[0], n_steps)
        last = w.shape[0] // rows - 1
        block = lambda width, last=last, rows=rows: pl.BlockSpec((rows, width), lambda i: (jnp.minimum(i, last), 0))
        in_specs.append(block(w.shape[1]))
        args.append(w)
        out_specs += [block(width) for width in widths]
        out_shapes += [jax.ShapeDtypeStruct((w.shape[0], width), BF16) for width in widths]
    outs = pl.pallas_call(
        functools.partial(_ffn_kernel, final_norm=final_g is not None, n_cast=len(cast)),
        grid=(n_steps,),
        in_specs=in_specs,
        out_specs=out_specs,
        out_shape=out_shapes,
        compiler_params=_params(1),
        name="ffn_final" if final_g is not None else "ffn",
    )(*args)
    return outs[0], outs[1:]


def _rows_by_residue(slab_ref, j, stage_ref, slot, n_rows, d):
    if d == 1:
        return [slab_ref[j]]
    if d == SPLIT_STRIDE:
        return [slab_ref[j, pl.ds(r, n_rows // d, stride=d), :] for r in range(d)]
    assert d == SPLIT_STRIDE * SPLIT_STRIDE
    for low in range(SPLIT_STRIDE):
        stage_ref[slot, low] = slab_ref[j, pl.ds(low, n_rows // SPLIT_STRIDE, stride=SPLIT_STRIDE), :]
    return [stage_ref[slot, r % SPLIT_STRIDE, pl.ds(r // SPLIT_STRIDE, n_rows // d, stride=SPLIT_STRIDE), :]
            for r in range(d)]


def _proj_kernel(x_ref, g_ref, w_ref, u_ref, *rest):
    qkv_refs, slabs = rest[:3 * N_PATTERNS], rest[3 * N_PATTERNS:]
    n_rows = TM_PROJ // PROJ_HALVES
    per_half = len(slabs) // PROJ_HALVES
    for half in range(PROJ_HALVES):
        base = half * n_rows
        su_ref, sq_ref, sk_ref, sv_ref, stage_ref = slabs[per_half * half:per_half * (half + 1)]
        un = _rmsnorm(x_ref[base:base + n_rows, :], g_ref[...]).astype(BF16)
        full = _dot(un, w_ref[...])
        slot = 0

        u = full[:, :SSM_WIDTH]
        chunks = slice(base // SSM_CHUNK, (base + n_rows) // SSM_CHUNK)
        for o in range(SSM_N_OCT):
            su_ref[o] = u[:, o * LANES:(o + 1) * LANES]
            for t, rows in enumerate(_rows_by_residue(su_ref, o, stage_ref, slot, n_rows, SSM_CHUNK)):
                lo = o * SSM_OCT_IN + t * LANES
                u_ref[chunks, lo:lo + LANES] = rows.astype(BF16)
            slot += 1

        for which, scr in enumerate((sq_ref, sk_ref, sv_ref)):
            lo = SSM_WIDTH + which * ATTN_WIDTH
            res = full[:, lo:lo + ATTN_WIDTH]
            if which == 0:
                res = res * (HEAD_DIM ** -0.5)
            for j in range(ATTN_WIDTH // LANES):
                scr[j] = res[:, j * LANES:(j + 1) * LANES]
            for g, (_, d) in enumerate(ATTN_PATTERNS):
                out = qkv_refs[which * N_PATTERNS + g]
                for jj in range(PATTERN_WIDTH // LANES):
                    j = g * (PATTERN_WIDTH // LANES) + jj
                    for r, rows in enumerate(_rows_by_residue(scr, j, stage_ref, slot, n_rows, d)):
                        out[0, r, base // d:(base + n_rows) // d, jj * LANES:(jj + 1) * LANES] = rows.astype(BF16)
                    slot += d == SPLIT_STRIDE * SPLIT_STRIDE


def _proj(x, norm_g, w_pre, bsz, seqlen):
    n = x.shape[0]
    tiles_per_seq = seqlen // TM_PROJ
    rows = lambda w: pl.BlockSpec((TM_PROJ, w), lambda i: (i, 0))
    qkv_specs, qkv_shapes = [], []
    for _ in range(3):
        for _, d in ATTN_PATTERNS:
            qkv_specs.append(pl.BlockSpec((1, d, TM_PROJ // d, PATTERN_WIDTH),
                                          lambda i: (i // tiles_per_seq, 0, i % tiles_per_seq, 0)))
            qkv_shapes.append(jax.ShapeDtypeStruct((bsz, d, seqlen // d, PATTERN_WIDTH), BF16))
    half_rows = TM_PROJ // PROJ_HALVES
    slabs = lambda k: pltpu.VMEM((k, half_rows, LANES), F32)
    n_staged = SSM_N_OCT + 3 * PATTERN_WIDTH // LANES
    stage = pltpu.VMEM((n_staged, SPLIT_STRIDE, half_rows // SPLIT_STRIDE, LANES), F32)
    n_out = 1 + 3 * N_PATTERNS
    out_specs = [pl.BlockSpec((TM_PROJ // SSM_CHUNK, SSM_ROW), lambda i: (i, 0))] + qkv_specs

    def streamed(x_hbm, g_ref, w_ref, *rest):
        out_hbm, scratch = rest[:n_out], rest[n_out:]
        body = lambda x_ref, *out_refs: _proj_kernel(x_ref, g_ref, w_ref, *out_refs, *scratch)
        pltpu.emit_pipeline(
            body, grid=(n // TM_PROJ,),
            in_specs=[pl.BlockSpec((TM_PROJ, D_MODEL), lambda i: (i, 0), pipeline_mode=pl.Buffered(PROJ_IN_BUFFERS))],
            out_specs=out_specs,
        )(x_hbm, *out_hbm)

    whole = lambda shape: pl.BlockSpec(shape, lambda: (0,) * len(shape), memory_space=pltpu.VMEM)
    outs = pl.pallas_call(
        streamed,
        in_specs=[pl.BlockSpec(memory_space=pl.ANY), whole((1, D_MODEL)), whole((D_MODEL, PRE_WIDTH))],
        out_specs=[pl.BlockSpec(memory_space=pl.ANY)] * n_out,
        out_shape=[jax.ShapeDtypeStruct((n // SSM_CHUNK, SSM_ROW), BF16)] + qkv_shapes,
        scratch_shapes=([slabs(SSM_N_OCT)] + [slabs(ATTN_WIDTH // LANES)] * 3 + [stage]) * PROJ_HALVES,
        compiler_params=pltpu.CompilerParams(vmem_limit_bytes=VMEM_LIMIT_BYTES),
        name="proj",
    )(x, norm_g.reshape(1, D_MODEL), w_pre)
    u = outs[0]
    q, k, v = (outs[1 + i * N_PATTERNS:1 + (i + 1) * N_PATTERNS] for i in range(3))
    return u, q, k, v


def _ssm_compact(a_re, a_im, log_dt, b_re, b_im, c_re, c_im, d_skip):
    t_len = SSM_CHUNK
    dt = jnp.exp(log_dt)[:, None]
    mag = jnp.exp(a_re * dt)
    ang = a_im * dt
    abar_re = mag * jnp.cos(ang)
    abar_im = mag * jnp.sin(ang)
    nr = abar_re - 1.0
    ni = abar_im
    den = a_re * a_re + a_im * a_im
    coef_re = (nr * a_re + ni * a_im) / den
    coef_im = (ni * a_re - nr * a_im) / den
    bbar_re = coef_re[..., None] * b_re - coef_im[..., None] * b_im
    bbar_im = coef_re[..., None] * b_im + coef_im[..., None] * b_re
    bt_re = jnp.swapaxes(bbar_re, -1, -2)
    bt_im = jnp.swapaxes(bbar_im, -1, -2)
    pw_re = [jnp.ones_like(abar_re)]
    pw_im = [jnp.zeros_like(abar_re)]
    for _ in range(t_len):
        pr, pi = pw_re[-1], pw_im[-1]
        pw_re.append(pr * abar_re - pi * abar_im)
        pw_im.append(pr * abar_im + pi * abar_re)
    rev_re = jnp.stack(pw_re[t_len - 1::-1])[:, :, None, :]
    rev_im = jnp.stack(pw_im[t_len - 1::-1])[:, :, None, :]
    pw_re = jnp.stack(pw_re)
    pw_im = jnp.stack(pw_im)
    n_oct, oct_, grp, st = SSM_N_OCT, SSM_OCT, SSM_GROUP, SSM_STATE

    def per_octet(re, im):
        both = jnp.concatenate([re, im], axis=-1)
        n = both.shape[0]
        return both.reshape(n, n_oct, oct_ * grp, 2 * st).transpose(1, 0, 2, 3).reshape(n_oct, n * LANES, 2 * st)

    into_c = per_octet(rev_re * bt_re[None] - rev_im * bt_im[None], rev_re * bt_im[None] + rev_im * bt_re[None])
    cp_re = c_re[None] * pw_re[:, :, None, :] - c_im[None] * pw_im[:, :, None, :]
    cp_im = c_re[None] * pw_im[:, :, None, :] + c_im[None] * pw_re[:, :, None, :]
    outof_c = per_octet(cp_re, -cp_im)
    decay = jnp.stack([pw_re[t_len].reshape(n_oct, SSM_OCT_STATE), pw_im[t_len].reshape(n_oct, SSM_OCT_STATE)],
                      axis=1)
    skip = d_skip.reshape(n_oct, 1, LANES)
    return into_c.astype(BF16), outof_c.astype(BF16), decay, skip


def _ssm_expanders():
    row = np.arange(LANES)
    col = np.arange(2 * SSM_OCT_STATE)
    spread = (row[:, None] // SSM_STATE == col[None, :] // SSM_OCT_STATE) & (
        row[:, None] % SSM_STATE == col[None, :] % SSM_STATE)
    keep = row[:, None] // SSM_GROUP == (col[None, :] % SSM_OCT_STATE) // SSM_STATE
    return jnp.asarray(spread, BF16), jnp.asarray(keep, F32)


def _ssm_kernel(x_ref, intoc_ref, outofc_ref, decay_ref, skip_ref, spread_ref, keep_ref, y_ref,
                toep_ref, into_ref, outof_ref, loc_ref, prev_ref, part_ref, *, n_chunks, n_seq):
    blk = lambda i: slice(i * LANES, (i + 1) * LANES)

    @pl.when(pl.program_id(1) == 0)
    def _build_octet_matrices():
        spread, keep = spread_ref[...], keep_ref[...]
        for s in range(SSM_CHUNK):
            into_ref[blk(s), :] = (_dot(intoc_ref[0, blk(s), :], spread) * keep).astype(BF16)
        for j in range(SSM_CHUNK + 1):
            outof_ref[blk(j), :] = (_dot(outofc_ref[0, blk(j), :], spread) * keep).astype(BF16)
        bbar = into_ref[blk(SSM_CHUNK - 1), :]
        diag = (lax.broadcasted_iota(jnp.int32, (LANES, LANES), 0)
                == lax.broadcasted_iota(jnp.int32, (LANES, LANES), 1))
        zeros = jnp.zeros((LANES, LANES), BF16)
        for j in range(SSM_CHUNK):
            lag = _dot_nt(bbar, outof_ref[blk(j), :])
            if j == 0:
                lag = lag + jnp.where(diag, skip_ref[0], 0.0)
            lag = lag.astype(BF16)
            for s in range(SSM_CHUNK - j):
                toep_ref[blk(s), blk(s + j)] = lag
            if j > 0:
                for s in range(j, SSM_CHUNK):
                    toep_ref[blk(s), blk(s - j)] = zeros

    loc_ref[...] = _dot(x_ref[...], into_ref[...])
    n_tiles = SSM_OCT_IN // MXU_TILE
    early = n_tiles - part_ref.shape[1] // MXU_TILE
    in_chunk = lambda j: _dot(x_ref[:, :(j + 1) * MXU_TILE], toep_ref[:(j + 1) * MXU_TILE, j * MXU_TILE:(j + 1) * MXU_TILE])
    for j in range(early, n_tiles):
        part_ref[:, (j - early) * MXU_TILE:(j - early + 1) * MXU_TILE] = in_chunk(j)
    ar = decay_ref[0, 0:1, :]
    ai = decay_ref[0, 1:2, :]

    def step(c, carry):
        new = []
        for q in range(n_seq):
            s_re, s_im = carry[2 * q], carry[2 * q + 1]
            row = pl.ds(q * n_chunks + c, 1)
            prev_ref[row, :SSM_OCT_STATE] = s_re
            prev_ref[row, SSM_OCT_STATE:] = s_im
            new.append(ar * s_re - ai * s_im + loc_ref[row, :SSM_OCT_STATE])
            new.append(ar * s_im + ai * s_re + loc_ref[row, SSM_OCT_STATE:])
        return tuple(new)

    zero = jnp.zeros((1, SSM_OCT_STATE), F32)
    lax.fori_loop(0, n_chunks, step, (zero,) * (2 * n_seq), unroll=True)
    prev = prev_ref[...].astype(BF16)
    for j in range(n_tiles):
        lo, hi = j * MXU_TILE, (j + 1) * MXU_TILE
        within = in_chunk(j) if j < early else part_ref[:, (j - early) * MXU_TILE:(j - early + 1) * MXU_TILE]
        y = within + _dot_nt(prev, outof_ref[lo + LANES:hi + LANES, :])
        y_ref[:, lo:hi] = jax.nn.gelu(y).astype(BF16)


def _ssm(u_rows, compact, bsz):
    into_c, outof_c, decay, skip = compact
    spread, keep = _ssm_expanders()
    n_chunks = u_rows.shape[0] // bsz
    n_seq = SSM_SEQ_PER_STEP
    per_oct = lambda *shape: pl.BlockSpec((1,) + shape, lambda o, b: (o,) + (0,) * len(shape))
    rows = pl.BlockSpec((n_seq * n_chunks, SSM_OCT_IN), lambda o, b: (b, o))
    wide = 2 * SSM_OCT_STATE
    return pl.pallas_call(
        functools.partial(_ssm_kernel, n_chunks=n_chunks, n_seq=n_seq),
        grid=(SSM_N_OCT, bsz // n_seq),
        in_specs=[rows, per_oct(SSM_OCT_IN, LANES), per_oct(SSM_OCT_IN + LANES, LANES),
                  per_oct(2, SSM_OCT_STATE), per_oct(1, LANES), _resident((LANES, wide)), _resident((LANES, wide))],
        out_specs=rows,
        out_shape=jax.ShapeDtypeStruct(u_rows.shape, BF16),
        scratch_shapes=[pltpu.VMEM((SSM_OCT_IN, SSM_OCT_IN), BF16),
                        pltpu.VMEM((SSM_OCT_IN, wide), BF16),
                        pltpu.VMEM((SSM_OCT_IN + LANES, wide), BF16),
                        pltpu.VMEM((n_seq * n_chunks, wide), F32),
                        pltpu.VMEM((n_seq * n_chunks, wide), F32),
                        pltpu.VMEM((n_seq * n_chunks, SSM_OCT_IN // 2), F32)],
        compiler_params=_params(2),
        name="ssm",
    )(u_rows, into_c, outof_c, decay, skip, spread, keep)


def _t5_buckets(dist):
    dist = np.asarray(dist, np.int32)
    max_exact = REL_BUCKETS // 2
    safe = np.maximum(dist, 1).astype(np.float32)
    large = max_exact + (np.log(safe / max_exact) / np.log(REL_MAX_DIST / max_exact)
                         * (REL_BUCKETS - max_exact)).astype(np.int32)
    large = np.minimum(large, REL_BUCKETS - 1)
    return np.where(dist < max_exact, dist, large).astype(np.int32)


def _band_bias(rel_table, pattern):
    dilation = ATTN_PATTERNS[pattern][1]
    buckets = _t5_buckets(np.arange(WINDOW_KEYS + 1) * dilation)
    table = rel_table[:, pattern * HEADS_PER_PATTERN:(pattern + 1) * HEADS_PER_PATTERN].T
    runs = []
    for b in np.unique(buckets):
        runs.append(jnp.broadcast_to(table[:, b:b + 1], (HEADS_PER_PATTERN, int(np.sum(buckets == b)))))
    per_step = jnp.concatenate(runs, axis=1)
    period = 3 * SUB_Q
    masked = lambda n: jnp.full((HEADS_PER_PATTERN, n), NEG_BIG, F32)
    w = jnp.concatenate([masked(SUB_Q - 1), per_step[:, ::-1], masked(period - SUB_Q - WINDOW_KEYS)], axis=1)
    skew = jnp.tile(w, (1, SUB_Q))[:, :SUB_Q * (period - 1)].reshape(HEADS_PER_PATTERN, SUB_Q, period - 1)
    bias = skew[:, :, SUB_Q - 1:3 * SUB_Q - 1]
    first = jnp.where((np.arange(2 * SUB_Q) >= SUB_Q)[None, None, :], bias, NEG_BIG)
    pairs = lambda b: b.reshape(HEADS_PER_PATTERN // 2, 2 * SUB_Q, 2 * SUB_Q)
    return jnp.stack([pairs(bias), pairs(first)])


def _attn_kernel(q_ref, kc_ref, kh_ref, vc_ref, vh_ref, bias_ref, o_ref):
    n_res, tq = q_ref.shape[1], q_ref.shape[2]
    first = jnp.where(pl.program_id(2) == 0, 1, 0)
    lane = lax.broadcasted_iota(jnp.int32, (SUB_Q, 2 * HEAD_DIM), 1)
    low_half = lane < HEAD_DIM
    stat_is_max = lax.broadcasted_iota(jnp.int32, (2 * SUB_Q, 2 * HEAD_DIM), 1) % HEAD_DIM < HEAD_DIM // 2
    for r in range(n_res):
        for j in range(tq // SUB_Q):
            rows = slice(j * SUB_Q, (j + 1) * SUB_Q)
            q = q_ref[0, r, rows, :]
            if j == 0:
                k = jnp.concatenate([kh_ref[0, r], kc_ref[0, r, rows, :]], axis=0)
                v = jnp.concatenate([vh_ref[0, r], vc_ref[0, r, rows, :]], axis=0)
            else:
                k = kc_ref[0, r, (j - 1) * SUB_Q:(j + 1) * SUB_Q, :]
                v = vc_ref[0, r, (j - 1) * SUB_Q:(j + 1) * SUB_Q, :]
            for pair in range(HEADS_PER_PATTERN // 2):
                cols = slice(pair * 2 * HEAD_DIM, (pair + 1) * 2 * HEAD_DIM)
                q2, k2, v2 = q[:, cols], k[:, cols], v[:, cols]
                zero = jnp.zeros_like(q2)
                qq = jnp.concatenate([jnp.where(low_half, q2, zero), jnp.where(low_half, zero, q2)], axis=0)
                s = _dot_nt(qq, k2) + (bias_ref[first, pair] if j == 0 else bias_ref[0, pair])
                m = jnp.max(s, axis=-1, keepdims=True)
                p = jnp.exp(s - m)
                l = jnp.sum(p, axis=-1, keepdims=True)
                o = _dot(p.astype(BF16), v2)
                wide = (2 * SUB_Q, 2 * HEAD_DIM)
                stat = jnp.where(stat_is_max, jnp.broadcast_to(m, wide), jnp.broadcast_to(l, wide))
                o_ref[0, r, rows, cols] = jnp.where(low_half, o[:SUB_Q], o[SUB_Q:])
                o_ref[0, r, rows, PATTERN_WIDTH + pair * 2 * HEAD_DIM:PATTERN_WIDTH + (pair + 1) * 2 * HEAD_DIM] = (
                    jnp.where(low_half, stat[:SUB_Q], stat[SUB_Q:]))


def _dilated_attention(q, k, v, rel_table, pattern):
    bsz, d, res_len, _ = q.shape
    tq = min(res_len, ATTN_STEP_ROWS)
    n_res = ATTN_STEP_ROWS // tq
    cur = pl.BlockSpec((1, n_res, tq, PATTERN_WIDTH), lambda b, r, i: (b, r, i, 0))
    halo = pl.BlockSpec((1, n_res, SUB_Q, PATTERN_WIDTH),
                        lambda b, r, i: (b, r, jnp.maximum(i * (tq // SUB_Q) - 1, 0), 0))
    return pl.pallas_call(
        _attn_kernel,
        grid=(bsz, d // n_res, res_len // tq),
        in_specs=[cur, cur, halo, cur, halo, _resident((2, HEADS_PER_PATTERN // 2, 2 * SUB_Q, 2 * SUB_Q))],
        out_specs=pl.BlockSpec((1, n_res, tq, 2 * PATTERN_WIDTH), lambda b, r, i: (b, r, i, 0)),
        out_shape=jax.ShapeDtypeStruct((bsz, d, res_len, 2 * PATTERN_WIDTH), F32),
        compiler_params=_params(3),
        name=f"attn_d{d}",
    )(q, k, k, v, v, _band_bias(rel_table, pattern))


def _memkv_kernel(mem_ref, g_ref, w_ref, k_ref, v_ref):
    mn = _rmsnorm(mem_ref[...], g_ref[...]).astype(BF16)
    k_ref[...] = _dot(mn, w_ref[:, :XATTN_WIDTH]).astype(BF16)
    v_ref[...] = _dot(mn, w_ref[:, XATTN_WIDTH:]).astype(BF16)


def _memkv(mem2d, norm_g, w_kv):
    rows = mem2d.shape[0]
    full = lambda *shape: pl.BlockSpec(shape, lambda i: (0,) * len(shape))
    return pl.pallas_call(
        _memkv_kernel,
        grid=(1,),
        in_specs=[full(rows, D_MODEL), full(1, D_MODEL), full(D_MODEL, 2 * XATTN_WIDTH)],
        out_specs=[full(rows, XATTN_WIDTH), full(rows, XATTN_WIDTH)],
        out_shape=[jax.ShapeDtypeStruct((rows, XATTN_WIDTH), BF16)] * 2,
        compiler_params=_params(1),
        name="memkv",
    )(mem2d, norm_g.reshape(1, D_MODEL), w_kv)


def _merge_kernel(x_ref, y_ref, o0_ref, o1_ref, o2_ref, mk_ref, mv_ref, g_ref, wpost_ref, wglu_ref,
                  wau_ref, wxu_ref, wout_ref, out_ref, ys_ref, s1_ref, s2_ref):
    x = x_ref[...]
    un = _rmsnorm(x, g_ref[...]).astype(BF16)

    def gate(index):
        lo = XATTN_WIDTH + index * D_MODEL
        return _sigmoid(_dot(un, wpost_ref[:, lo:lo + D_MODEL]))

    slabs = 2 * PATTERN_WIDTH // LANES
    for o_ref, s_ref in ((o1_ref, s1_ref), (o2_ref, s2_ref)):
        d = o_ref.shape[1]
        for r in range(d):
            for j in range(slabs):
                s_ref[j, pl.ds(r, TM_MERGE // d, stride=d), :] = o_ref[0, r, :, j * LANES:(j + 1) * LANES]
    half = slabs // 2
    is_max = lax.broadcasted_iota(jnp.int32, (TM_MERGE, LANES), 1) % HEAD_DIM < HEAD_DIM // 2

    def unpack(stat):
        return (jnp.where(is_max, stat, pltpu.roll(stat, HEAD_DIM // 2, axis=1)),
                jnp.where(is_max, pltpu.roll(stat, LANES - HEAD_DIM // 2, axis=1), stat))

    def pattern(read):
        out = jnp.concatenate([read(j) for j in range(half)], axis=-1)
        stats = [unpack(read(j)) for j in range(half, slabs)]
        return (out, jnp.concatenate([m for m, _ in stats], axis=-1),
                jnp.concatenate([l for _, l in stats], axis=-1))

    parts = [pattern(lambda j: o0_ref[0, 0, :, j * LANES:(j + 1) * LANES]),
             pattern(lambda j: s1_ref[j]), pattern(lambda j: s2_ref[j])]
    top = jnp.maximum(jnp.maximum(parts[0][1], parts[1][1]), parts[2][1])
    ws = [jnp.exp(m - top) for _, m, _ in parts]
    att = ((ws[0] * parts[0][0] + ws[1] * parts[1][0] + ws[2] * parts[2][0])
           / (ws[0] * parts[0][2] + ws[1] * parts[1][2] + ws[2] * parts[2][2]))
    att = att.astype(BF16)

    chunk_rows = TM_MERGE // SSM_CHUNK
    for o in range(SSM_N_OCT):
        for t in range(SSM_CHUNK):
            lo = o * SSM_OCT_IN + t * LANES
            ys_ref[o, pl.ds(t, chunk_rows, stride=SSM_CHUNK), :] = y_ref[:, lo:lo + LANES].astype(F32)
    ys = jnp.concatenate([ys_ref[o] for o in range(SSM_N_OCT)], axis=-1).astype(BF16)
    glu = _dot(ys, wglu_ref[:, :D_MODEL]) * _sigmoid(_dot(ys, wglu_ref[:, D_MODEL:]))
    merged = gate(0) * glu

    xq = _dot(un, wpost_ref[:, :XATTN_WIDTH]).astype(BF16)
    heads = []
    for h in range(XATTN_HEADS):
        cols = slice(h * XATTN_HEAD_DIM, (h + 1) * XATTN_HEAD_DIM)
        s = _dot_nt(xq[:, cols], mk_ref[0, :, cols]) * (XATTN_HEAD_DIM ** -0.5)
        p = jnp.exp(s - jnp.max(s, axis=-1, keepdims=True))
        heads.append(_dot(p.astype(BF16), mv_ref[0, :, cols]) / jnp.sum(p, axis=-1, keepdims=True))
    xo = jnp.concatenate(heads, axis=-1).astype(BF16)
    merged = merged + gate(2) * _dot(xo, wxu_ref[...])

    merged = merged + gate(1) * _dot(att, wau_ref[...])

    out_ref[...] = x + _dot(merged.astype(BF16), wout_ref[...])


def _merge(x, y_rows, attn_outs, mk, mv, norm_g, w_post, w_glu, w_au, w_xu, w_out, seqlen):
    n = x.shape[0]
    tiles_per_seq = seqlen // TM_MERGE
    rows = lambda w: pl.BlockSpec((TM_MERGE, w), lambda i: (i, 0))
    mem = pl.BlockSpec((1, MEM_LEN, XATTN_WIDTH), lambda i: (i // tiles_per_seq, 0, 0))
    attn_specs = [pl.BlockSpec((1, d, TM_MERGE // d, 2 * PATTERN_WIDTH),
                               lambda i: (i // tiles_per_seq, 0, i % tiles_per_seq, 0))
                  for _, d in ATTN_PATTERNS]
    slabs = lambda k: pltpu.VMEM((k, TM_MERGE, LANES), F32)
    return pl.pallas_call(
        _merge_kernel,
        grid=(n // TM_MERGE,),
        in_specs=[rows(D_MODEL), pl.BlockSpec((TM_MERGE // SSM_CHUNK, SSM_ROW), lambda i: (i, 0))] + attn_specs
                 + [mem, mem, _resident((1, D_MODEL)), _resident((D_MODEL, POST_WIDTH)),
                    _resident((SSM_WIDTH, 2 * D_MODEL)), _resident((PATTERN_WIDTH, D_MODEL)),
                    _resident((XATTN_WIDTH, D_MODEL)), _resident((D_MODEL, D_MODEL))],
        out_specs=rows(D_MODEL),
        out_shape=jax.ShapeDtypeStruct((n, D_MODEL), F32),
        scratch_shapes=[slabs(SSM_N_OCT), slabs(2 * PATTERN_WIDTH // LANES), slabs(2 * PATTERN_WIDTH // LANES)],
        compiler_params=_params(1),
        name="merge",
    )(x, y_rows, *attn_outs, mk, mv, norm_g.reshape(1, D_MODEL), w_post, w_glu, w_au, w_xu, w_out)


def _layer(x, mem, rel_table, ffn1_norm, ffn1_w_in, ffn1_w_down, mix_norm, w_in,
           ssm_a_re, ssm_a_im, ssm_log_dt, ssm_b_re, ssm_b_im, ssm_c_re, ssm_c_im, ssm_d, ssm_w_glu,
           attn_w_up, mem_norm, xattn_w_kv, xattn_w_up, w_out, ffn2_norm, ffn2_w_in, ffn2_w_down,
           final_norm, bsz, seqlen):
    cast = [(w_in, (PRE_WIDTH, POST_WIDTH)), (ssm_w_glu, (2 * D_MODEL,)), (attn_w_up, (D_MODEL,)),
            (xattn_w_kv, (2 * XATTN_WIDTH,)), (xattn_w_up, (D_MODEL,)), (w_out, (D_MODEL,)),
            (ffn2_w_in, (2 * D_FF,)), (ffn2_w_down, (D_MODEL,))]
    x, (w_pre, w_post, w_glu, w_au, w_kv, w_xu, w_o, w2_in, w2_down) = _ffn(
        x, ffn1_norm, ffn1_w_in.astype(BF16), ffn1_w_down.astype(BF16), cast=cast)
    u_rows, q, k, v = _proj(x, mix_norm, w_pre, bsz, seqlen)
    compact = _ssm_compact(ssm_a_re, ssm_a_im, ssm_log_dt, ssm_b_re, ssm_b_im, ssm_c_re, ssm_c_im, ssm_d)
    y_rows = _ssm(u_rows, compact, bsz)
    attn_outs = [_dilated_attention(q[g], k[g], v[g], rel_table, g) for g in range(N_PATTERNS)]
    mk, mv = _memkv(mem.reshape(bsz * MEM_LEN, D_MODEL), mem_norm, w_kv)
    mk = mk.reshape(bsz, MEM_LEN, XATTN_WIDTH)
    mv = mv.reshape(bsz, MEM_LEN, XATTN_WIDTH)
    x = _merge(x, y_rows, attn_outs, mk, mv, mix_norm, w_post, w_glu, w_au, w_xu, w_o, seqlen)
    return _ffn(x, ffn2_norm, w2_in, w2_down, final_g=final_norm)[0]


def kernel(x, mem, ffn1_norm, ffn1_w_in, ffn1_w_down, mix_norm, w_in, ssm_a_re, ssm_a_im, ssm_log_dt,
           ssm_b_re, ssm_b_im, ssm_c_re, ssm_c_im, ssm_d, ssm_w_glu, rel_table, attn_w_up, mem_norm,
           xattn_w_kv, xattn_w_up, w_out, ffn2_norm, ffn2_w_in, ffn2_w_down, final_norm):
    bsz, seqlen, _ = x.shape
    assert ffn1_norm.shape[0] == 1, "single-layer trunk"
    assert x.shape[2] == D_MODEL and mem.shape == (bsz, MEM_LEN, D_MODEL)
    assert w_in.shape[1:] == (D_MODEL, PRE_WIDTH + POST_WIDTH) and ffn1_w_in.shape[1:] == (D_MODEL, 2 * D_FF)
    assert ssm_b_re.shape[1:] == (SSM_WIDTH // SSM_GROUP, SSM_STATE, SSM_GROUP)
    assert rel_table.shape == (REL_BUCKETS, N_PATTERNS * HEADS_PER_PATTERN)
    assert seqlen % ATTN_STEP_ROWS == 0 and (seqlen // ATTN_PATTERNS[-1][1]) % SUB_Q == 0
    assert seqlen % TM_PROJ == 0 and seqlen % TM_MERGE == 0 and (bsz * seqlen) % TM_FFN == 0
    assert bsz % SSM_SEQ_PER_STEP == 0
    h = _layer(x.reshape(bsz * seqlen, D_MODEL), mem, rel_table, ffn1_norm[0], ffn1_w_in[0], ffn1_w_down[0],
               mix_norm[0], w_in[0], ssm_a_re[0], ssm_a_im[0], ssm_log_dt[0], ssm_b_re[0], ssm_b_im[0],
               ssm_c_re[0], ssm_c_im[0], ssm_d[0], ssm_w_glu[0], attn_w_up[0], mem_norm[0], xattn_w_kv[0],
               xattn_w_up[0], w_out[0], ffn2_norm[0], ffn2_w_in[0], ffn2_w_down[0], final_norm,
               bsz, seqlen)
    return h.reshape(bsz, seqlen, D_MODEL)
```

```python
import functools

import numpy as np
import jax
import jax.numpy as jnp
from jax import lax
from jax.experimental import pallas as pl
from jax.experimental.pallas import tpu as pltpu

F32 = jnp.float32
BF16 = jnp.bfloat16

LANES = 128
BF16_SUBLANES = 16
MXU_TILE = 256

D_MODEL = 1024
D_FF = 2816
EPS = 1e-6

HEAD_DIM = 64
ATTN_PATTERNS = ((128, 1), (512, 4), (2048, 16))
N_PATTERNS = 3
HEADS_PER_PATTERN = 4
ATTN_WIDTH = N_PATTERNS * HEADS_PER_PATTERN * HEAD_DIM
PATTERN_WIDTH = HEADS_PER_PATTERN * HEAD_DIM
WINDOW_KEYS = 128
REL_BUCKETS = 32
REL_MAX_DIST = 2048

SSM_GROUP = 16
SSM_STATE = 64
SSM_WIDTH = 512
SSM_CHUNK = 16
SSM_OCT = LANES // SSM_GROUP
SSM_N_OCT = SSM_WIDTH // LANES
SSM_OCT_IN = SSM_CHUNK * LANES
SSM_OCT_STATE = SSM_OCT * SSM_STATE
SSM_ROW = SSM_CHUNK * SSM_WIDTH

MEM_LEN = 256
XATTN_HEADS = 4
XATTN_HEAD_DIM = 128
XATTN_WIDTH = XATTN_HEADS * XATTN_HEAD_DIM

PRE_WIDTH = SSM_WIDTH + 3 * ATTN_WIDTH
POST_WIDTH = XATTN_WIDTH + 3 * D_MODEL

NEG_BIG = -1e30

VMEM_LIMIT_BYTES = 56 * 1024 * 1024

TM_FFN = 1024
FFN_ROW_PARTS = 4
FFN_CHUNKS = ((0, 1536), (1536, D_FF))
TM_PROJ = 1024
PROJ_HALVES = 2
PROJ_IN_BUFFERS = 3
MERGE_IN_BUFFERS = 3
SPLIT_STRIDE = 4
TM_MERGE = 512
ATTN_STEP_ROWS = 4096
SSM_SEQ_PER_STEP = 2
SUB_Q = 128


def _rmsnorm(x, g):
    return x * lax.rsqrt(jnp.mean(x * x, axis=-1, keepdims=True) + EPS) * g


def _sigmoid(x):
    return 0.5 * jnp.tanh(0.5 * x) + 0.5


def _dot(a, b):
    return jnp.dot(a, b, preferred_element_type=F32)


def _dot_nt(a, b):
    return lax.dot_general(a, b, (((1,), (1,)), ((), ())), preferred_element_type=F32)


def _resident(shape):
    nd = len(shape)
    return pl.BlockSpec(shape, lambda *_: (0,) * nd, pipeline_mode=pl.Buffered(1))


def _params(n_axes):
    return pltpu.CompilerParams(dimension_semantics=("arbitrary",) * n_axes,
                                vmem_limit_bytes=VMEM_LIMIT_BYTES)


def _ffn_kernel(x_ref, g_ref, win_ref, wdn_ref, *rest, final_norm, n_cast):
    rest = list(rest)
    fg_ref = rest.pop(0) if final_norm else None
    cast_src, o_ref, cast_dst = rest[:n_cast], rest[n_cast], rest[n_cast + 1:]
    for part in range(FFN_ROW_PARTS):
        rows = slice(part * TM_FFN // FFN_ROW_PARTS, (part + 1) * TM_FFN // FFN_ROW_PARTS)
        x = x_ref[rows, :]
        un = _rmsnorm(x, g_ref[...]).astype(BF16)
        acc = None
        for lo, hi in FFN_CHUNKS:
            a = _dot(un, win_ref[:, lo:hi])
            b = _dot(un, win_ref[:, D_FF + lo:D_FF + hi])
            h = (a * _sigmoid(a) * b).astype(BF16)
            part = _dot(h, wdn_ref[lo:hi, :])
            acc = part if acc is None else acc + part
        y = x + 0.5 * acc
        if final_norm:
            y = _rmsnorm(y, fg_ref[...])
        o_ref[rows, :] = y

    dst = iter(cast_dst)
    for src in cast_src:
        lo = 0
        while lo < src.shape[1]:
            out = next(dst)
            out[...] = src[:, lo:lo + out.shape[1]].astype(BF16)
            lo += out.shape[1]


def _cast_rows(n_rows, n_steps):
    rows = BF16_SUBLANES
    while n_rows % rows or n_rows // rows > n_steps:
        rows += BF16_SUBLANES
    return rows


def _ffn(x, norm_g, w_in, w_down, final_g=None, cast=()):
    n = x.shape[0]
    n_steps = n // TM_FFN
    row = pl.BlockSpec((TM_FFN, D_MODEL), lambda i: (i, 0))
    in_specs = [row, _resident((1, D_MODEL)), _resident((D_MODEL, 2 * D_FF)), _resident((D_FF, D_MODEL))]
    args = [x, norm_g.reshape(1, D_MODEL), w_in, w_down]
    if final_g is not None:
        in_specs.append(_resident((1, D_MODEL)))
        args.append(final_g.reshape(1, D_MODEL))
    out_specs, out_shapes = [row], [jax.ShapeDtypeStruct((n, D_MODEL), F32)]
    for w, widths in cast:
        assert sum(widths) == w.shape[1]
        rows = _cast_rows(w.shape[0], n_steps)
        last = w.shape[0] // rows - 1
        block = lambda width, last=last, rows=rows: pl.BlockSpec((rows, width), lambda i: (jnp.minimum(i, last), 0))
        in_specs.append(block(w.shape[1]))
        args.append(w)
        out_specs += [block(width) for width in widths]
        out_shapes += [jax.ShapeDtypeStruct((w.shape[0], width), BF16) for width in widths]
    outs = pl.pallas_call(
        functools.partial(_ffn_kernel, final_norm=final_g is not None, n_cast=len(cast)),
        grid=(n_steps,),
        in_specs=in_specs,
        out_specs=out_specs,
        out_shape=out_shapes,
        compiler_params=_params(1),
        name="ffn_final" if final_g is not None else "ffn",
    )(*args)
    return outs[0], outs[1:]


def _rows_by_residue(slab_ref, j, stage_ref, slot, n_rows, d):
    if d == 1:
        return [slab_ref[j]]
    if d == SPLIT_STRIDE:
        return [slab_ref[j, pl.ds(r, n_rows // d, stride=d), :] for r in range(d)]
    assert d == SPLIT_STRIDE * SPLIT_STRIDE
    for low in range(SPLIT_STRIDE):
        stage_ref[slot, low] = slab_ref[j, pl.ds(low, n_rows // SPLIT_STRIDE, stride=SPLIT_STRIDE), :]
    return [stage_ref[slot, r % SPLIT_STRIDE, pl.ds(r // SPLIT_STRIDE, n_rows // d, stride=SPLIT_STRIDE), :]
            for r in range(d)]


def _proj_kernel(x_ref, g_ref, w_ref, u_ref, *rest):
    qkv_refs, slabs = rest[:3 * N_PATTERNS], rest[3 * N_PATTERNS:]
    n_rows = TM_PROJ // PROJ_HALVES
    per_half = len(slabs) // PROJ_HALVES
    for half in range(PROJ_HALVES):
        base = half * n_rows
        su_ref, sq_ref, sk_ref, sv_ref, stage_ref = slabs[per_half * half:per_half * (half + 1)]
        un = _rmsnorm(x_ref[base:base + n_rows, :], g_ref[...]).astype(BF16)
        full = _dot(un, w_ref[...])
        slot = 0

        u = full[:, :SSM_WIDTH]
        chunks = slice(base // SSM_CHUNK, (base + n_rows) // SSM_CHUNK)
        for o in range(SSM_N_OCT):
            su_ref[o] = u[:, o * LANES:(o + 1) * LANES]
            for t, rows in enumerate(_rows_by_residue(su_ref, o, stage_ref, slot, n_rows, SSM_CHUNK)):
                lo = o * SSM_OCT_IN + t * LANES
                u_ref[chunks, lo:lo + LANES] = rows.astype(BF16)
            slot += 1

        for which, scr in enumerate((sq_ref, sk_ref, sv_ref)):
            lo = SSM_WIDTH + which * ATTN_WIDTH
            res = full[:, lo:lo + ATTN_WIDTH]
            if which == 0:
                res = res * (HEAD_DIM ** -0.5)
            for j in range(ATTN_WIDTH // LANES):
                scr[j] = res[:, j * LANES:(j + 1) * LANES]
            for g, (_, d) in enumerate(ATTN_PATTERNS):
                out = qkv_refs[which * N_PATTERNS + g]
                for jj in range(PATTERN_WIDTH // LANES):
                    j = g * (PATTERN_WIDTH // LANES) + jj
                    for r, rows in enumerate(_rows_by_residue(scr, j, stage_ref, slot, n_rows, d)):
                        out[0, r, base // d:(base + n_rows) // d, jj * LANES:(jj + 1) * LANES] = rows.astype(BF16)
                    slot += d == SPLIT_STRIDE * SPLIT_STRIDE


def _proj(x, norm_g, w_pre, bsz, seqlen):
    n = x.shape[0]
    tiles_per_seq = seqlen // TM_PROJ
    rows = lambda w: pl.BlockSpec((TM_PROJ, w), lambda i: (i, 0))
    qkv_specs, qkv_shapes = [], []
    for _ in range(3):
        for _, d in ATTN_PATTERNS:
            qkv_specs.append(pl.BlockSpec((1, d, TM_PROJ // d, PATTERN_WIDTH),
                                          lambda i: (i // tiles_per_seq, 0, i % tiles_per_seq, 0)))
            qkv_shapes.append(jax.ShapeDtypeStruct((bsz, d, seqlen // d, PATTERN_WIDTH), BF16))
    half_rows = TM_PROJ // PROJ_HALVES
    slabs = lambda k: pltpu.VMEM((k, half_rows, LANES), F32)
    n_staged = SSM_N_OCT + 3 * PATTERN_WIDTH // LANES
    stage = pltpu.VMEM((n_staged, SPLIT_STRIDE, half_rows // SPLIT_STRIDE, LANES), F32)
    n_out = 1 + 3 * N_PATTERNS
    out_specs = [pl.BlockSpec((TM_PROJ // SSM_CHUNK, SSM_ROW), lambda i: (i, 0))] + qkv_specs

    def streamed(x_hbm, g_ref, w_ref, *rest):
        out_hbm, scratch = rest[:n_out], rest[n_out:]
        body = lambda x_ref, *out_refs: _proj_kernel(x_ref, g_ref, w_ref, *out_refs, *scratch)
        pltpu.emit_pipeline(
            body, grid=(n // TM_PROJ,),
            in_specs=[pl.BlockSpec((TM_PROJ, D_MODEL), lambda i: (i, 0), pipeline_mode=pl.Buffered(PROJ_IN_BUFFERS))],
            out_specs=out_specs,
        )(x_hbm, *out_hbm)

    whole = lambda shape: pl.BlockSpec(shape, lambda: (0,) * len(shape), memory_space=pltpu.VMEM)
    outs = pl.pallas_call(
        streamed,
        in_specs=[pl.BlockSpec(memory_space=pl.ANY), whole((1, D_MODEL)), whole((D_MODEL, PRE_WIDTH))],
        out_specs=[pl.BlockSpec(memory_space=pl.ANY)] * n_out,
        out_shape=[jax.ShapeDtypeStruct((n // SSM_CHUNK, SSM_ROW), BF16)] + qkv_shapes,
        scratch_shapes=([slabs(SSM_N_OCT)] + [slabs(ATTN_WIDTH // LANES)] * 3 + [stage]) * PROJ_HALVES,
        compiler_params=pltpu.CompilerParams(vmem_limit_bytes=VMEM_LIMIT_BYTES),
        name="proj",
    )(x, norm_g.reshape(1, D_MODEL), w_pre)
    u = outs[0]
    q, k, v = (outs[1 + i * N_PATTERNS:1 + (i + 1) * N_PATTERNS] for i in range(3))
    return u, q, k, v


def _ssm_compact(a_re, a_im, log_dt, b_re, b_im, c_re, c_im, d_skip):
    t_len = SSM_CHUNK
    dt = jnp.exp(log_dt)[:, None]
    mag = jnp.exp(a_re * dt)
    ang = a_im * dt
    abar_re = mag * jnp.cos(ang)
    abar_im = mag * jnp.sin(ang)
    nr = abar_re - 1.0
    ni = abar_im
    den = a_re * a_re + a_im * a_im
    coef_re = (nr * a_re + ni * a_im) / den
    coef_im = (ni * a_re - nr * a_im) / den
    bbar_re = coef_re[..., None] * b_re - coef_im[..., None] * b_im
    bbar_im = coef_re[..., None] * b_im + coef_im[..., None] * b_re
    bt_re = jnp.swapaxes(bbar_re, -1, -2)
    bt_im = jnp.swapaxes(bbar_im, -1, -2)
    pw_re = [jnp.ones_like(abar_re)]
    pw_im = [jnp.zeros_like(abar_re)]
    for _ in range(t_len):
        pr, pi = pw_re[-1], pw_im[-1]
        pw_re.append(pr * abar_re - pi * abar_im)
        pw_im.append(pr * abar_im + pi * abar_re)
    rev_re = jnp.stack(pw_re[t_len - 1::-1])[:, :, None, :]
    rev_im = jnp.stack(pw_im[t_len - 1::-1])[:, :, None, :]
    pw_re = jnp.stack(pw_re)
    pw_im = jnp.stack(pw_im)
    n_oct, oct_, grp, st = SSM_N_OCT, SSM_OCT, SSM_GROUP, SSM_STATE

    def per_octet(re, im):
        both = jnp.concatenate([re, im], axis=-1)
        n = both.shape[0]
        return both.reshape(n, n_oct, oct_ * grp, 2 * st).transpose(1, 0, 2, 3).reshape(n_oct, n * LANES, 2 * st)

    into_c = per_octet(rev_re * bt_re[None] - rev_im * bt_im[None], rev_re * bt_im[None] + rev_im * bt_re[None])
    cp_re = c_re[None] * pw_re[:, :, None, :] - c_im[None] * pw_im[:, :, None, :]
    cp_im = c_re[None] * pw_im[:, :, None, :] + c_im[None] * pw_re[:, :, None, :]
    outof_c = per_octet(cp_re, -cp_im)
    decay = jnp.stack([pw_re[t_len].reshape(n_oct, SSM_OCT_STATE), pw_im[t_len].reshape(n_oct, SSM_OCT_STATE)],
                      axis=1)
    skip = d_skip.reshape(n_oct, 1, LANES)
    return into_c.astype(BF16), outof_c.astype(BF16), decay, skip


def _ssm_expanders():
    row = np.arange(LANES)
    col = np.arange(2 * SSM_OCT_STATE)
    spread = (row[:, None] // SSM_STATE == col[None, :] // SSM_OCT_STATE) & (
        row[:, None] % SSM_STATE == col[None, :] % SSM_STATE)
    keep = row[:, None] // SSM_GROUP == (col[None, :] % SSM_OCT_STATE) // SSM_STATE
    return jnp.asarray(spread, BF16), jnp.asarray(keep, F32)


def _ssm_kernel(x_ref, intoc_ref, outofc_ref, decay_ref, skip_ref, spread_ref, keep_ref, y_ref,
                toep_ref, into_ref, outof_ref, loc_ref, prev_ref, part_ref, *, n_chunks, n_seq):
    blk = lambda i: slice(i * LANES, (i + 1) * LANES)

    @pl.when(pl.program_id(1) == 0)
    def _build_octet_matrices():
        spread, keep = spread_ref[...], keep_ref[...]
        for s in range(SSM_CHUNK):
            into_ref[blk(s), :] = (_dot(intoc_ref[0, blk(s), :], spread) * keep).astype(BF16)
        for j in range(SSM_CHUNK + 1):
            outof_ref[blk(j), :] = (_dot(outofc_ref[0, blk(j), :], spread) * keep).astype(BF16)
        bbar = into_ref[blk(SSM_CHUNK - 1), :]
        diag = (lax.broadcasted_iota(jnp.int32, (LANES, LANES), 0)
                == lax.broadcasted_iota(jnp.int32, (LANES, LANES), 1))
        zeros = jnp.zeros((LANES, LANES), BF16)
        for j in range(SSM_CHUNK):
            lag = _dot_nt(bbar, outof_ref[blk(j), :])
            if j == 0:
                lag = lag + jnp.where(diag, skip_ref[0], 0.0)
            lag = lag.astype(BF16)
            for s in range(SSM_CHUNK - j):
                toep_ref[blk(s), blk(s + j)] = lag
            if j > 0:
                for s in range(j, SSM_CHUNK):
                    toep_ref[blk(s), blk(s - j)] = zeros

    loc_ref[...] = _dot(x_ref[...], into_ref[...])
    n_tiles = SSM_OCT_IN // MXU_TILE
    early = n_tiles - part_ref.shape[1] // MXU_TILE
    in_chunk = lambda j: _dot(x_ref[:, :(j + 1) * MXU_TILE], toep_ref[:(j + 1) * MXU_TILE, j * MXU_TILE:(j + 1) * MXU_TILE])
    for j in range(early, n_tiles):
        part_ref[:, (j - early) * MXU_TILE:(j - early + 1) * MXU_TILE] = in_chunk(j)
    ar = decay_ref[0, 0:1, :]
    ai = decay_ref[0, 1:2, :]

    def step(c, carry):
        new = []
        for q in range(n_seq):
            s_re, s_im = carry[2 * q], carry[2 * q + 1]
            row = pl.ds(q * n_chunks + c, 1)
            prev_ref[row, :SSM_OCT_STATE] = s_re
            prev_ref[row, SSM_OCT_STATE:] = s_im
            new.append(ar * s_re - ai * s_im + loc_ref[row, :SSM_OCT_STATE])
            new.append(ar * s_im + ai * s_re + loc_ref[row, SSM_OCT_STATE:])
        return tuple(new)

    zero = jnp.zeros((1, SSM_OCT_STATE), F32)
    lax.fori_loop(0, n_chunks, step, (zero,) * (2 * n_seq), unroll=True)
    prev = prev_ref[...].astype(BF16)
    for j in range(n_tiles):
        lo, hi = j * MXU_TILE, (j + 1) * MXU_TILE
        within = in_chunk(j) if j < early else part_ref[:, (j - early) * MXU_TILE:(j - early + 1) * MXU_TILE]
        y = within + _dot_nt(prev, outof_ref[lo + LANES:hi + LANES, :])
        y_ref[:, lo:hi] = jax.nn.gelu(y).astype(BF16)


def _ssm(u_rows, compact, bsz):
    into_c, outof_c, decay, skip = compact
    spread, keep = _ssm_expanders()
    n_chunks = u_rows.shape[0] // bsz
    n_seq = SSM_SEQ_PER_STEP
    per_oct = lambda *shape: pl.BlockSpec((1,) + shape, lambda o, b: (o,) + (0,) * len(shape))
    rows = pl.BlockSpec((n_seq * n_chunks, SSM_OCT_IN), lambda o, b: (b, o))
    wide = 2 * SSM_OCT_STATE
    return pl.pallas_call(
        functools.partial(_ssm_kernel, n_chunks=n_chunks, n_seq=n_seq),
        grid=(SSM_N_OCT, bsz // n_seq),
        in_specs=[rows, per_oct(SSM_OCT_IN, LANES), per_oct(SSM_OCT_IN + LANES, LANES),
                  per_oct(2, SSM_OCT_STATE), per_oct(1, LANES), _resident((LANES, wide)), _resident((LANES, wide))],
        out_specs=rows,
        out_shape=jax.ShapeDtypeStruct(u_rows.shape, BF16),
        scratch_shapes=[pltpu.VMEM((SSM_OCT_IN, SSM_OCT_IN), BF16),
                        pltpu.VMEM((SSM_OCT_IN, wide), BF16),
                        pltpu.VMEM((SSM_OCT_IN + LANES, wide), BF16),
                        pltpu.VMEM((n_seq * n_chunks, wide), F32),
                        pltpu.VMEM((n_seq * n_chunks, wide), F32),
                        pltpu.VMEM((n_seq * n_chunks, SSM_OCT_IN // 2), F32)],
        compiler_params=_params(2),
        name="ssm",
    )(u_rows, into_c, outof_c, decay, skip, spread, keep)


def _t5_buckets(dist):
    dist = np.asarray(dist, np.int32)
    max_exact = REL_BUCKETS // 2
    safe = np.maximum(dist, 1).astype(np.float32)
    large = max_exact + (np.log(safe / max_exact) / np.log(REL_MAX_DIST / max_exact)
                         * (REL_BUCKETS - max_exact)).astype(np.int32)
    large = np.minimum(large, REL_BUCKETS - 1)
    return np.where(dist < max_exact, dist, large).astype(np.int32)


def _band_bias(rel_table, pattern):
    dilation = ATTN_PATTERNS[pattern][1]
    buckets = _t5_buckets(np.arange(WINDOW_KEYS + 1) * dilation)
    table = rel_table[:, pattern * HEADS_PER_PATTERN:(pattern + 1) * HEADS_PER_PATTERN].T
    runs = []
    for b in np.unique(buckets):
        runs.append(jnp.broadcast_to(table[:, b:b + 1], (HEADS_PER_PATTERN, int(np.sum(buckets == b)))))
    per_step = jnp.concatenate(runs, axis=1)
    period = 3 * SUB_Q
    masked = lambda n: jnp.full((HEADS_PER_PATTERN, n), NEG_BIG, F32)
    w = jnp.concatenate([masked(SUB_Q - 1), per_step[:, ::-1], masked(period - SUB_Q - WINDOW_KEYS)], axis=1)
    skew = jnp.tile(w, (1, SUB_Q))[:, :SUB_Q * (period - 1)].reshape(HEADS_PER_PATTERN, SUB_Q, period - 1)
    bias = skew[:, :, SUB_Q - 1:3 * SUB_Q - 1]
    first = jnp.where((np.arange(2 * SUB_Q) >= SUB_Q)[None, None, :], bias, NEG_BIG)
    pairs = lambda b: b.reshape(HEADS_PER_PATTERN // 2, 2 * SUB_Q, 2 * SUB_Q)
    return jnp.stack([pairs(bias), pairs(first)])


def _attn_kernel(q_ref, kc_ref, kh_ref, vc_ref, vh_ref, bias_ref, o_ref):
    n_res, tq = q_ref.shape[1], q_ref.shape[2]
    first = jnp.where(pl.program_id(2) == 0, 1, 0)
    lane = lax.broadcasted_iota(jnp.int32, (SUB_Q, 2 * HEAD_DIM), 1)
    low_half = lane < HEAD_DIM
    stat_is_max = lax.broadcasted_iota(jnp.int32, (2 * SUB_Q, 2 * HEAD_DIM), 1) % HEAD_DIM < HEAD_DIM // 2
    for r in range(n_res):
        for j in range(tq // SUB_Q):
            rows = slice(j * SUB_Q, (j + 1) * SUB_Q)
            q = q_ref[0, r, rows, :]
            if j == 0:
                k = jnp.concatenate([kh_ref[0, r], kc_ref[0, r, rows, :]], axis=0)
                v = jnp.concatenate([vh_ref[0, r], vc_ref[0, r, rows, :]], axis=0)
            else:
                k = kc_ref[0, r, (j - 1) * SUB_Q:(j + 1) * SUB_Q, :]
                v = vc_ref[0, r, (j - 1) * SUB_Q:(j + 1) * SUB_Q, :]
            for pair in range(HEADS_PER_PATTERN // 2):
                cols = slice(pair * 2 * HEAD_DIM, (pair + 1) * 2 * HEAD_DIM)
                q2, k2, v2 = q[:, cols], k[:, cols], v[:, cols]
                zero = jnp.zeros_like(q2)
                qq = jnp.concatenate([jnp.where(low_half, q2, zero), jnp.where(low_half, zero, q2)], axis=0)
                s = _dot_nt(qq, k2) + (bias_ref[first, pair] if j == 0 else bias_ref[0, pair])
                m = jnp.max(s, axis=-1, keepdims=True)
                p = jnp.exp(s - m)
                l = jnp.sum(p, axis=-1, keepdims=True)
                o = _dot(p.astype(BF16), v2)
                wide = (2 * SUB_Q, 2 * HEAD_DIM)
                stat = jnp.where(stat_is_max, jnp.broadcast_to(m, wide), jnp.broadcast_to(l, wide))
                o_ref[0, r, rows, cols] = jnp.where(low_half, o[:SUB_Q], o[SUB_Q:])
                o_ref[0, r, rows, PATTERN_WIDTH + pair * 2 * HEAD_DIM:PATTERN_WIDTH + (pair + 1) * 2 * HEAD_DIM] = (
                    jnp.where(low_half, stat[:SUB_Q], stat[SUB_Q:]))


def _dilated_attention(q, k, v, rel_table, pattern):
    bsz, d, res_len, _ = q.shape
    tq = min(res_len, ATTN_STEP_ROWS)
    n_res = ATTN_STEP_ROWS // tq
    cur = pl.BlockSpec((1, n_res, tq, PATTERN_WIDTH), lambda b, r, i: (b, r, i, 0))
    halo = pl.BlockSpec((1, n_res, SUB_Q, PATTERN_WIDTH),
                        lambda b, r, i: (b, r, jnp.maximum(i * (tq // SUB_Q) - 1, 0), 0))
    return pl.pallas_call(
        _attn_kernel,
        grid=(bsz, d // n_res, res_len // tq),
        in_specs=[cur, cur, halo, cur, halo, _resident((2, HEADS_PER_PATTERN // 2, 2 * SUB_Q, 2 * SUB_Q))],
        out_specs=pl.BlockSpec((1, n_res, tq, 2 * PATTERN_WIDTH), lambda b, r, i: (b, r, i, 0)),
        out_shape=jax.ShapeDtypeStruct((bsz, d, res_len, 2 * PATTERN_WIDTH), F32),
        compiler_params=_params(3),
        name=f"attn_d{d}",
    )(q, k, k, v, v, _band_bias(rel_table, pattern))


def _memkv_kernel(mem_ref, g_ref, w_ref, k_ref, v_ref):
    mn = _rmsnorm(mem_ref[...], g_ref[...]).astype(BF16)
    k_ref[...] = _dot(mn, w_ref[:, :XATTN_WIDTH]).astype(BF16)
    v_ref[...] = _dot(mn, w_ref[:, XATTN_WIDTH:]).astype(BF16)


def _memkv(mem2d, norm_g, w_kv):
    rows = mem2d.shape[0]
    full = lambda *shape: pl.BlockSpec(shape, lambda i: (0,) * len(shape))
    return pl.pallas_call(
        _memkv_kernel,
        grid=(1,),
        in_specs=[full(rows, D_MODEL), full(1, D_MODEL), full(D_MODEL, 2 * XATTN_WIDTH)],
        out_specs=[full(rows, XATTN_WIDTH), full(rows, XATTN_WIDTH)],
        out_shape=[jax.ShapeDtypeStruct((rows, XATTN_WIDTH), BF16)] * 2,
        compiler_params=_params(1),
        name="memkv",
    )(mem2d, norm_g.reshape(1, D_MODEL), w_kv)


def _merge_kernel(x_ref, y_ref, o0_ref, o1_ref, o2_ref, mk_ref, mv_ref, g_ref, wpost_ref, wglu_ref,
                  wau_ref, wxu_ref, wout_ref, out_ref, ys_ref, s1_ref, s2_ref):
    x = x_ref[...]
    un = _rmsnorm(x, g_ref[...]).astype(BF16)

    def gate(index):
        lo = XATTN_WIDTH + index * D_MODEL
        return _sigmoid(_dot(un, wpost_ref[:, lo:lo + D_MODEL]))

    slabs = 2 * PATTERN_WIDTH // LANES
    for o_ref, s_ref in ((o1_ref, s1_ref), (o2_ref, s2_ref)):
        d = o_ref.shape[1]
        for r in range(d):
            for j in range(slabs):
                s_ref[j, pl.ds(r, TM_MERGE // d, stride=d), :] = o_ref[0, r, :, j * LANES:(j + 1) * LANES]
    half = slabs // 2
    is_max = lax.broadcasted_iota(jnp.int32, (TM_MERGE, LANES), 1) % HEAD_DIM < HEAD_DIM // 2

    def unpack(stat):
        return (jnp.where(is_max, stat, pltpu.roll(stat, HEAD_DIM // 2, axis=1)),
                jnp.where(is_max, pltpu.roll(stat, LANES - HEAD_DIM // 2, axis=1), stat))

    def pattern(read):
        out = jnp.concatenate([read(j) for j in range(half)], axis=-1)
        stats = [unpack(read(j)) for j in range(half, slabs)]
        return (out, jnp.concatenate([m for m, _ in stats], axis=-1),
                jnp.concatenate([l for _, l in stats], axis=-1))

    parts = [pattern(lambda j: o0_ref[0, 0, :, j * LANES:(j + 1) * LANES]),
             pattern(lambda j: s1_ref[j]), pattern(lambda j: s2_ref[j])]
    top = jnp.maximum(jnp.maximum(parts[0][1], parts[1][1]), parts[2][1])
    ws = [jnp.exp(m - top) for _, m, _ in parts]
    att = ((ws[0] * parts[0][0] + ws[1] * parts[1][0] + ws[2] * parts[2][0])
           / (ws[0] * parts[0][2] + ws[1] * parts[1][2] + ws[2] * parts[2][2]))
    att = att.astype(BF16)

    chunk_rows = TM_MERGE // SSM_CHUNK
    for o in range(SSM_N_OCT):
        for t in range(SSM_CHUNK):
            lo = o * SSM_OCT_IN + t * LANES
            ys_ref[o, pl.ds(t, chunk_rows, stride=SSM_CHUNK), :] = y_ref[:, lo:lo + LANES].astype(F32)
    ys = jnp.concatenate([ys_ref[o] for o in range(SSM_N_OCT)], axis=-1).astype(BF16)
    glu = _dot(ys, wglu_ref[:, :D_MODEL]) * _sigmoid(_dot(ys, wglu_ref[:, D_MODEL:]))
    merged = gate(0) * glu

    xq = _dot(un, wpost_ref[:, :XATTN_WIDTH]).astype(BF16)
    heads = []
    for h in range(XATTN_HEADS):
        cols = slice(h * XATTN_HEAD_DIM, (h + 1) * XATTN_HEAD_DIM)
        s = _dot_nt(xq[:, cols], mk_ref[0, :, cols]) * (XATTN_HEAD_DIM ** -0.5)
        p = jnp.exp(s - jnp.max(s, axis=-1, keepdims=True))
        heads.append(_dot(p.astype(BF16), mv_ref[0, :, cols]) / jnp.sum(p, axis=-1, keepdims=True))
    xo = jnp.concatenate(heads, axis=-1).astype(BF16)
    merged = merged + gate(2) * _dot(xo, wxu_ref[...])

    merged = merged + gate(1) * _dot(att, wau_ref[...])

    out_ref[...] = x + _dot(merged.astype(BF16), wout_ref[...])


def _merge(x, y_rows, attn_outs, mk, mv, norm_g, w_post, w_glu, w_au, w_xu, w_out, seqlen):
    n = x.shape[0]
    tiles_per_seq = seqlen // TM_MERGE
    rows = lambda w: pl.BlockSpec((TM_MERGE, w), lambda i: (i, 0))
    mem = pl.BlockSpec((1, MEM_LEN, XATTN_WIDTH), lambda i: (i // tiles_per_seq, 0, 0))
    attn_specs = [pl.BlockSpec((1, d, TM_MERGE // d, 2 * PATTERN_WIDTH),
                               lambda i: (i // tiles_per_seq, 0, i % tiles_per_seq, 0))
                  for _, d in ATTN_PATTERNS]
    slabs = lambda k: pltpu.VMEM((k, TM_MERGE, LANES), F32)
    deep = lambda spec: pl.BlockSpec(spec.block_shape, spec.index_map, pipeline_mode=pl.Buffered(MERGE_IN_BUFFERS))
    streamed_specs = [deep(s) for s in [rows(D_MODEL), pl.BlockSpec((TM_MERGE // SSM_CHUNK, SSM_ROW), lambda i: (i, 0))]
                      + attn_specs] + [mem, mem]
    n_streamed = len(streamed_specs)
    weight_shapes = [(1, D_MODEL), (D_MODEL, POST_WIDTH), (SSM_WIDTH, 2 * D_MODEL), (PATTERN_WIDTH, D_MODEL),
                     (XATTN_WIDTH, D_MODEL), (D_MODEL, D_MODEL)]

    def streamed(*refs):
        in_hbm, refs = refs[:n_streamed], refs[n_streamed:]
        weights, out_hbm, scratch = refs[:len(weight_shapes)], refs[len(weight_shapes)], refs[len(weight_shapes) + 1:]
        body = lambda *tiles: _merge_kernel(*tiles[:n_streamed], *weights, tiles[n_streamed], *scratch)
        pltpu.emit_pipeline(body, grid=(n // TM_MERGE,), in_specs=streamed_specs,
                            out_specs=[rows(D_MODEL)])(*in_hbm, out_hbm)

    whole = lambda shape: pl.BlockSpec(shape, lambda: (0,) * len(shape), memory_space=pltpu.VMEM)
    return pl.pallas_call(
        streamed,
        in_specs=[pl.BlockSpec(memory_space=pl.ANY)] * n_streamed + [whole(s) for s in weight_shapes],
        out_specs=pl.BlockSpec(memory_space=pl.ANY),
        out_shape=jax.ShapeDtypeStruct((n, D_MODEL), F32),
        scratch_shapes=[slabs(SSM_N_OCT), slabs(2 * PATTERN_WIDTH // LANES), slabs(2 * PATTERN_WIDTH // LANES)],
        compiler_params=pltpu.CompilerParams(vmem_limit_bytes=VMEM_LIMIT_BYTES),
        name="merge",
    )(x, y_rows, *attn_outs, mk, mv, norm_g.reshape(1, D_MODEL), w_post, w_glu, w_au, w_xu, w_out)


def _layer(x, mem, rel_table, ffn1_norm, ffn1_w_in, ffn1_w_down, mix_norm, w_in,
           ssm_a_re, ssm_a_im, ssm_log_dt, ssm_b_re, ssm_b_im, ssm_c_re, ssm_c_im, ssm_d, ssm_w_glu,
           attn_w_up, mem_norm, xattn_w_kv, xattn_w_up, w_out, ffn2_norm, ffn2_w_in, ffn2_w_down,
           final_norm, bsz, seqlen):
    cast = [(w_in, (PRE_WIDTH, POST_WIDTH)), (ssm_w_glu, (2 * D_MODEL,)), (attn_w_up, (D_MODEL,)),
            (xattn_w_kv, (2 * XATTN_WIDTH,)), (xattn_w_up, (D_MODEL,)), (w_out, (D_MODEL,)),
            (ffn2_w_in, (2 * D_FF,)), (ffn2_w_down, (D_MODEL,))]
    x, (w_pre, w_post, w_glu, w_au, w_kv, w_xu, w_o, w2_in, w2_down) = _ffn(
        x, ffn1_norm, ffn1_w_in.astype(BF16), ffn1_w_down.astype(BF16), cast=cast)
    u_rows, q, k, v = _proj(x, mix_norm, w_pre, bsz, seqlen)
    compact = _ssm_compact(ssm_a_re, ssm_a_im, ssm_log_dt, ssm_b_re, ssm_b_im, ssm_c_re, ssm_c_im, ssm_d)
    y_rows = _ssm(u_rows, compact, bsz)
    attn_outs = [_dilated_attention(q[g], k[g], v[g], rel_table, g) for g in range(N_PATTERNS)]
    mk, mv = _memkv(mem.reshape(bsz * MEM_LEN, D_MODEL), mem_norm, w_kv)
    mk = mk.reshape(bsz, MEM_LEN, XATTN_WIDTH)
    mv = mv.reshape(bsz, MEM_LEN, XATTN_WIDTH)
    x = _merge(x, y_rows, attn_outs, mk, mv, mix_norm, w_post, w_glu, w_au, w_xu, w_o, seqlen)
    return _ffn(x, ffn2_norm, w2_in, w2_down, final_g=final_norm)[0]


def kernel(x, mem, ffn1_norm, ffn1_w_in, ffn1_w_down, mix_norm, w_in, ssm_a_re, ssm_a_im, ssm_log_dt,
           ssm_b_re, ssm_b_im, ssm_c_re, ssm_c_im, ssm_d, ssm_w_glu, rel_table, attn_w_up, mem_norm,
           xattn_w_kv, xattn_w_up, w_out, ffn2_norm, ffn2_w_in, ffn2_w_down, final_norm):
    bsz, seqlen, _ = x.shape
    assert ffn1_norm.shape[0] == 1, "single-layer trunk"
    assert x.shape[2] == D_MODEL and mem.shape == (bsz, MEM_LEN, D_MODEL)
    assert w_in.shape[1:] == (D_MODEL, PRE_WIDTH + POST_WIDTH) and ffn1_w_in.shape[1:] == (D_MODEL, 2 * D_FF)
    assert ssm_b_re.shape[1:] == (SSM_WIDTH // SSM_GROUP, SSM_STATE, SSM_GROUP)
    assert rel_table.shape == (REL_BUCKETS, N_PATTERNS * HEADS_PER_PATTERN)
    assert seqlen % ATTN_STEP_ROWS == 0 and (seqlen // ATTN_PATTERNS[-1][1]) % SUB_Q == 0
    assert seqlen % TM_PROJ == 0 and seqlen % TM_MERGE == 0 and (bsz * seqlen) % TM_FFN == 0
    assert bsz % SSM_SEQ_PER_STEP == 0
    h = _layer(x.reshape(bsz * seqlen, D_MODEL), mem, rel_table, ffn1_norm[0], ffn1_w_in[0], ffn1_w_down[0],
               mix_norm[0], w_in[0], ssm_a_re[0], ssm_a_im[0], ssm_log_dt[0], ssm_b_re[0], ssm_b_im[0],
               ssm_c_re[0], ssm_c_im[0], ssm_d[0], ssm_w_glu[0], attn_w_up[0], mem_norm[0], xattn_w_kv[0],
               xattn_w_up[0], w_out[0], ffn2_norm[0], ffn2_w_in[0], ffn2_w_down[0], final_norm,
               bsz, seqlen)
    return h.reshape(bsz, seqlen, D_MODEL)
```

```python
import functools

import numpy as np
import jax
import jax.numpy as jnp
from jax import lax
from jax.experimental import pallas as pl
from jax.experimental.pallas import tpu as pltpu

F32 = jnp.float32
BF16 = jnp.bfloat16

LANES = 128
BF16_SUBLANES = 16
MXU_TILE = 256

D_MODEL = 1024
D_FF = 2816
EPS = 1e-6

HEAD_DIM = 64
ATTN_PATTERNS = ((128, 1), (512, 4), (2048, 16))
N_PATTERNS = 3
HEADS_PER_PATTERN = 4
ATTN_WIDTH = N_PATTERNS * HEADS_PER_PATTERN * HEAD_DIM
PATTERN_WIDTH = HEADS_PER_PATTERN * HEAD_DIM
WINDOW_KEYS = 128
REL_BUCKETS = 32
REL_MAX_DIST = 2048

SSM_GROUP = 16
SSM_STATE = 64
SSM_WIDTH = 512
SSM_CHUNK = 16
SSM_OCT = LANES // SSM_GROUP
SSM_N_OCT = SSM_WIDTH // LANES
SSM_OCT_IN = SSM_CHUNK * LANES
SSM_OCT_STATE = SSM_OCT * SSM_STATE
SSM_ROW = SSM_CHUNK * SSM_WIDTH

MEM_LEN = 256
XATTN_HEADS = 4
XATTN_HEAD_DIM = 128
XATTN_WIDTH = XATTN_HEADS * XATTN_HEAD_DIM

PRE_WIDTH = SSM_WIDTH + 3 * ATTN_WIDTH
POST_WIDTH = XATTN_WIDTH + 3 * D_MODEL

NEG_BIG = -1e30

VMEM_LIMIT_BYTES = 56 * 1024 * 1024

TM_FFN = 1024
FFN_ROW_PARTS = 4
FFN_CHUNKS = ((0, 1536), (1536, D_FF))
TM_PROJ = 1024
PROJ_HALVES = 2
PROJ_IN_BUFFERS = 3
SPLIT_STRIDE = 4
TM_MERGE = 512
ATTN_STEP_ROWS = 4096
SSM_SEQ_PER_STEP = 2
SUB_Q = 128


def _rmsnorm(x, g):
    return x * lax.rsqrt(jnp.mean(x * x, axis=-1, keepdims=True) + EPS) * g


def _sigmoid(x):
    return 0.5 * jnp.tanh(0.5 * x) + 0.5


def _dot(a, b):
    return jnp.dot(a, b, preferred_element_type=F32)


def _dot_nt(a, b):
    return lax.dot_general(a, b, (((1,), (1,)), ((), ())), preferred_element_type=F32)


def _resident(shape):
    nd = len(shape)
    return pl.BlockSpec(shape, lambda *_: (0,) * nd, pipeline_mode=pl.Buffered(1))


def _params(n_axes):
    return pltpu.CompilerParams(dimension_semantics=("arbitrary",) * n_axes,
                                vmem_limit_bytes=VMEM_LIMIT_BYTES)


def _ffn_kernel(x_ref, g_ref, win_ref, wdn_ref, *rest, final_norm, n_cast):
    rest = list(rest)
    fg_ref = rest.pop(0) if final_norm else None
    cast_src, o_ref, cast_dst = rest[:n_cast], rest[n_cast], rest[n_cast + 1:]
    for part in range(FFN_ROW_PARTS):
        rows = slice(part * TM_FFN // FFN_ROW_PARTS, (part + 1) * TM_FFN // FFN_ROW_PARTS)
        x = x_ref[rows, :]
        un = _rmsnorm(x, g_ref[...]).astype(BF16)
        acc = None
        for lo, hi in FFN_CHUNKS:
            a = _dot(un, win_ref[:, lo:hi])
            b = _dot(un, win_ref[:, D_FF + lo:D_FF + hi])
            h = (a * _sigmoid(a) * b).astype(BF16)
            part = _dot(h, wdn_ref[lo:hi, :])
            acc = part if acc is None else acc + part
        y = x + 0.5 * acc
        if final_norm:
            y = _rmsnorm(y, fg_ref[...])
        o_ref[rows, :] = y

    dst = iter(cast_dst)
    for src in cast_src:
        lo = 0
        while lo < src.shape[1]:
            out = next(dst)
            out[...] = src[:, lo:lo + out.shape[1]].astype(BF16)
            lo += out.shape[1]


def _cast_rows(n_rows, n_steps):
    rows = BF16_SUBLANES
    while n_rows % rows or n_rows // rows > n_steps:
        rows += BF16_SUBLANES
    return rows


def _ffn(x, norm_g, w_in, w_down, final_g=None, cast=()):
    n = x.shape[0]
    n_steps = n // TM_FFN
    row = pl.BlockSpec((TM_FFN, D_MODEL), lambda i: (i, 0))
    in_specs = [row, _resident((1, D_MODEL)), _resident((D_MODEL, 2 * D_FF)), _resident((D_FF, D_MODEL))]
    args = [x, norm_g.reshape(1, D_MODEL), w_in, w_down]
    if final_g is not None:
        in_specs.append(_resident((1, D_MODEL)))
        args.append(final_g.reshape(1, D_MODEL))
    out_specs, out_shapes = [row], [jax.ShapeDtypeStruct((n, D_MODEL), F32)]
    for w, widths in cast:
        assert sum(widths) == w.shape[1]
        rows = _cast_rows(w.shape[0], n_steps)
        last = w.shape[0] // rows - 1
        block = lambda width, last=last, rows=rows: pl.BlockSpec((rows, width), lambda i: (jnp.minimum(i, last), 0))
        in_specs.append(block(w.shape[1]))
        args.append(w)
        out_specs += [block(width) for width in widths]
        out_shapes += [jax.ShapeDtypeStruct((w.shape[0], width), BF16) for width in widths]
    if not cast:
        body_kernel = functools.partial(_ffn_kernel, final_norm=final_g is not None, n_cast=0)

        def streamed(x_hbm, *refs):
            weights, y_hbm = refs[:-1], refs[-1]
            pltpu.emit_pipeline(
                lambda x_ref, y_ref: body_kernel(x_ref, *weights, y_ref), grid=(n_steps,),
                in_specs=[pl.BlockSpec((TM_FFN, D_MODEL), lambda i: (i, 0), pipeline_mode=pl.Buffered(PROJ_IN_BUFFERS))],
                out_specs=[row],
            )(x_hbm, y_hbm)

        whole = lambda a: pl.BlockSpec(a.shape, lambda: (0,) * a.ndim, memory_space=pltpu.VMEM)
        y = pl.pallas_call(
            streamed,
            in_specs=[pl.BlockSpec(memory_space=pl.ANY)] + [whole(a) for a in args[1:]],
            out_specs=pl.BlockSpec(memory_space=pl.ANY),
            out_shape=out_shapes[0],
            compiler_params=pltpu.CompilerParams(vmem_limit_bytes=VMEM_LIMIT_BYTES),
            name="ffn_final" if final_g is not None else "ffn",
        )(*args)
        return y, ()
    outs = pl.pallas_call(
        functools.partial(_ffn_kernel, final_norm=final_g is not None, n_cast=len(cast)),
        grid=(n_steps,),
        in_specs=in_specs,
        out_specs=out_specs,
        out_shape=out_shapes,
        compiler_params=_params(1),
        name="ffn_final" if final_g is not None else "ffn",
    )(*args)
    return outs[0], outs[1:]


def _rows_by_residue(slab_ref, j, stage_ref, slot, n_rows, d):
    if d == 1:
        return [slab_ref[j]]
    if d == SPLIT_STRIDE:
        return [slab_ref[j, pl.ds(r, n_rows // d, stride=d), :] for r in range(d)]
    assert d == SPLIT_STRIDE * SPLIT_STRIDE
    for low in range(SPLIT_STRIDE):
        stage_ref[slot, low] = slab_ref[j, pl.ds(low, n_rows // SPLIT_STRIDE, stride=SPLIT_STRIDE), :]
    return [stage_ref[slot, r % SPLIT_STRIDE, pl.ds(r // SPLIT_STRIDE, n_rows // d, stride=SPLIT_STRIDE), :]
            for r in range(d)]


def _proj_kernel(x_ref, g_ref, w_ref, u_ref, *rest):
    qkv_refs, slabs = rest[:3 * N_PATTERNS], rest[3 * N_PATTERNS:]
    n_rows = TM_PROJ // PROJ_HALVES
    per_half = len(slabs) // PROJ_HALVES
    for half in range(PROJ_HALVES):
        base = half * n_rows
        su_ref, sq_ref, sk_ref, sv_ref, stage_ref = slabs[per_half * half:per_half * (half + 1)]
        un = _rmsnorm(x_ref[base:base + n_rows, :], g_ref[...]).astype(BF16)
        full = _dot(un, w_ref[...])
        slot = 0

        u = full[:, :SSM_WIDTH]
        chunks = slice(base // SSM_CHUNK, (base + n_rows) // SSM_CHUNK)
        for o in range(SSM_N_OCT):
            su_ref[o] = u[:, o * LANES:(o + 1) * LANES]
            for t, rows in enumerate(_rows_by_residue(su_ref, o, stage_ref, slot, n_rows, SSM_CHUNK)):
                lo = o * SSM_OCT_IN + t * LANES
                u_ref[chunks, lo:lo + LANES] = rows.astype(BF16)
            slot += 1

        for which, scr in enumerate((sq_ref, sk_ref, sv_ref)):
            lo = SSM_WIDTH + which * ATTN_WIDTH
            res = full[:, lo:lo + ATTN_WIDTH]
            if which == 0:
                res = res * (HEAD_DIM ** -0.5)
            for j in range(ATTN_WIDTH // LANES):
                scr[j] = res[:, j * LANES:(j + 1) * LANES]
            for g, (_, d) in enumerate(ATTN_PATTERNS):
                out = qkv_refs[which * N_PATTERNS + g]
                for jj in range(PATTERN_WIDTH // LANES):
                    j = g * (PATTERN_WIDTH // LANES) + jj
                    for r, rows in enumerate(_rows_by_residue(scr, j, stage_ref, slot, n_rows, d)):
                        out[0, r, base // d:(base + n_rows) // d, jj * LANES:(jj + 1) * LANES] = rows.astype(BF16)
                    slot += d == SPLIT_STRIDE * SPLIT_STRIDE


def _proj(x, norm_g, w_pre, bsz, seqlen):
    n = x.shape[0]
    tiles_per_seq = seqlen // TM_PROJ
    rows = lambda w: pl.BlockSpec((TM_PROJ, w), lambda i: (i, 0))
    qkv_specs, qkv_shapes = [], []
    for _ in range(3):
        for _, d in ATTN_PATTERNS:
            qkv_specs.append(pl.BlockSpec((1, d, TM_PROJ // d, PATTERN_WIDTH),
                                          lambda i: (i // tiles_per_seq, 0, i % tiles_per_seq, 0)))
            qkv_shapes.append(jax.ShapeDtypeStruct((bsz, d, seqlen // d, PATTERN_WIDTH), BF16))
    half_rows = TM_PROJ // PROJ_HALVES
    slabs = lambda k: pltpu.VMEM((k, half_rows, LANES), F32)
    n_staged = SSM_N_OCT + 3 * PATTERN_WIDTH // LANES
    stage = pltpu.VMEM((n_staged, SPLIT_STRIDE, half_rows // SPLIT_STRIDE, LANES), F32)
    n_out = 1 + 3 * N_PATTERNS
    out_specs = [pl.BlockSpec((TM_PROJ // SSM_CHUNK, SSM_ROW), lambda i: (i, 0))] + qkv_specs

    def streamed(x_hbm, g_ref, w_ref, *rest):
        out_hbm, scratch = rest[:n_out], rest[n_out:]
        body = lambda x_ref, *out_refs: _proj_kernel(x_ref, g_ref, w_ref, *out_refs, *scratch)
        pltpu.emit_pipeline(
            body, grid=(n // TM_PROJ,),
            in_specs=[pl.BlockSpec((TM_PROJ, D_MODEL), lambda i: (i, 0), pipeline_mode=pl.Buffered(PROJ_IN_BUFFERS))],
            out_specs=out_specs,
        )(x_hbm, *out_hbm)

    whole = lambda shape: pl.BlockSpec(shape, lambda: (0,) * len(shape), memory_space=pltpu.VMEM)
    outs = pl.pallas_call(
        streamed,
        in_specs=[pl.BlockSpec(memory_space=pl.ANY), whole((1, D_MODEL)), whole((D_MODEL, PRE_WIDTH))],
        out_specs=[pl.BlockSpec(memory_space=pl.ANY)] * n_out,
        out_shape=[jax.ShapeDtypeStruct((n // SSM_CHUNK, SSM_ROW), BF16)] + qkv_shapes,
        scratch_shapes=([slabs(SSM_N_OCT)] + [slabs(ATTN_WIDTH // LANES)] * 3 + [stage]) * PROJ_HALVES,
        compiler_params=pltpu.CompilerParams(vmem_limit_bytes=VMEM_LIMIT_BYTES),
        name="proj",
    )(x, norm_g.reshape(1, D_MODEL), w_pre)
    u = outs[0]
    q, k, v = (outs[1 + i * N_PATTERNS:1 + (i + 1) * N_PATTERNS] for i in range(3))
    return u, q, k, v


def _ssm_compact(a_re, a_im, log_dt, b_re, b_im, c_re, c_im, d_skip):
    t_len = SSM_CHUNK
    dt = jnp.exp(log_dt)[:, None]
    mag = jnp.exp(a_re * dt)
    ang = a_im * dt
    abar_re = mag * jnp.cos(ang)
    abar_im = mag * jnp.sin(ang)
    nr = abar_re - 1.0
    ni = abar_im
    den = a_re * a_re + a_im * a_im
    coef_re = (nr * a_re + ni * a_im) / den
    coef_im = (ni * a_re - nr * a_im) / den
    bbar_re = coef_re[..., None] * b_re - coef_im[..., None] * b_im
    bbar_im = coef_re[..., None] * b_im + coef_im[..., None] * b_re
    bt_re = jnp.swapaxes(bbar_re, -1, -2)
    bt_im = jnp.swapaxes(bbar_im, -1, -2)
    pw_re = [jnp.ones_like(abar_re)]
    pw_im = [jnp.zeros_like(abar_re)]
    for _ in range(t_len):
        pr, pi = pw_re[-1], pw_im[-1]
        pw_re.append(pr * abar_re - pi * abar_im)
        pw_im.append(pr * abar_im + pi * abar_re)
    rev_re = jnp.stack(pw_re[t_len - 1::-1])[:, :, None, :]
    rev_im = jnp.stack(pw_im[t_len - 1::-1])[:, :, None, :]
    pw_re = jnp.stack(pw_re)
    pw_im = jnp.stack(pw_im)
    n_oct, oct_, grp, st = SSM_N_OCT, SSM_OCT, SSM_GROUP, SSM_STATE

    def per_octet(re, im):
        both = jnp.concatenate([re, im], axis=-1)
        n = both.shape[0]
        return both.reshape(n, n_oct, oct_ * grp, 2 * st).transpose(1, 0, 2, 3).reshape(n_oct, n * LANES, 2 * st)

    into_c = per_octet(rev_re * bt_re[None] - rev_im * bt_im[None], rev_re * bt_im[None] + rev_im * bt_re[None])
    cp_re = c_re[None] * pw_re[:, :, None, :] - c_im[None] * pw_im[:, :, None, :]
    cp_im = c_re[None] * pw_im[:, :, None, :] + c_im[None] * pw_re[:, :, None, :]
    outof_c = per_octet(cp_re, -cp_im)
    decay = jnp.stack([pw_re[t_len].reshape(n_oct, SSM_OCT_STATE), pw_im[t_len].reshape(n_oct, SSM_OCT_STATE)],
                      axis=1)
    skip = d_skip.reshape(n_oct, 1, LANES)
    return into_c.astype(BF16), outof_c.astype(BF16), decay, skip


def _ssm_expanders():
    row = np.arange(LANES)
    col = np.arange(2 * SSM_OCT_STATE)
    spread = (row[:, None] // SSM_STATE == col[None, :] // SSM_OCT_STATE) & (
        row[:, None] % SSM_STATE == col[None, :] % SSM_STATE)
    keep = row[:, None] // SSM_GROUP == (col[None, :] % SSM_OCT_STATE) // SSM_STATE
    return jnp.asarray(spread, BF16), jnp.asarray(keep, F32)


def _ssm_kernel(x_ref, intoc_ref, outofc_ref, decay_ref, skip_ref, spread_ref, keep_ref, y_ref,
                toep_ref, into_ref, outof_ref, loc_ref, prev_ref, part_ref, *, n_chunks, n_seq):
    blk = lambda i: slice(i * LANES, (i + 1) * LANES)

    @pl.when(pl.program_id(1) == 0)
    def _build_octet_matrices():
        spread, keep = spread_ref[...], keep_ref[...]
        for s in range(SSM_CHUNK):
            into_ref[blk(s), :] = (_dot(intoc_ref[0, blk(s), :], spread) * keep).astype(BF16)
        for j in range(SSM_CHUNK + 1):
            outof_ref[blk(j), :] = (_dot(outofc_ref[0, blk(j), :], spread) * keep).astype(BF16)
        bbar = into_ref[blk(SSM_CHUNK - 1), :]
        diag = (lax.broadcasted_iota(jnp.int32, (LANES, LANES), 0)
                == lax.broadcasted_iota(jnp.int32, (LANES, LANES), 1))
        zeros = jnp.zeros((LANES, LANES), BF16)
        for j in range(SSM_CHUNK):
            lag = _dot_nt(bbar, outof_ref[blk(j), :])
            if j == 0:
                lag = lag + jnp.where(diag, skip_ref[0], 0.0)
            lag = lag.astype(BF16)
            for s in range(SSM_CHUNK - j):
                toep_ref[blk(s), blk(s + j)] = lag
            if j > 0:
                for s in range(j, SSM_CHUNK):
                    toep_ref[blk(s), blk(s - j)] = zeros

    loc_ref[...] = _dot(x_ref[...], into_ref[...])
    n_tiles = SSM_OCT_IN // MXU_TILE
    early = n_tiles - part_ref.shape[1] // MXU_TILE
    in_chunk = lambda j: _dot(x_ref[:, :(j + 1) * MXU_TILE], toep_ref[:(j + 1) * MXU_TILE, j * MXU_TILE:(j + 1) * MXU_TILE])
    for j in range(early, n_tiles):
        part_ref[:, (j - early) * MXU_TILE:(j - early + 1) * MXU_TILE] = in_chunk(j)
    ar = decay_ref[0, 0:1, :]
    ai = decay_ref[0, 1:2, :]

    def step(c, carry):
        new = []
        for q in range(n_seq):
            s_re, s_im = carry[2 * q], carry[2 * q + 1]
            row = pl.ds(q * n_chunks + c, 1)
            prev_ref[row, :SSM_OCT_STATE] = s_re
            prev_ref[row, SSM_OCT_STATE:] = s_im
            new.append(ar * s_re - ai * s_im + loc_ref[row, :SSM_OCT_STATE])
            new.append(ar * s_im + ai * s_re + loc_ref[row, SSM_OCT_STATE:])
        return tuple(new)

    zero = jnp.zeros((1, SSM_OCT_STATE), F32)
    lax.fori_loop(0, n_chunks, step, (zero,) * (2 * n_seq), unroll=True)
    prev = prev_ref[...].astype(BF16)
    for j in range(n_tiles):
        lo, hi = j * MXU_TILE, (j + 1) * MXU_TILE
        within = in_chunk(j) if j < early else part_ref[:, (j - early) * MXU_TILE:(j - early + 1) * MXU_TILE]
        y = within + _dot_nt(prev, outof_ref[lo + LANES:hi + LANES, :])
        y_ref[:, lo:hi] = jax.nn.gelu(y).astype(BF16)


def _ssm(u_rows, compact, bsz):
    into_c, outof_c, decay, skip = compact
    spread, keep = _ssm_expanders()
    n_chunks = u_rows.shape[0] // bsz
    n_seq = SSM_SEQ_PER_STEP
    per_oct = lambda *shape: pl.BlockSpec((1,) + shape, lambda o, b: (o,) + (0,) * len(shape))
    rows = pl.BlockSpec((n_seq * n_chunks, SSM_OCT_IN), lambda o, b: (b, o))
    wide = 2 * SSM_OCT_STATE
    return pl.pallas_call(
        functools.partial(_ssm_kernel, n_chunks=n_chunks, n_seq=n_seq),
        grid=(SSM_N_OCT, bsz // n_seq),
        in_specs=[rows, per_oct(SSM_OCT_IN, LANES), per_oct(SSM_OCT_IN + LANES, LANES),
                  per_oct(2, SSM_OCT_STATE), per_oct(1, LANES), _resident((LANES, wide)), _resident((LANES, wide))],
        out_specs=rows,
        out_shape=jax.ShapeDtypeStruct(u_rows.shape, BF16),
        scratch_shapes=[pltpu.VMEM((SSM_OCT_IN, SSM_OCT_IN), BF16),
                        pltpu.VMEM((SSM_OCT_IN, wide), BF16),
                        pltpu.VMEM((SSM_OCT_IN + LANES, wide), BF16),
                        pltpu.VMEM((n_seq * n_chunks, wide), F32),
                        pltpu.VMEM((n_seq * n_chunks, wide), F32),
                        pltpu.VMEM((n_seq * n_chunks, SSM_OCT_IN // 2), F32)],
        compiler_params=_params(2),
        name="ssm",
    )(u_rows, into_c, outof_c, decay, skip, spread, keep)


def _t5_buckets(dist):
    dist = np.asarray(dist, np.int32)
    max_exact = REL_BUCKETS // 2
    safe = np.maximum(dist, 1).astype(np.float32)
    large = max_exact + (np.log(safe / max_exact) / np.log(REL_MAX_DIST / max_exact)
                         * (REL_BUCKETS - max_exact)).astype(np.int32)
    large = np.minimum(large, REL_BUCKETS - 1)
    return np.where(dist < max_exact, dist, large).astype(np.int32)


def _band_bias(rel_table, pattern):
    dilation = ATTN_PATTERNS[pattern][1]
    buckets = _t5_buckets(np.arange(WINDOW_KEYS + 1) * dilation)
    table = rel_table[:, pattern * HEADS_PER_PATTERN:(pattern + 1) * HEADS_PER_PATTERN].T
    runs = []
    for b in np.unique(buckets):
        runs.append(jnp.broadcast_to(table[:, b:b + 1], (HEADS_PER_PATTERN, int(np.sum(buckets == b)))))
    per_step = jnp.concatenate(runs, axis=1)
    period = 3 * SUB_Q
    masked = lambda n: jnp.full((HEADS_PER_PATTERN, n), NEG_BIG, F32)
    w = jnp.concatenate([masked(SUB_Q - 1), per_step[:, ::-1], masked(period - SUB_Q - WINDOW_KEYS)], axis=1)
    skew = jnp.tile(w, (1, SUB_Q))[:, :SUB_Q * (period - 1)].reshape(HEADS_PER_PATTERN, SUB_Q, period - 1)
    bias = skew[:, :, SUB_Q - 1:3 * SUB_Q - 1]
    first = jnp.where((np.arange(2 * SUB_Q) >= SUB_Q)[None, None, :], bias, NEG_BIG)
    pairs = lambda b: b.reshape(HEADS_PER_PATTERN // 2, 2 * SUB_Q, 2 * SUB_Q)
    return jnp.stack([pairs(bias), pairs(first)])


def _attn_kernel(q_ref, kc_ref, kh_ref, vc_ref, vh_ref, bias_ref, o_ref):
    n_res, tq = q_ref.shape[1], q_ref.shape[2]
    first = jnp.where(pl.program_id(2) == 0, 1, 0)
    lane = lax.broadcasted_iota(jnp.int32, (SUB_Q, 2 * HEAD_DIM), 1)
    low_half = lane < HEAD_DIM
    stat_is_max = lax.broadcasted_iota(jnp.int32, (2 * SUB_Q, 2 * HEAD_DIM), 1) % HEAD_DIM < HEAD_DIM // 2
    for r in range(n_res):
        for j in range(tq // SUB_Q):
            rows = slice(j * SUB_Q, (j + 1) * SUB_Q)
            q = q_ref[0, r, rows, :]
            if j == 0:
                k = jnp.concatenate([kh_ref[0, r], kc_ref[0, r, rows, :]], axis=0)
                v = jnp.concatenate([vh_ref[0, r], vc_ref[0, r, rows, :]], axis=0)
            else:
                k = kc_ref[0, r, (j - 1) * SUB_Q:(j + 1) * SUB_Q, :]
                v = vc_ref[0, r, (j - 1) * SUB_Q:(j + 1) * SUB_Q, :]
            for pair in range(HEADS_PER_PATTERN // 2):
                cols = slice(pair * 2 * HEAD_DIM, (pair + 1) * 2 * HEAD_DIM)
                q2, k2, v2 = q[:, cols], k[:, cols], v[:, cols]
                zero = jnp.zeros_like(q2)
                qq = jnp.concatenate([jnp.where(low_half, q2, zero), jnp.where(low_half, zero, q2)], axis=0)
                s = _dot_nt(qq, k2) + (bias_ref[first, pair] if j == 0 else bias_ref[0, pair])
                m = jnp.max(s, axis=-1, keepdims=True)
                p = jnp.exp(s - m)
                l = jnp.sum(p, axis=-1, keepdims=True)
                o = _dot(p.astype(BF16), v2)
                wide = (2 * SUB_Q, 2 * HEAD_DIM)
                stat = jnp.where(stat_is_max, jnp.broadcast_to(m, wide), jnp.broadcast_to(l, wide))
                o_ref[0, r, rows, cols] = jnp.where(low_half, o[:SUB_Q], o[SUB_Q:])
                o_ref[0, r, rows, PATTERN_WIDTH + pair * 2 * HEAD_DIM:PATTERN_WIDTH + (pair + 1) * 2 * HEAD_DIM] = (
                    jnp.where(low_half, stat[:SUB_Q], stat[SUB_Q:]))


def _dilated_attention(q, k, v, rel_table, pattern):
    bsz, d, res_len, _ = q.shape
    tq = min(res_len, ATTN_STEP_ROWS)
    n_res = ATTN_STEP_ROWS // tq
    cur = pl.BlockSpec((1, n_res, tq, PATTERN_WIDTH), lambda b, r, i: (b, r, i, 0))
    halo = pl.BlockSpec((1, n_res, SUB_Q, PATTERN_WIDTH),
                        lambda b, r, i: (b, r, jnp.maximum(i * (tq // SUB_Q) - 1, 0), 0))
    return pl.pallas_call(
        _attn_kernel,
        grid=(bsz, d // n_res, res_len // tq),
        in_specs=[cur, cur, halo, cur, halo, _resident((2, HEADS_PER_PATTERN // 2, 2 * SUB_Q, 2 * SUB_Q))],
        out_specs=pl.BlockSpec((1, n_res, tq, 2 * PATTERN_WIDTH), lambda b, r, i: (b, r, i, 0)),
        out_shape=jax.ShapeDtypeStruct((bsz, d, res_len, 2 * PATTERN_WIDTH), F32),
        compiler_params=_params(3),
        name=f"attn_d{d}",
    )(q, k, k, v, v, _band_bias(rel_table, pattern))


def _memkv_kernel(mem_ref, g_ref, w_ref, k_ref, v_ref):
    mn = _rmsnorm(mem_ref[...], g_ref[...]).astype(BF16)
    k_ref[...] = _dot(mn, w_ref[:, :XATTN_WIDTH]).astype(BF16)
    v_ref[...] = _dot(mn, w_ref[:, XATTN_WIDTH:]).astype(BF16)


def _memkv(mem2d, norm_g, w_kv):
    rows = mem2d.shape[0]
    full = lambda *shape: pl.BlockSpec(shape, lambda i: (0,) * len(shape))
    return pl.pallas_call(
        _memkv_kernel,
        grid=(1,),
        in_specs=[full(rows, D_MODEL), full(1, D_MODEL), full(D_MODEL, 2 * XATTN_WIDTH)],
        out_specs=[full(rows, XATTN_WIDTH), full(rows, XATTN_WIDTH)],
        out_shape=[jax.ShapeDtypeStruct((rows, XATTN_WIDTH), BF16)] * 2,
        compiler_params=_params(1),
        name="memkv",
    )(mem2d, norm_g.reshape(1, D_MODEL), w_kv)


def _merge_kernel(x_ref, y_ref, o0_ref, o1_ref, o2_ref, mk_ref, mv_ref, g_ref, wpost_ref, wglu_ref,
                  wau_ref, wxu_ref, wout_ref, out_ref, ys_ref, s1_ref, s2_ref):
    x = x_ref[...]
    un = _rmsnorm(x, g_ref[...]).astype(BF16)

    def gate(index):
        lo = XATTN_WIDTH + index * D_MODEL
        return _sigmoid(_dot(un, wpost_ref[:, lo:lo + D_MODEL]))

    slabs = 2 * PATTERN_WIDTH // LANES
    for o_ref, s_ref in ((o1_ref, s1_ref), (o2_ref, s2_ref)):
        d = o_ref.shape[1]
        for r in range(d):
            for j in range(slabs):
                s_ref[j, pl.ds(r, TM_MERGE // d, stride=d), :] = o_ref[0, r, :, j * LANES:(j + 1) * LANES]
    half = slabs // 2
    is_max = lax.broadcasted_iota(jnp.int32, (TM_MERGE, LANES), 1) % HEAD_DIM < HEAD_DIM // 2

    def unpack(stat):
        return (jnp.where(is_max, stat, pltpu.roll(stat, HEAD_DIM // 2, axis=1)),
                jnp.where(is_max, pltpu.roll(stat, LANES - HEAD_DIM // 2, axis=1), stat))

    def pattern(read):
        out = jnp.concatenate([read(j) for j in range(half)], axis=-1)
        stats = [unpack(read(j)) for j in range(half, slabs)]
        return (out, jnp.concatenate([m for m, _ in stats], axis=-1),
                jnp.concatenate([l for _, l in stats], axis=-1))

    parts = [pattern(lambda j: o0_ref[0, 0, :, j * LANES:(j + 1) * LANES]),
             pattern(lambda j: s1_ref[j]), pattern(lambda j: s2_ref[j])]
    top = jnp.maximum(jnp.maximum(parts[0][1], parts[1][1]), parts[2][1])
    ws = [jnp.exp(m - top) for _, m, _ in parts]
    att = ((ws[0] * parts[0][0] + ws[1] * parts[1][0] + ws[2] * parts[2][0])
           / (ws[0] * parts[0][2] + ws[1] * parts[1][2] + ws[2] * parts[2][2]))
    att = att.astype(BF16)

    chunk_rows = TM_MERGE // SSM_CHUNK
    for o in range(SSM_N_OCT):
        for t in range(SSM_CHUNK):
            lo = o * SSM_OCT_IN + t * LANES
            ys_ref[o, pl.ds(t, chunk_rows, stride=SSM_CHUNK), :] = y_ref[:, lo:lo + LANES].astype(F32)
    ys = jnp.concatenate([ys_ref[o] for o in range(SSM_N_OCT)], axis=-1).astype(BF16)
    glu = _dot(ys, wglu_ref[:, :D_MODEL]) * _sigmoid(_dot(ys, wglu_ref[:, D_MODEL:]))
    merged = gate(0) * glu

    xq = _dot(un, wpost_ref[:, :XATTN_WIDTH]).astype(BF16)
    heads = []
    for h in range(XATTN_HEADS):
        cols = slice(h * XATTN_HEAD_DIM, (h + 1) * XATTN_HEAD_DIM)
        s = _dot_nt(xq[:, cols], mk_ref[0, :, cols]) * (XATTN_HEAD_DIM ** -0.5)
        p = jnp.exp(s - jnp.max(s, axis=-1, keepdims=True))
        heads.append(_dot(p.astype(BF16), mv_ref[0, :, cols]) / jnp.sum(p, axis=-1, keepdims=True))
    xo = jnp.concatenate(heads, axis=-1).astype(BF16)
    merged = merged + gate(2) * _dot(xo, wxu_ref[...])

    merged = merged + gate(1) * _dot(att, wau_ref[...])

    out_ref[...] = x + _dot(merged.astype(BF16), wout_ref[...])


def _merge(x, y_rows, attn_outs, mk, mv, norm_g, w_post, w_glu, w_au, w_xu, w_out, seqlen):
    n = x.shape[0]
    tiles_per_seq = seqlen // TM_MERGE
    rows = lambda w: pl.BlockSpec((TM_MERGE, w), lambda i: (i, 0))
    mem = pl.BlockSpec((1, MEM_LEN, XATTN_WIDTH), lambda i: (i // tiles_per_seq, 0, 0))
    attn_specs = [pl.BlockSpec((1, d, TM_MERGE // d, 2 * PATTERN_WIDTH),
                               lambda i: (i // tiles_per_seq, 0, i % tiles_per_seq, 0))
                  for _, d in ATTN_PATTERNS]
    slabs = lambda k: pltpu.VMEM((k, TM_MERGE, LANES), F32)
    return pl.pallas_call(
        _merge_kernel,
        grid=(n // TM_MERGE,),
        in_specs=[rows(D_MODEL), pl.BlockSpec((TM_MERGE // SSM_CHUNK, SSM_ROW), lambda i: (i, 0))] + attn_specs
                 + [mem, mem, _resident((1, D_MODEL)), _resident((D_MODEL, POST_WIDTH)),
                    _resident((SSM_WIDTH, 2 * D_MODEL)), _resident((PATTERN_WIDTH, D_MODEL)),
                    _resident((XATTN_WIDTH, D_MODEL)), _resident((D_MODEL, D_MODEL))],
        out_specs=rows(D_MODEL),
        out_shape=jax.ShapeDtypeStruct((n, D_MODEL), F32),
        scratch_shapes=[slabs(SSM_N_OCT), slabs(2 * PATTERN_WIDTH // LANES), slabs(2 * PATTERN_WIDTH // LANES)],
        compiler_params=_params(1),
        name="merge",
    )(x, y_rows, *attn_outs, mk, mv, norm_g.reshape(1, D_MODEL), w_post, w_glu, w_au, w_xu, w_out)


def _layer(x, mem, rel_table, ffn1_norm, ffn1_w_in, ffn1_w_down, mix_norm, w_in,
           ssm_a_re, ssm_a_im, ssm_log_dt, ssm_b_re, ssm_b_im, ssm_c_re, ssm_c_im, ssm_d, ssm_w_glu,
           attn_w_up, mem_norm, xattn_w_kv, xattn_w_up, w_out, ffn2_norm, ffn2_w_in, ffn2_w_down,
           final_norm, bsz, seqlen):
    cast = [(w_in, (PRE_WIDTH, POST_WIDTH)), (ssm_w_glu, (2 * D_MODEL,)), (attn_w_up, (D_MODEL,)),
            (xattn_w_kv, (2 * XATTN_WIDTH,)), (xattn_w_up, (D_MODEL,)), (w_out, (D_MODEL,)),
            (ffn2_w_in, (2 * D_FF,)), (ffn2_w_down, (D_MODEL,))]
    x, (w_pre, w_post, w_glu, w_au, w_kv, w_xu, w_o, w2_in, w2_down) = _ffn(
        x, ffn1_norm, ffn1_w_in.astype(BF16), ffn1_w_down.astype(BF16), cast=cast)
    u_rows, q, k, v = _proj(x, mix_norm, w_pre, bsz, seqlen)
    compact = _ssm_compact(ssm_a_re, ssm_a_im, ssm_log_dt, ssm_b_re, ssm_b_im, ssm_c_re, ssm_c_im, ssm_d)
    y_rows = _ssm(u_rows, compact, bsz)
    attn_outs = [_dilated_attention(q[g], k[g], v[g], rel_table, g) for g in range(N_PATTERNS)]
    mk, mv = _memkv(mem.reshape(bsz * MEM_LEN, D_MODEL), mem_norm, w_kv)
    mk = mk.reshape(bsz, MEM_LEN, XATTN_WIDTH)
    mv = mv.reshape(bsz, MEM_LEN, XATTN_WIDTH)
    x = _merge(x, y_rows, attn_outs, mk, mv, mix_norm, w_post, w_glu, w_au, w_xu, w_o, seqlen)
    return _ffn(x, ffn2_norm, w2_in, w2_down, final_g=final_norm)[0]


def kernel(x, mem, ffn1_norm, ffn1_w_in, ffn1_w_down, mix_norm, w_in, ssm_a_re, ssm_a_im, ssm_log_dt,
           ssm_b_re, ssm_b_im, ssm_c_re, ssm_c_im, ssm_d, ssm_w_glu, rel_table, attn_w_up, mem_norm,
           xattn_w_kv, xattn_w_up, w_out, ffn2_norm, ffn2_w_in, ffn2_w_down, final_norm):
    bsz, seqlen, _ = x.shape
    assert ffn1_norm.shape[0] == 1, "single-layer trunk"
    assert x.shape[2] == D_MODEL and mem.shape == (bsz, MEM_LEN, D_MODEL)
    assert w_in.shape[1:] == (D_MODEL, PRE_WIDTH + POST_WIDTH) and ffn1_w_in.shape[1:] == (D_MODEL, 2 * D_FF)
    assert ssm_b_re.shape[1:] == (SSM_WIDTH // SSM_GROUP, SSM_STATE, SSM_GROUP)
    assert rel_table.shape == (REL_BUCKETS, N_PATTERNS * HEADS_PER_PATTERN)
    assert seqlen % ATTN_STEP_ROWS == 0 and (seqlen // ATTN_PATTERNS[-1][1]) % SUB_Q == 0
    assert seqlen % TM_PROJ == 0 and seqlen % TM_MERGE == 0 and (bsz * seqlen) % TM_FFN == 0
    assert bsz % SSM_SEQ_PER_STEP == 0
    h = _layer(x.reshape(bsz * seqlen, D_MODEL), mem, rel_table, ffn1_norm[0], ffn1_w_in[0], ffn1_w_down[0],
               mix_norm[0], w_in[0], ssm_a_re[0], ssm_a_im[0], ssm_log_dt[0], ssm_b_re[0], ssm_b_im[0],
               ssm_c_re[0], ssm_c_im[0], ssm_d[0], ssm_w_glu[0], attn_w_up[0], mem_norm[0], xattn_w_kv[0],
               xattn_w_up[0], w_out[0], ffn2_norm[0], ffn2_w_in[0], ffn2_w_down[0], final_norm,
               bsz, seqlen)
    return h.reshape(bsz, seqlen, D_MODEL)
```
